```python
import math
import jax, jax.numpy as jnp
from jax import lax
import numpy as np

D_MODEL = 2048
BATCH = 2
SEQ = 8192
DEPTH = 4

HEAD_DIM = 64
N_MIX_HEADS = D_MODEL // HEAD_DIM
A_HEADS = N_MIX_HEADS // 4
A_KV_HEADS = A_HEADS // 4
B_HEADS = N_MIX_HEADS // 4
C_HEADS = N_MIX_HEADS - A_HEADS - B_HEADS
C_KV_HEADS = C_HEADS // 4
A_W = A_HEADS * HEAD_DIM
A_KV_W = A_KV_HEADS * HEAD_DIM
B_W = B_HEADS * HEAD_DIM
C_W = C_HEADS * HEAD_DIM
C_KV_W = C_KV_HEADS * HEAD_DIM
MIX_WIDTH = A_W + B_W + C_W
IN_SIZES = [A_W, A_KV_W, A_KV_W, B_W, B_W, B_W, C_W, C_KV_W, C_KV_W]
IN_WIDTH = sum(IN_SIZES)
IN_SPLITS = [int(v) for v in np.cumsum(IN_SIZES)[:-1]]

WINDOW = 128
A_BLOCK = 128
T5_BUCKETS = 32
T5_MAX_DIST = 128
GRID_W = 64
NA_ROWS_MAX = 8
NA_COLS = 16
NA_QCOLS = 16
C_BLOCK = 128
ROPE_THETA = 10000.0
D_FF = 4 * D_MODEL
EPS = 1e-6
MASK_VALUE = -1e30

kernel_name = "hymba_style_hybrid_encoder"


def rmsnorm(x, g):
    xf = x.astype(jnp.float32)
    y = xf * lax.rsqrt(jnp.mean(xf * xf, axis=-1, keepdims=True) + EPS)
    return (y * g.astype(jnp.float32)).astype(x.dtype)


def t5_bucket(rel):
    nb = T5_BUCKETS // 2
    max_exact = nb // 2
    base = jnp.where(rel > 0, nb, 0)
    n = jnp.abs(rel)
    nf = jnp.maximum(n, 1).astype(jnp.float32)
    large = max_exact + (jnp.log(nf / max_exact) / math.log(T5_MAX_DIST / max_exact)
                         * (nb - max_exact)).astype(jnp.int32)
    large = jnp.minimum(large, nb - 1)
    return base + jnp.where(n < max_exact, n, large)


def window_attention(q, k, v, sink, t5_table):
    bsz, s_len = q.shape[0], q.shape[1]
    nb = s_len // A_BLOCK
    grp = A_HEADS // A_KV_HEADS
    qb = q.reshape(bsz, nb, A_BLOCK, A_KV_HEADS, grp, HEAD_DIM)

    def kv_blocks(t):
        tp = jnp.pad(t, ((0, 0), (A_BLOCK, A_BLOCK), (0, 0), (0, 0)))
        parts = [tp[:, o:o + s_len].reshape(bsz, nb, A_BLOCK, A_KV_HEADS, HEAD_DIM)
                 for o in (0, A_BLOCK, 2 * A_BLOCK)]
        return jnp.concatenate(parts, axis=2)

    kb, vb = kv_blocks(k), kv_blocks(v)
    qi = jnp.arange(A_BLOCK)[:, None]
    kj = jnp.arange(3 * A_BLOCK)[None, :]
    rel = kj - A_BLOCK - qi
    bias = t5_table[t5_bucket(rel)]
    bias = bias.transpose(2, 0, 1).reshape(A_KV_HEADS, grp, A_BLOCK, 3 * A_BLOCK).astype(jnp.float32)
    key_pos = jnp.arange(nb)[:, None] * A_BLOCK - A_BLOCK + jnp.arange(3 * A_BLOCK)[None, :]
    valid = ((jnp.abs(rel) <= WINDOW)[None]
             & ((key_pos >= 0) & (key_pos < s_len))[:, None, :])
    scale = HEAD_DIM ** -0.5
    s = jnp.einsum('bnqgrd,bnkgd->bngrqk', qb, kb).astype(jnp.float32) * scale + bias
    s = jnp.where(valid[None, :, None, None], s, MASK_VALUE)
    sink_l = jnp.broadcast_to(sink.astype(jnp.float32).reshape(1, 1, A_KV_HEADS, grp, 1, 1),
                              s.shape[:-1] + (1,))
    p = jax.nn.softmax(jnp.concatenate([s, sink_l], axis=-1), axis=-1)[..., :-1]
    o = jnp.einsum('bngrqk,bnkgd->bnqgrd', p.astype(v.dtype), vb)
    return o.reshape(bsz, s_len, A_W)


def neighborhood_attention(q, k, v, rpb):
    bsz, s_len = q.shape[0], q.shape[1]
    rows = s_len // GRID_W
    kr = min(NA_ROWS_MAX, rows)
    ncb = GRID_W // NA_QCOLS
    kbw = min(GRID_W, NA_COLS + NA_QCOLS)
    q5 = q.reshape(bsz, rows, GRID_W, B_HEADS, HEAD_DIM)
    k5 = k.reshape(bsz, rows, GRID_W, B_HEADS, HEAD_DIM)
    v5 = v.reshape(bsz, rows, GRID_W, B_HEADS, HEAD_DIM)
    r = jnp.arange(rows)
    row_start = jnp.clip(r - kr // 2, 0, rows - kr)
    row_idx = row_start[:, None] + jnp.arange(kr)[None, :]
    dr_idx = row_idx - r[:, None] + (NA_ROWS_MAX - 1)
    c = jnp.arange(GRID_W).reshape(ncb, NA_QCOLS)
    col_start = jnp.clip(c - NA_COLS // 2, 0, GRID_W - NA_COLS)
    kblk_start = jnp.clip(jnp.arange(ncb) * NA_QCOLS - NA_COLS // 2, 0, GRID_W - kbw)
    kcols = kblk_start[:, None] + jnp.arange(kbw)[None, :]
    kc = kcols[:, None, :]
    cs = col_start[:, :, None]
    col_valid = (kc >= cs) & (kc < cs + NA_COLS)
    dc_idx = jnp.clip(kc - c[:, :, None] + NA_COLS - 1, 0, 2 * NA_COLS - 2)
    scale = HEAD_DIM ** -0.5

    def row_block(args):
        q_r, ridx, dridx = args
        kg = k5[:, ridx][:, :, kcols]
        vg = v5[:, ridx][:, :, kcols]
        qg = q_r.reshape(bsz, ncb, NA_QCOLS, B_HEADS, HEAD_DIM)
        s = jnp.einsum('bmqhd,bimjhd->bmhqij', qg, kg).astype(jnp.float32) * scale
        bias = rpb[:, dridx][:, :, dc_idx]
        s = s + bias.transpose(2, 0, 3, 1, 4).astype(jnp.float32)[None]
        s = jnp.where(col_valid[None, :, None, :, None, :], s, MASK_VALUE)
        p = jax.nn.softmax(s.reshape(bsz, ncb, B_HEADS, NA_QCOLS, kr * kbw), axis=-1)
        p = p.reshape(s.shape).astype(v.dtype)
        o = jnp.einsum('bmhqij,bimjhd->bmqhd', p, vg)
        return o.reshape(bsz, GRID_W, B_HEADS, HEAD_DIM)

    o = lax.map(row_block, (q5.transpose(1, 0, 2, 3, 4), row_idx, dr_idx))
    return o.transpose(1, 0, 2, 3, 4).reshape(bsz, s_len, B_W)


def rope_axis(x, ang):
    x1, x2 = jnp.split(x, 2, axis=-1)
    cos = jnp.cos(ang)[None, :, None, :]
    sin = jnp.sin(ang)[None, :, None, :]
    return jnp.concatenate([x1 * cos - x2 * sin, x2 * cos + x1 * sin], axis=-1).astype(x.dtype)


def axial_rope(x, ang_row, ang_col):
    x_row, x_col = jnp.split(x, 2, axis=-1)
    return jnp.concatenate([rope_axis(x_row, ang_row), rope_axis(x_col, ang_col)], axis=-1)


def axial_global_attention(q, k, v, q_gain, k_gain):
    bsz, s_len = q.shape[0], q.shape[1]
    grp = C_HEADS // C_KV_HEADS
    t = jnp.arange(s_len)
    row = (t // GRID_W).astype(jnp.float32)
    col = (t % GRID_W).astype(jnp.float32)
    axis_dim = HEAD_DIM // 2
    freqs = ROPE_THETA ** (-jnp.arange(0, axis_dim, 2, dtype=jnp.float32) / axis_dim)
    ang_row = row[:, None] * freqs[None, :]
    ang_col = col[:, None] * freqs[None, :]
    q = axial_rope(rmsnorm(q, q_gain), ang_row, ang_col)
    k = axial_rope(rmsnorm(k, k_gain), ang_row, ang_col)
    nb = s_len // C_BLOCK
    qb = q.reshape(bsz, nb, C_BLOCK, C_KV_HEADS, grp, HEAD_DIM).transpose(1, 0, 2, 3, 4, 5)
    scale = HEAD_DIM ** -0.5

    def block(q_blk):
        s = jnp.einsum('bqgrd,bkgd->bgrqk', q_blk, k).astype(jnp.float32) * scale
        p = jax.nn.softmax(s, axis=-1).astype(v.dtype)
        return jnp.einsum('bgrqk,bkgd->bqgrd', p, v)

    o = lax.map(block, qb)
    return o.transpose(1, 0, 2, 3, 4, 5).reshape(bsz, s_len, C_W)


def setup_inputs(seed: int = 0) -> dict:
    key = jax.random.key(seed)
    ks = jax.random.split(key, 16)
    f32 = jnp.float32

    def nrm(k, shape, scale):
        return jax.random.normal(k, shape, f32) * scale

    return {
        "x": nrm(ks[0], (BATCH, SEQ, D_MODEL), 1.0),
        "norm_mix": 1.0 + nrm(ks[1], (DEPTH, D_MODEL), 0.02),
        "w_in": nrm(ks[2], (DEPTH, D_MODEL, IN_WIDTH), D_MODEL ** -0.5),
        "a_sink": nrm(ks[3], (DEPTH, A_HEADS), 0.5),
        "t5_table": nrm(ks[4], (T5_BUCKETS, A_HEADS), 0.5),
        "b_rpb": nrm(ks[5], (DEPTH, B_HEADS, 2 * NA_ROWS_MAX - 1, 2 * NA_COLS - 1), 0.5),
        "c_q_gain": 1.0 + nrm(ks[6], (DEPTH, HEAD_DIM), 0.02),
        "c_k_gain": 1.0 + nrm(ks[7], (DEPTH, HEAD_DIM), 0.02),
        "out_gain_a": 1.0 + nrm(ks[8], (DEPTH, A_W), 0.02),
        "out_gain_b": 1.0 + nrm(ks[9], (DEPTH, B_W), 0.02),
        "out_gain_c": 1.0 + nrm(ks[10], (DEPTH, C_W), 0.02),
        "w_o": nrm(ks[11], (DEPTH, MIX_WIDTH, D_MODEL), MIX_WIDTH ** -0.5),
        "norm_mlp": 1.0 + nrm(ks[12], (DEPTH, D_MODEL), 0.02),
        "w_up": nrm(ks[13], (DEPTH, D_MODEL, D_FF), D_MODEL ** -0.5),
        "w_down": nrm(ks[14], (DEPTH, D_FF, D_MODEL), D_FF ** -0.5),
        "norm_final": 1.0 + nrm(ks[15], (D_MODEL,), 0.02),
    }


def reference(x, norm_mix, w_in, a_sink, t5_table, b_rpb, c_q_gain, c_k_gain,
              out_gain_a, out_gain_b, out_gain_c, w_o, norm_mlp, w_up, w_down, norm_final):
    bsz, s_len = x.shape[0], x.shape[1]
    for l in range(DEPTH):
        h = rmsnorm(x, norm_mix[l])
        proj = jnp.einsum('bsd,de->bse', h, w_in[l])
        qa, ka, va, qb, kb, vb, qc, kc, vc = jnp.split(proj, IN_SPLITS, axis=-1)
        oa = window_attention(qa.reshape(bsz, s_len, A_HEADS, HEAD_DIM),
                              ka.reshape(bsz, s_len, A_KV_HEADS, HEAD_DIM),
                              va.reshape(bsz, s_len, A_KV_HEADS, HEAD_DIM),
                              a_sink[l], t5_table)
        ob = neighborhood_attention(qb.reshape(bsz, s_len, B_HEADS, HEAD_DIM),
                                    kb.reshape(bsz, s_len, B_HEADS, HEAD_DIM),
                                    vb.reshape(bsz, s_len, B_HEADS, HEAD_DIM),
                                    b_rpb[l])
        oc = axial_global_attention(qc.reshape(bsz, s_len, C_HEADS, HEAD_DIM),
                                    kc.reshape(bsz, s_len, C_KV_HEADS, HEAD_DIM),
                                    vc.reshape(bsz, s_len, C_KV_HEADS, HEAD_DIM),
                                    c_q_gain[l], c_k_gain[l])
        mix = jnp.concatenate([rmsnorm(oa, out_gain_a[l]),
                               rmsnorm(ob, out_gain_b[l]),
                               rmsnorm(oc, out_gain_c[l])], axis=-1)
        x = x + jnp.einsum('bse,ed->bsd', mix, w_o[l])
        h = rmsnorm(x, norm_mlp[l])
        u = jax.nn.relu(jnp.einsum('bsd,df->bsf', h, w_up[l]))
        x = x + jnp.einsum('bsf,fd->bsd', u * u, w_down[l])
    return rmsnorm(x, norm_final)
```

```python
import functools
import math

import numpy as np
import jax
import jax.numpy as jnp
from jax import lax
from jax.experimental import pallas as pl
from jax.experimental.pallas import tpu as pltpu

HEAD_DIM = 64
LANES = 128
WINDOW = 128
A_BLOCK = 128
T5_BUCKETS = 32
T5_MAX_DIST = 128
GRID_W = 64
NA_ROWS = 8
NA_COLS = 16
ROPE_THETA = 10000.0
EPS = 1e-6
MASK_VALUE = -1e30
VMEM_LIMIT = 56 * 1024 * 1024

F32 = jnp.float32
BF16 = jnp.bfloat16

A_PERM = (0, 4, 1, 5, 2, 6, 3, 7)
C_PERM = (0, 4, 1, 5, 2, 6, 3, 7, 8, 12, 9, 13, 10, 14, 11, 15)


def _params(*sem):
    return pltpu.CompilerParams(dimension_semantics=sem, vmem_limit_bytes=VMEM_LIMIT)


def _rms_scale(x):
    return lax.rsqrt(jnp.mean(x * x, axis=-1, keepdims=True) + EPS)


def _lo_lanes():
    return lax.broadcasted_iota(jnp.int32, (1, LANES), 1) < HEAD_DIM


def _inproj_kernel(x_ref, g_ref, w_ref, o_ref, h_ref):
    @pl.when(pl.program_id(1) == 0)
    def _():
        x = x_ref[...]
        h_ref[...] = (x * _rms_scale(x) * g_ref[...]).astype(BF16)

    o_ref[...] = jnp.dot(h_ref[...], w_ref[...], preferred_element_type=F32).astype(o_ref.dtype)


def _inproj(x, gain, w, *, tm, tn):
    n, d = x.shape
    e = w.shape[1]
    return pl.pallas_call(
        _inproj_kernel,
        grid=(n // tm, e // tn),
        in_specs=[pl.BlockSpec((tm, d), lambda i, j: (i, 0)),
                  pl.BlockSpec((1, d), lambda i, j: (0, 0)),
                  pl.BlockSpec((d, tn), lambda i, j: (0, j))],
        out_specs=pl.BlockSpec((tm, tn), lambda i, j: (i, j)),
        out_shape=jax.ShapeDtypeStruct((n, e), BF16),
        scratch_shapes=[pltpu.VMEM((tm, d), BF16)],
        compiler_params=_params("parallel", "arbitrary"),
        name="inproj",
    )(x, gain.reshape(1, d), w)


def _attn_a_kernel(sink_ref, q_ref, k_ref, v_ref, bias_ref, o_ref, *, tq, s_len):
    i = pl.program_id(1)
    lo = _lo_lanes()
    qi = lax.broadcasted_iota(jnp.int32, (A_BLOCK, 3 * A_BLOCK), 0)
    kj = lax.broadcasted_iota(jnp.int32, (A_BLOCK, 3 * A_BLOCK), 1)
    in_window = jnp.abs(kj - A_BLOCK - qi) <= WINDOW
    for j in range(tq // A_BLOCK):
        p0 = i * tq + j * A_BLOCK
        k_parts, v_parts = [], []
        for o in (-1, 0, 1):
            st = pl.multiple_of(jnp.clip(p0 + o * A_BLOCK, 0, s_len - A_BLOCK), A_BLOCK)
            k_parts.append(k_ref[pl.ds(st, A_BLOCK), :])
            v_parts.append(v_ref[pl.ds(st, A_BLOCK), :])
        k3 = jnp.concatenate(k_parts, axis=0)
        v3 = jnp.concatenate(v_parts, axis=0)
        key_pos = p0 - A_BLOCK + kj
        valid = in_window & (key_pos >= 0) & (key_pos < s_len)
        rows = slice(j * A_BLOCK, (j + 1) * A_BLOCK)
        q_blocks = [q_ref[rows, r * LANES:(r + 1) * LANES] for r in range(4)]
        halves = []
        for half in (0, 1):
            keep = lo if half == 0 else jnp.logical_not(lo)
            qs = jnp.concatenate([jnp.where(keep, qb, jnp.zeros_like(qb)) for qb in q_blocks], axis=0)
            s = lax.dot_general(qs, k3, (((1,), (1,)), ((), ())), preferred_element_type=F32)
            s = s.reshape(4, A_BLOCK, 3 * A_BLOCK) + bias_ref[half]
            s = jnp.where(valid[None], s, MASK_VALUE)
            sink = jnp.stack([jnp.full((A_BLOCK, 1), sink_ref[2 * r + half], F32) for r in range(4)])
            m = jnp.maximum(jnp.max(s, axis=-1, keepdims=True), sink)
            e = jnp.exp(s - m)
            den = jnp.sum(e, axis=-1, keepdims=True) + jnp.exp(sink - m)
            pv = jnp.dot(e.reshape(4 * A_BLOCK, 3 * A_BLOCK).astype(BF16), v3,
                         preferred_element_type=F32)
            halves.append(pv.reshape(4, A_BLOCK, LANES) * (1.0 / den))
        for r in range(4):
            o_ref[rows, r * LANES:(r + 1) * LANES] = jnp.where(lo, halves[0][r], halves[1][r])


def _attn_a(proj, sink_p, bias_p, *, bsz, s_len, tq):
    n = proj.shape[0]
    nq = s_len // tq
    kern = functools.partial(_attn_a_kernel, tq=tq, s_len=s_len)
    return pl.pallas_call(
        kern,
        grid=(bsz, nq),
        in_specs=[pl.BlockSpec(memory_space=pltpu.SMEM),
                  pl.BlockSpec((tq, 4 * LANES), lambda b, i: (b * nq + i, 0)),
                  pl.BlockSpec((s_len, LANES), lambda b, i: (b, 4)),
                  pl.BlockSpec((s_len, LANES), lambda b, i: (b, 5)),
                  pl.BlockSpec((2, 4, A_BLOCK, 3 * A_BLOCK), lambda b, i: (0, 0, 0, 0))],
        out_specs=pl.BlockSpec((tq, 4 * LANES), lambda b, i: (b * nq + i, 0)),
        out_shape=jax.ShapeDtypeStruct((n, 4 * LANES), F32),
        compiler_params=_params("parallel", "arbitrary"),
        name="attn_a",
    )(sink_p, proj, proj, proj, bias_p)


def _attn_b_kernel(q_ref, k_ref, v_ref, t_ref, o_ref, *, rblk, rows):
    r0 = pl.program_id(2) * rblk
    lo = _lo_lanes()
    nkeys = NA_ROWS * GRID_W
    cq = lax.broadcasted_iota(jnp.int32, (GRID_W, nkeys), 0)
    ck = lax.broadcasted_iota(jnp.int32, (GRID_W, nkeys), 1) % GRID_W
    cs = jnp.clip(cq - NA_COLS // 2, 0, GRID_W - NA_COLS)
    col_valid = (ck >= cs) & (ck < cs + NA_COLS)
    for a in range(rblk):
        rq = r0 + a
        rs = jnp.clip(rq - NA_ROWS // 2, 0, rows - NA_ROWS)
        off = rs - rq + (NA_ROWS - 1)
        st = pl.multiple_of(rs * GRID_W, GRID_W)
        kwin = k_ref[pl.ds(st, nkeys), :]
        vwin = v_ref[pl.ds(st, nkeys), :]
        qrow = q_ref[a * GRID_W:(a + 1) * GRID_W, :]
        zero = jnp.zeros_like(qrow)
        qs = jnp.concatenate([jnp.where(lo, qrow, zero), jnp.where(lo, zero, qrow)], axis=0)
        s = lax.dot_general(qs, kwin, (((1,), (1,)), ((), ())), preferred_element_type=F32)
        bias = jnp.concatenate(
            [jnp.concatenate([t_ref[half, off + 2 * j] for j in range(NA_ROWS // 2)], axis=1)
             for half in (0, 1)], axis=0)
        s = s.reshape(2, GRID_W, nkeys) + bias.reshape(2, GRID_W, nkeys)
        s = jnp.where(col_valid[None], s, MASK_VALUE)
        m = jnp.max(s, axis=-1, keepdims=True)
        e = jnp.exp(s - m)
        den = jnp.sum(e, axis=-1, keepdims=True)
        pv = jnp.dot(e.reshape(2 * GRID_W, nkeys).astype(BF16), vwin, preferred_element_type=F32)
        pv = pv.reshape(2, GRID_W, LANES) * (1.0 / den)
        o_ref[a * GRID_W:(a + 1) * GRID_W, :] = jnp.where(lo, pv[0], pv[1])


def _attn_b(proj, t_pair, *, bsz, s_len, rblk):
    n = proj.shape[0]
    rows = s_len // GRID_W
    nrb = rows // rblk
    kern = functools.partial(_attn_b_kernel, rblk=rblk, rows=rows)
    tq = rblk * GRID_W
    return pl.pallas_call(
        kern,
        grid=(bsz, 4, nrb),
        in_specs=[pl.BlockSpec((tq, LANES), lambda b, h, i: (b * nrb + i, 6 + h)),
                  pl.BlockSpec((s_len, LANES), lambda b, h, i: (b, 10 + h)),
                  pl.BlockSpec((s_len, LANES), lambda b, h, i: (b, 14 + h)),
                  pl.BlockSpec((2, 2 * NA_ROWS - 2, GRID_W, LANES), lambda b, h, i: (h, 0, 0, 0))],
        out_specs=pl.BlockSpec((tq, LANES), lambda b, h, i: (b * nrb + i, h)),
        out_shape=jax.ShapeDtypeStruct((n, 4 * LANES), F32),
        compiler_params=_params("parallel", "parallel", "arbitrary"),
        name="attn_b",
    )(proj, proj, proj, t_pair)


def _c_prep_kernel(x_ref, g_ref, cos_ref, sdn_ref, sup_ref, o_ref):
    x = x_ref[...].astype(F32)
    lo = _lo_lanes()
    xx = x * x
    ms_lo = jnp.sum(jnp.where(lo, xx, 0.0), axis=-1, keepdims=True)
    ms_hi = jnp.sum(jnp.where(lo, 0.0, xx), axis=-1, keepdims=True)
    ms = jnp.where(lo, ms_lo, ms_hi) * (1.0 / HEAD_DIM)
    y = x * lax.rsqrt(ms + EPS) * g_ref[0]
    quarter = HEAD_DIM // 4
    y_dn = pltpu.roll(y, LANES - quarter, 1)
    y_up = pltpu.roll(y, quarter, 1)
    o_ref[...] = (y * cos_ref[...] + y_dn * sdn_ref[...] + y_up * sup_ref[...]).astype(o_ref.dtype)


def _c_prep(proj, gains, cos_t, sdn_t, sup_t, *, s_len, tr, first):
    n = proj.shape[0]
    nblk = gains.shape[0]
    npos = s_len // tr
    pos_spec = pl.BlockSpec((tr, LANES), lambda i, j: (i % npos, 0))
    return pl.pallas_call(
        _c_prep_kernel,
        grid=(n // tr, nblk),
        in_specs=[pl.BlockSpec((tr, LANES), lambda i, j: (i, first + j)),
                  pl.BlockSpec((1, 1, LANES), lambda i, j: (j, 0, 0)),
                  pos_spec, pos_spec, pos_spec],
        out_specs=pl.BlockSpec((tr, LANES), lambda i, j: (i, j)),
        out_shape=jax.ShapeDtypeStruct((n, nblk * LANES), BF16),
        compiler_params=_params("parallel", "arbitrary"),
        name="c_prep",
    )(proj, gains, cos_t, sdn_t, sup_t)


def _attn_c_kernel(q_ref, k_ref, v_ref, o_ref, qs_ref, m_ref, l_ref, acc_ref, *, tq, tk, s_len):
    lo = _lo_lanes()
    for r in range(4):
        qb = q_ref[:, r * LANES:(r + 1) * LANES]
        zero = jnp.zeros_like(qb)
        qs_ref[r * tq:(r + 1) * tq, :] = jnp.where(lo, qb, zero)
        qs_ref[(4 + r) * tq:(5 + r) * tq, :] = jnp.where(lo, zero, qb)
    m_ref[...] = jnp.full(m_ref.shape, -jnp.inf, F32)
    l_ref[...] = jnp.zeros(l_ref.shape, F32)
    acc_ref[...] = jnp.zeros(acc_ref.shape, F32)

    def step(t, carry):
        st = pl.multiple_of(t * tk, tk)
        kt = k_ref[pl.ds(st, tk), :]
        vt = v_ref[pl.ds(st, tk), :]
        s = lax.dot_general(qs_ref[...], kt, (((1,), (1,)), ((), ())), preferred_element_type=F32)
        m_old = m_ref[...]
        m_new = jnp.maximum(m_old, jnp.max(s, axis=-1, keepdims=True))
        alpha = jnp.exp(m_old - m_new)
        p = jnp.exp(s - m_new)
        l_ref[...] = alpha * l_ref[...] + jnp.sum(p, axis=-1, keepdims=True)
        acc_ref[...] = alpha * acc_ref[...] + jnp.dot(p.astype(BF16), vt, preferred_element_type=F32)
        m_ref[...] = m_new
        return carry

    lax.fori_loop(0, s_len // tk, step, 0)
    out = acc_ref[...] * (1.0 / l_ref[...])
    for r in range(4):
        o_ref[:, r * LANES:(r + 1) * LANES] = jnp.where(
            lo, out[r * tq:(r + 1) * tq], out[(4 + r) * tq:(5 + r) * tq])


def _attn_c(qk, proj, *, bsz, s_len, tq, tk):
    n = proj.shape[0]
    nq = s_len // tq
    v_first = proj.shape[1] // LANES - 2
    kern = functools.partial(_attn_c_kernel, tq=tq, tk=tk, s_len=s_len)
    return pl.pallas_call(
        kern,
        grid=(bsz, 2, nq),
        in_specs=[pl.BlockSpec((tq, 4 * LANES), lambda b, p, i: (b * nq + i, p)),
                  pl.BlockSpec((s_len, LANES), lambda b, p, i: (b, 8 + p)),
                  pl.BlockSpec((s_len, LANES), lambda b, p, i: (b, v_first + p))],
        out_specs=pl.BlockSpec((tq, 4 * LANES), lambda b, p, i: (b * nq + i, p)),
        out_shape=jax.ShapeDtypeStruct((n, 8 * LANES), F32),
        scratch_shapes=[pltpu.VMEM((8 * tq, LANES), BF16),
                        pltpu.VMEM((8 * tq, 1), F32),
                        pltpu.VMEM((8 * tq, 1), F32),
                        pltpu.VMEM((8 * tq, LANES), F32)],
        compiler_params=_params("parallel", "parallel", "arbitrary"),
        name="attn_c",
    )(qk, qk, proj)


def _mix_out_kernel(x_ref, oa_ref, ob_ref, oc_ref, ga_ref, gb_ref, gc_ref, w_ref, o_ref):
    parts = []
    for o_r, g_r in ((oa_ref, ga_ref), (ob_ref, gb_ref), (oc_ref, gc_ref)):
        o = o_r[...]
        parts.append((o * _rms_scale(o) * g_r[...]).astype(BF16))
    mix = jnp.concatenate(parts, axis=-1)
    o_ref[...] = x_ref[...] + jnp.dot(mix, w_ref[...], preferred_element_type=F32)


def _mix_out(x, oa, ob, oc, ga, gb, gc, w, *, tm):
    n, d = x.shape
    row = lambda width: pl.BlockSpec((tm, width), lambda i: (i, 0))
    const = lambda shape: pl.BlockSpec(shape, lambda i: (0, 0))
    return pl.pallas_call(
        _mix_out_kernel,
        grid=(n // tm,),
        in_specs=[row(d), row(oa.shape[1]), row(ob.shape[1]), row(oc.shape[1]),
                  const((1, oa.shape[1])), const((1, ob.shape[1])), const((1, oc.shape[1])),
                  const(w.shape)],
        out_specs=row(d),
        out_shape=jax.ShapeDtypeStruct((n, d), F32),
        compiler_params=_params("parallel"),
        name="mix_out",
    )(x, oa, ob, oc, ga.reshape(1, -1), gb.reshape(1, -1), gc.reshape(1, -1), w)


def _mlp_kernel(x_ref, g_ref, wu_ref, wd_ref, gf_ref, o_ref, h_ref, *, final_norm):
    f = pl.program_id(1)

    @pl.when(f == 0)
    def _():
        x = x_ref[...]
        h_ref[...] = (x * _rms_scale(x) * g_ref[...]).astype(BF16)
        o_ref[...] = x

    u = jnp.maximum(jnp.dot(h_ref[...], wu_ref[...], preferred_element_type=F32), 0.0)
    o_ref[...] += jnp.dot((u * u).astype(BF16), wd_ref[...], preferred_element_type=F32)

    if final_norm:
        @pl.when(f == pl.num_programs(1) - 1)
        def _():
            y = o_ref[...]
            o_ref[...] = y * _rms_scale(y) * gf_ref[...]


def _mlp(x, gain, w_up, w_down, final_gain, *, tm, tf, final_norm):
    n, d = x.shape
    ff = w_up.shape[1]
    kern = functools.partial(_mlp_kernel, final_norm=final_norm)
    return pl.pallas_call(
        kern,
        grid=(n // tm, ff // tf),
        in_specs=[pl.BlockSpec((tm, d), lambda i, f: (i, 0)),
                  pl.BlockSpec((1, d), lambda i, f: (0, 0)),
                  pl.BlockSpec((d, tf), lambda i, f: (0, f)),
                  pl.BlockSpec((tf, d), lambda i, f: (f, 0)),
                  pl.BlockSpec((1, d), lambda i, f: (0, 0))],
        out_specs=pl.BlockSpec((tm, d), lambda i, f: (i, 0)),
        out_shape=jax.ShapeDtypeStruct((n, d), F32),
        scratch_shapes=[pltpu.VMEM((tm, d), BF16)],
        compiler_params=_params("parallel", "arbitrary"),
        name="mlp",
    )(x, gain.reshape(1, d), w_up, w_down, final_gain.reshape(1, d))


def _t5_bucket_np(rel):
    nb = T5_BUCKETS // 2
    max_exact = nb // 2
    base = np.where(rel > 0, nb, 0)
    n = np.abs(rel)
    nf = np.maximum(n, 1).astype(np.float32)
    large = max_exact + (np.log(nf / np.float32(max_exact)) / np.float32(math.log(T5_MAX_DIST / max_exact))
                         * np.float32(nb - max_exact)).astype(np.int32)
    large = np.minimum(large, nb - 1)
    return base + np.where(n < max_exact, n, large)


def _head_cols(perm):
    return np.concatenate([np.arange(h * HEAD_DIM, (h + 1) * HEAD_DIM) for h in perm])


def _rope_tables(s_len):
    axis_dim = HEAD_DIM // 2
    quarter = axis_dim // 2
    t = jnp.arange(s_len)
    row = (t // GRID_W).astype(F32)
    col = (t % GRID_W).astype(F32)
    freqs = ROPE_THETA ** (-jnp.arange(0, axis_dim, 2, dtype=F32) / axis_dim)
    ang_row = row[:, None] * freqs[None, :]
    ang_col = col[:, None] * freqs[None, :]
    ang = jnp.concatenate([ang_row, ang_row, ang_col, ang_col], axis=-1)
    first = (np.arange(HEAD_DIM) % axis_dim) < quarter
    cos = jnp.cos(ang)
    sin = jnp.sin(ang)
    sdn = jnp.where(first[None, :], -sin, 0.0)
    sup = jnp.where(first[None, :], 0.0, sin)
    two = lambda a: jnp.concatenate([a, a], axis=-1)
    return two(cos), two(sdn), two(sup)


def kernel(x, norm_mix, w_in, a_sink, t5_table, b_rpb, c_q_gain, c_k_gain, out_gain_a, out_gain_b,
           out_gain_c, w_o, norm_mlp, w_up, w_down, norm_final):
    bsz, s_len, d_model = x.shape
    depth = w_in.shape[0]
    n = bsz * s_len
    scale = HEAD_DIM ** -0.5

    a_w = len(A_PERM) * HEAD_DIM
    c_w = len(C_PERM) * HEAD_DIM
    a_kv_w = a_w // 4
    b_w = a_w
    c_kv_w = c_w // 4
    off_qa, off_ka = 0, a_w
    off_qb = a_w + 2 * a_kv_w
    off_qc = off_qb + 3 * b_w
    off_kc = off_qc + c_w
    in_width = off_kc + 2 * c_kv_w
    assert in_width == w_in.shape[2]

    cols = np.arange(in_width)
    cols[off_qa:off_qa + a_w] = off_qa + _head_cols(A_PERM)
    cols[off_qc:off_qc + c_w] = off_qc + _head_cols(C_PERM)
    col_scale = np.ones((in_width,), np.float32)
    col_scale[off_qa:off_qa + a_w] = scale
    col_scale[off_qb:off_qb + b_w] = scale
    w_in_p = (w_in[:, :, cols] * col_scale).astype(BF16)

    rows_o = np.arange(w_o.shape[1])
    rows_o[:a_w] = _head_cols(A_PERM)
    rows_o[a_w + b_w:] = a_w + b_w + _head_cols(C_PERM)
    w_o_p = w_o[:, rows_o, :].astype(BF16)
    gain_a_p = out_gain_a[:, _head_cols(A_PERM)]
    gain_c_p = out_gain_c[:, _head_cols(C_PERM)]
    w_up_b = w_up.astype(BF16)
    w_down_b = w_down.astype(BF16)

    qi = np.arange(A_BLOCK)[:, None]
    kj = np.arange(3 * A_BLOCK)[None, :]
    bucket = _t5_bucket_np(kj - A_BLOCK - qi)
    bias_a = t5_table[bucket].transpose(2, 0, 1).astype(F32)
    bias_a = bias_a[np.array(A_PERM)].reshape(4, 2, A_BLOCK, 3 * A_BLOCK).transpose(1, 0, 2, 3)
    sink_p = a_sink[:, np.array(A_PERM)].astype(F32)

    cq = np.arange(GRID_W)[:, None]
    ck = np.arange(GRID_W)[None, :]
    dc = np.clip(ck - cq + NA_COLS - 1, 0, 2 * NA_COLS - 2)
    t_blocks = b_rpb[:, :, :, dc].astype(F32)
    t_pair = jnp.concatenate([t_blocks[:, :, :-1], t_blocks[:, :, 1:]], axis=-1)

    cos_t, sdn_t, sup_t = _rope_tables(s_len)
    two = lambda g: jnp.concatenate([g, g], axis=-1)
    n_qblk, n_kblk = c_w // LANES, c_kv_w // LANES

    xf = x.reshape(n, d_model)
    for l in range(depth):
        proj = _inproj(xf, norm_mix[l], w_in_p[l], tm=min(1024, n), tn=in_width // 3)
        oa = _attn_a(proj, sink_p[l], bias_a, bsz=bsz, s_len=s_len, tq=min(512, s_len))
        ob = _attn_b(proj, t_pair[l], bsz=bsz, s_len=s_len, rblk=8)
        gains = jnp.concatenate([jnp.tile(two(c_q_gain[l] * scale)[None], (n_qblk, 1)),
                                 jnp.tile(two(c_k_gain[l])[None], (n_kblk, 1))], axis=0)
        qk = _c_prep(proj, gains.reshape(n_qblk + n_kblk, 1, LANES).astype(F32), cos_t, sdn_t, sup_t,
                     s_len=s_len, tr=min(1024, s_len), first=off_qc // LANES)
        oc = _attn_c(qk, proj, bsz=bsz, s_len=s_len, tq=min(128, s_len), tk=min(512, s_len))
        xf = _mix_out(xf, oa, ob, oc, gain_a_p[l], out_gain_b[l], gain_c_p[l], w_o_p[l], tm=min(512, n))
        xf = _mlp(xf, norm_mlp[l], w_up_b[l], w_down_b[l], norm_final,
                  tm=min(512, n), tf=512, final_norm=(l == depth - 1))
    return xf.reshape(bsz, s_len, d_model)
```

```python
import functools
import math

import numpy as np
import jax
import jax.numpy as jnp
from jax import lax
from jax.experimental import pallas as pl
from jax.experimental.pallas import tpu as pltpu

HEAD_DIM = 64
LANES = 128
WINDOW = 128
A_BLOCK = 128
T5_BUCKETS = 32
T5_MAX_DIST = 128
GRID_W = 64
NA_ROWS = 8
NA_COLS = 16
ROPE_THETA = 10000.0
EPS = 1e-6
MASK_VALUE = -1e30
VMEM_LIMIT = 56 * 1024 * 1024

F32 = jnp.float32
BF16 = jnp.bfloat16

A_PERM = (0, 4, 1, 5, 2, 6, 3, 7)
C_PERM = (0, 4, 1, 5, 2, 6, 3, 7, 8, 12, 9, 13, 10, 14, 11, 15)


def _params(*sem):
    return pltpu.CompilerParams(dimension_semantics=sem, vmem_limit_bytes=VMEM_LIMIT)


def _rms_scale(x):
    return lax.rsqrt(jnp.mean(x * x, axis=-1, keepdims=True) + EPS)


def _lo_lanes():
    return lax.broadcasted_iota(jnp.int32, (1, LANES), 1) < HEAD_DIM


def _inproj_kernel(x_ref, g_ref, w_ref, o_ref, h_ref):
    @pl.when(pl.program_id(1) == 0)
    def _():
        x = x_ref[...]
        h_ref[...] = (x * _rms_scale(x) * g_ref[...]).astype(BF16)

    o_ref[...] = jnp.dot(h_ref[...], w_ref[...], preferred_element_type=F32).astype(o_ref.dtype)


def _inproj(x, gain, w, *, tm, tn):
    n, d = x.shape
    e = w.shape[1]
    return pl.pallas_call(
        _inproj_kernel,
        grid=(n // tm, e // tn),
        in_specs=[pl.BlockSpec((tm, d), lambda i, j: (i, 0)),
                  pl.BlockSpec((1, d), lambda i, j: (0, 0)),
                  pl.BlockSpec((d, tn), lambda i, j: (0, j))],
        out_specs=pl.BlockSpec((tm, tn), lambda i, j: (i, j)),
        out_shape=jax.ShapeDtypeStruct((n, e), BF16),
        scratch_shapes=[pltpu.VMEM((tm, d), BF16)],
        compiler_params=_params("parallel", "arbitrary"),
        name="inproj",
    )(x, gain.reshape(1, d), w)


def _attn_a_kernel(sink_ref, q_ref, k_ref, v_ref, bias_ref, o_ref, *, tq, s_len):
    i = pl.program_id(1)
    lo = _lo_lanes()
    qi = lax.broadcasted_iota(jnp.int32, (A_BLOCK, 3 * A_BLOCK), 0)
    kj = lax.broadcasted_iota(jnp.int32, (A_BLOCK, 3 * A_BLOCK), 1)
    in_window = jnp.abs(kj - A_BLOCK - qi) <= WINDOW
    for j in range(tq // A_BLOCK):
        p0 = i * tq + j * A_BLOCK
        k_parts, v_parts = [], []
        for o in (-1, 0, 1):
            st = pl.multiple_of(jnp.clip(p0 + o * A_BLOCK, 0, s_len - A_BLOCK), A_BLOCK)
            k_parts.append(k_ref[pl.ds(st, A_BLOCK), :])
            v_parts.append(v_ref[pl.ds(st, A_BLOCK), :])
        k3 = jnp.concatenate(k_parts, axis=0)
        v3 = jnp.concatenate(v_parts, axis=0)
        key_pos = p0 - A_BLOCK + kj
        valid = in_window & (key_pos >= 0) & (key_pos < s_len)
        rows = slice(j * A_BLOCK, (j + 1) * A_BLOCK)
        q_blocks = [q_ref[rows, r * LANES:(r + 1) * LANES] for r in range(4)]
        halves = []
        for half in (0, 1):
            keep = lo if half == 0 else jnp.logical_not(lo)
            qs = jnp.concatenate([jnp.where(keep, qb, jnp.zeros_like(qb)) for qb in q_blocks], axis=0)
            s = lax.dot_general(qs, k3, (((1,), (1,)), ((), ())), preferred_element_type=F32)
            s = s.reshape(4, A_BLOCK, 3 * A_BLOCK) + bias_ref[half]
            s = jnp.where(valid[None], s, MASK_VALUE)
            sink = jnp.stack([jnp.full((A_BLOCK, 1), sink_ref[2 * r + half], F32) for r in range(4)])
            m = jnp.maximum(jnp.max(s, axis=-1, keepdims=True), sink)
            e = jnp.exp(s - m)
            den = jnp.sum(e, axis=-1, keepdims=True) + jnp.exp(sink - m)
            pv = jnp.dot(e.reshape(4 * A_BLOCK, 3 * A_BLOCK).astype(BF16), v3,
                         preferred_element_type=F32)
            halves.append(pv.reshape(4, A_BLOCK, LANES) * (1.0 / den))
        for r in range(4):
            o_ref[rows, r * LANES:(r + 1) * LANES] = jnp.where(lo, halves[0][r], halves[1][r])


def _attn_a(proj, sink_p, bias_p, *, bsz, s_len, tq):
    n = proj.shape[0]
    nq = s_len // tq
    kern = functools.partial(_attn_a_kernel, tq=tq, s_len=s_len)
    return pl.pallas_call(
        kern,
        grid=(bsz, nq),
        in_specs=[pl.BlockSpec(memory_space=pltpu.SMEM),
                  pl.BlockSpec((tq, 4 * LANES), lambda b, i: (b * nq + i, 0)),
                  pl.BlockSpec((s_len, LANES), lambda b, i: (b, 4)),
                  pl.BlockSpec((s_len, LANES), lambda b, i: (b, 5)),
                  pl.BlockSpec((2, 4, A_BLOCK, 3 * A_BLOCK), lambda b, i: (0, 0, 0, 0))],
        out_specs=pl.BlockSpec((tq, 4 * LANES), lambda b, i: (b * nq + i, 0)),
        out_shape=jax.ShapeDtypeStruct((n, 4 * LANES), F32),
        compiler_params=_params("parallel", "arbitrary"),
        name="attn_a",
    )(sink_p, proj, proj, proj, bias_p)


def _attn_b_kernel(q_ref, k_ref, v_ref, t_ref, o_ref, *, rblk, rows):
    r0 = pl.program_id(2) * rblk
    lo = _lo_lanes()
    nkeys = NA_ROWS * GRID_W
    cq = lax.broadcasted_iota(jnp.int32, (GRID_W, nkeys), 0)
    ck = lax.broadcasted_iota(jnp.int32, (GRID_W, nkeys), 1) % GRID_W
    cs = jnp.clip(cq - NA_COLS // 2, 0, GRID_W - NA_COLS)
    col_valid = (ck >= cs) & (ck < cs + NA_COLS)
    for a in range(rblk):
        rq = r0 + a
        rs = jnp.clip(rq - NA_ROWS // 2, 0, rows - NA_ROWS)
        off = rs - rq + (NA_ROWS - 1)
        st = pl.multiple_of(rs * GRID_W, GRID_W)
        kwin = k_ref[pl.ds(st, nkeys), :]
        vwin = v_ref[pl.ds(st, nkeys), :]
        qrow = q_ref[a * GRID_W:(a + 1) * GRID_W, :]
        zero = jnp.zeros_like(qrow)
        qs = jnp.concatenate([jnp.where(lo, qrow, zero), jnp.where(lo, zero, qrow)], axis=0)
        s = lax.dot_general(qs, kwin, (((1,), (1,)), ((), ())), preferred_element_type=F32)
        bias = jnp.concatenate(
            [jnp.concatenate([t_ref[half, off + 2 * j] for j in range(NA_ROWS // 2)], axis=1)
             for half in (0, 1)], axis=0)
        s = s.reshape(2, GRID_W, nkeys) + bias.reshape(2, GRID_W, nkeys)
        s = jnp.where(col_valid[None], s, MASK_VALUE)
        m = jnp.max(s, axis=-1, keepdims=True)
        e = jnp.exp(s - m)
        den = jnp.sum(e, axis=-1, keepdims=True)
        pv = jnp.dot(e.reshape(2 * GRID_W, nkeys).astype(BF16), vwin, preferred_element_type=F32)
        pv = pv.reshape(2, GRID_W, LANES) * (1.0 / den)
        o_ref[a * GRID_W:(a + 1) * GRID_W, :] = jnp.where(lo, pv[0], pv[1])


def _attn_b(proj, t_pair, *, bsz, s_len, rblk):
    n = proj.shape[0]
    rows = s_len // GRID_W
    nrb = rows // rblk
    kern = functools.partial(_attn_b_kernel, rblk=rblk, rows=rows)
    tq = rblk * GRID_W
    return pl.pallas_call(
        kern,
        grid=(bsz, 4, nrb),
        in_specs=[pl.BlockSpec((tq, LANES), lambda b, h, i: (b * nrb + i, 6 + h)),
                  pl.BlockSpec((s_len, LANES), lambda b, h, i: (b, 10 + h)),
                  pl.BlockSpec((s_len, LANES), lambda b, h, i: (b, 14 + h)),
                  pl.BlockSpec((2, 2 * NA_ROWS - 2, GRID_W, LANES), lambda b, h, i: (h, 0, 0, 0))],
        out_specs=pl.BlockSpec((tq, LANES), lambda b, h, i: (b * nrb + i, h)),
        out_shape=jax.ShapeDtypeStruct((n, 4 * LANES), F32),
        compiler_params=_params("parallel", "parallel", "arbitrary"),
        name="attn_b",
    )(proj, proj, proj, t_pair)


def _c_prep_kernel(x_ref, g_ref, cos_ref, sdn_ref, sup_ref, o_ref):
    x = x_ref[...].astype(F32)
    lo = _lo_lanes()
    xx = x * x
    ms_lo = jnp.sum(jnp.where(lo, xx, 0.0), axis=-1, keepdims=True)
    ms_hi = jnp.sum(jnp.where(lo, 0.0, xx), axis=-1, keepdims=True)
    ms = jnp.where(lo, ms_lo, ms_hi) * (1.0 / HEAD_DIM)
    y = x * lax.rsqrt(ms + EPS) * g_ref[0]
    quarter = HEAD_DIM // 4
    y_dn = pltpu.roll(y, LANES - quarter, 1)
    y_up = pltpu.roll(y, quarter, 1)
    o_ref[...] = (y * cos_ref[...] + y_dn * sdn_ref[...] + y_up * sup_ref[...]).astype(o_ref.dtype)


def _c_prep(proj, gains, cos_t, sdn_t, sup_t, *, s_len, tr, first):
    n = proj.shape[0]
    nblk = gains.shape[0]
    npos = s_len // tr
    pos_spec = pl.BlockSpec((tr, LANES), lambda i, j: (i % npos, 0))
    return pl.pallas_call(
        _c_prep_kernel,
        grid=(n // tr, nblk),
        in_specs=[pl.BlockSpec((tr, LANES), lambda i, j: (i, first + j)),
                  pl.BlockSpec((1, 1, LANES), lambda i, j: (j, 0, 0)),
                  pos_spec, pos_spec, pos_spec],
        out_specs=pl.BlockSpec((tr, LANES), lambda i, j: (i, j)),
        out_shape=jax.ShapeDtypeStruct((n, nblk * LANES), BF16),
        compiler_params=_params("parallel", "arbitrary"),
        name="c_prep",
    )(proj, gains, cos_t, sdn_t, sup_t)


def _c_vprep_kernel(x_ref, o_ref):
    xt = x_ref[...].astype(F32).T
    first = lax.broadcasted_iota(jnp.int32, (LANES, 1), 0) < HEAD_DIM
    o_ref[:LANES, :] = jnp.where(first, xt, 1.0).astype(o_ref.dtype)
    o_ref[LANES:, :] = jnp.where(first, 1.0, xt).astype(o_ref.dtype)


def _c_vprep(proj, *, bsz, s_len, tk):
    nk = s_len // tk
    v_first = proj.shape[1] // LANES - 2
    return pl.pallas_call(
        _c_vprep_kernel,
        grid=(bsz, 2, nk),
        in_specs=[pl.BlockSpec((tk, LANES), lambda b, p, i: (b * nk + i, v_first + p))],
        out_specs=pl.BlockSpec((None, None, None, 2 * LANES, tk), lambda b, p, i: (b, p, i, 0, 0)),
        out_shape=jax.ShapeDtypeStruct((bsz, 2, nk, 2 * LANES, tk), BF16),
        compiler_params=_params("parallel", "parallel", "arbitrary"),
        name="c_vprep",
    )(proj)


C_GROUP = 256


def _attn_c_kernel(q_ref, k_ref, vt_ref, o_ref, qs_ref, acc_ref, s_ref, *, tq, tk, s_len):
    lo = _lo_lanes()
    rows = 8 * tq
    half_rows = rows // 2
    for r in range(4):
        qb = q_ref[:, r * LANES:(r + 1) * LANES]
        zero = jnp.zeros_like(qb)
        qs_ref[r * tq:(r + 1) * tq, :] = jnp.where(lo, qb, zero)
        qs_ref[(4 + r) * tq:(5 + r) * tq, :] = jnp.where(lo, zero, qb)
    acc_ref[...] = jnp.zeros(acc_ref.shape, F32)

    nk = s_len // tk

    def scores(t, slot):
        kt = k_ref[pl.ds(pl.multiple_of(t * tk, tk), tk), :]
        s_ref[slot] = lax.dot_general(kt, qs_ref[...], (((1,), (1,)), ((), ())),
                                      preferred_element_type=F32)

    def softmax_pv(t, slot, m):
        m_parts = []
        for g in range(rows // C_GROUP):
            cols = slice(g * C_GROUP, (g + 1) * C_GROUP)
            st = s_ref[slot, :, cols]
            m_old = m[:, cols]
            m_new = jnp.maximum(m_old, jnp.max(st, axis=0, keepdims=True))
            alpha = jnp.exp(m_old - m_new)
            pt = jnp.exp(st - m_new).astype(BF16)
            half = 0 if (g + 1) * C_GROUP <= half_rows else 1
            vt = vt_ref[t, half * LANES:(half + 1) * LANES, :]
            acc_ref[:, cols] = alpha * acc_ref[:, cols] + jnp.dot(vt, pt, preferred_element_type=F32)
            m_parts.append(m_new)
        return jnp.concatenate(m_parts, axis=1)

    scores(0, 0)

    def two_steps(u, m):
        t0 = 2 * u
        scores(t0 + 1, 1)
        m = softmax_pv(t0, 0, m)
        scores(jnp.minimum(t0 + 2, nk - 1), 0)
        return softmax_pv(t0 + 1, 1, m)

    lax.fori_loop(0, nk // 2, two_steps, jnp.full((1, rows), -jnp.inf, F32))
    acc_lo = acc_ref[:, :half_rows]
    acc_hi = acc_ref[:, half_rows:]
    o_lo = acc_lo[:HEAD_DIM] * (1.0 / acc_lo[HEAD_DIM:HEAD_DIM + 1])
    o_hi = acc_hi[HEAD_DIM:] * (1.0 / acc_hi[:1])
    for r in range(4):
        ot = jnp.concatenate([o_lo[:, r * tq:(r + 1) * tq], o_hi[:, r * tq:(r + 1) * tq]], axis=0)
        o_ref[:, r * LANES:(r + 1) * LANES] = ot.T


def _attn_c(qk, vt, *, bsz, s_len, tq, tk):
    n = qk.shape[0]
    nq = s_len // tq
    nk = s_len // tk
    assert (4 * tq) % C_GROUP == 0 and nk % 2 == 0
    kern = functools.partial(_attn_c_kernel, tq=tq, tk=tk, s_len=s_len)
    return pl.pallas_call(
        kern,
        grid=(bsz, 2, nq),
        in_specs=[pl.BlockSpec((tq, 4 * LANES), lambda b, p, i: (b * nq + i, p)),
                  pl.BlockSpec((s_len, LANES), lambda b, p, i: (b, 8 + p)),
                  pl.BlockSpec((None, None, nk, 2 * LANES, tk), lambda b, p, i: (b, p, 0, 0, 0))],
        out_specs=pl.BlockSpec((tq, 4 * LANES), lambda b, p, i: (b * nq + i, p)),
        out_shape=jax.ShapeDtypeStruct((n, 8 * LANES), F32),
        scratch_shapes=[pltpu.VMEM((8 * tq, LANES), BF16),
                        pltpu.VMEM((LANES, 8 * tq), F32),
                        pltpu.VMEM((2, tk, 8 * tq), F32)],
        compiler_params=_params("parallel", "parallel", "arbitrary"),
        name="attn_c",
    )(qk, qk, vt)


def _mix_out_kernel(x_ref, oa_ref, ob_ref, oc_ref, ga_ref, gb_ref, gc_ref, w_ref, o_ref):
    parts = []
    for o_r, g_r in ((oa_ref, ga_ref), (ob_ref, gb_ref), (oc_ref, gc_ref)):
        o = o_r[...]
        parts.append((o * _rms_scale(o) * g_r[...]).astype(BF16))
    mix = jnp.concatenate(parts, axis=-1)
    o_ref[...] = x_ref[...] + jnp.dot(mix, w_ref[...], preferred_element_type=F32)


def _mix_out(x, oa, ob, oc, ga, gb, gc, w, *, tm):
    n, d = x.shape
    row = lambda width: pl.BlockSpec((tm, width), lambda i: (i, 0))
    const = lambda shape: pl.BlockSpec(shape, lambda i: (0, 0))
    return pl.pallas_call(
        _mix_out_kernel,
        grid=(n // tm,),
        in_specs=[row(d), row(oa.shape[1]), row(ob.shape[1]), row(oc.shape[1]),
                  const((1, oa.shape[1])), const((1, ob.shape[1])), const((1, oc.shape[1])),
                  const(w.shape)],
        out_specs=row(d),
        out_shape=jax.ShapeDtypeStruct((n, d), F32),
        compiler_params=_params("parallel"),
        name="mix_out",
    )(x, oa, ob, oc, ga.reshape(1, -1), gb.reshape(1, -1), gc.reshape(1, -1), w)


def _mlp_kernel(x_ref, g_ref, wu_ref, wd_ref, gf_ref, o_ref, h_ref, *, final_norm):
    f = pl.program_id(1)

    @pl.when(f == 0)
    def _():
        x = x_ref[...]
        h_ref[...] = (x * _rms_scale(x) * g_ref[...]).astype(BF16)
        o_ref[...] = x

    u = jnp.maximum(jnp.dot(h_ref[...], wu_ref[...], preferred_element_type=F32), 0.0)
    o_ref[...] += jnp.dot((u * u).astype(BF16), wd_ref[...], preferred_element_type=F32)

    if final_norm:
        @pl.when(f == pl.num_programs(1) - 1)
        def _():
            y = o_ref[...]
            o_ref[...] = y * _rms_scale(y) * gf_ref[...]


def _mlp(x, gain, w_up, w_down, final_gain, *, tm, tf, final_norm):
    n, d = x.shape
    ff = w_up.shape[1]
    kern = functools.partial(_mlp_kernel, final_norm=final_norm)
    return pl.pallas_call(
        kern,
        grid=(n // tm, ff // tf),
        in_specs=[pl.BlockSpec((tm, d), lambda i, f: (i, 0)),
                  pl.BlockSpec((1, d), lambda i, f: (0, 0)),
                  pl.BlockSpec((d, tf), lambda i, f: (0, f)),
                  pl.BlockSpec((tf, d), lambda i, f: (f, 0)),
                  pl.BlockSpec((1, d), lambda i, f: (0, 0))],
        out_specs=pl.BlockSpec((tm, d), lambda i, f: (i, 0)),
        out_shape=jax.ShapeDtypeStruct((n, d), F32),
        scratch_shapes=[pltpu.VMEM((tm, d), BF16)],
        compiler_params=_params("parallel", "arbitrary"),
        name="mlp",
    )(x, gain.reshape(1, d), w_up, w_down, final_gain.reshape(1, d))


def _t5_bucket_np(rel):
    nb = T5_BUCKETS // 2
    max_exact = nb // 2
    base = np.where(rel > 0, nb, 0)
    n = np.abs(rel)
    nf = np.maximum(n, 1).astype(np.float32)
    large = max_exact + (np.log(nf / np.float32(max_exact)) / np.float32(math.log(T5_MAX_DIST / max_exact))
                         * np.float32(nb - max_exact)).astype(np.int32)
    large = np.minimum(large, nb - 1)
    return base + np.where(n < max_exact, n, large)


def _head_cols(perm):
    return np.concatenate([np.arange(h * HEAD_DIM, (h + 1) * HEAD_DIM) for h in perm])


def _rope_tables(s_len):
    axis_dim = HEAD_DIM // 2
    quarter = axis_dim // 2
    t = jnp.arange(s_len)
    row = (t // GRID_W).astype(F32)
    col = (t % GRID_W).astype(F32)
    freqs = ROPE_THETA ** (-jnp.arange(0, axis_dim, 2, dtype=F32) / axis_dim)
    ang_row = row[:, None] * freqs[None, :]
    ang_col = col[:, None] * freqs[None, :]
    ang = jnp.concatenate([ang_row, ang_row, ang_col, ang_col], axis=-1)
    first = (np.arange(HEAD_DIM) % axis_dim) < quarter
    cos = jnp.cos(ang)
    sin = jnp.sin(ang)
    sdn = jnp.where(first[None, :], -sin, 0.0)
    sup = jnp.where(first[None, :], 0.0, sin)
    two = lambda a: jnp.concatenate([a, a], axis=-1)
    return two(cos), two(sdn), two(sup)


def kernel(x, norm_mix, w_in, a_sink, t5_table, b_rpb, c_q_gain, c_k_gain, out_gain_a, out_gain_b,
           out_gain_c, w_o, norm_mlp, w_up, w_down, norm_final):
    bsz, s_len, d_model = x.shape
    depth = w_in.shape[0]
    n = bsz * s_len
    scale = HEAD_DIM ** -0.5

    a_w = len(A_PERM) * HEAD_DIM
    c_w = len(C_PERM) * HEAD_DIM
    a_kv_w = a_w // 4
    b_w = a_w
    c_kv_w = c_w // 4
    off_qa, off_ka = 0, a_w
    off_qb = a_w + 2 * a_kv_w
    off_qc = off_qb + 3 * b_w
    off_kc = off_qc + c_w
    in_width = off_kc + 2 * c_kv_w
    assert in_width == w_in.shape[2]

    cols = np.arange(in_width)
    cols[off_qa:off_qa + a_w] = off_qa + _head_cols(A_PERM)
    cols[off_qc:off_qc + c_w] = off_qc + _head_cols(C_PERM)
    col_scale = np.ones((in_width,), np.float32)
    col_scale[off_qa:off_qa + a_w] = scale
    col_scale[off_qb:off_qb + b_w] = scale
    w_in_p = (w_in[:, :, cols] * col_scale).astype(BF16)

    rows_o = np.arange(w_o.shape[1])
    rows_o[:a_w] = _head_cols(A_PERM)
    rows_o[a_w + b_w:] = a_w + b_w + _head_cols(C_PERM)
    w_o_p = w_o[:, rows_o, :].astype(BF16)
    gain_a_p = out_gain_a[:, _head_cols(A_PERM)]
    gain_c_p = out_gain_c[:, _head_cols(C_PERM)]
    w_up_b = w_up.astype(BF16)
    w_down_b = w_down.astype(BF16)

    qi = np.arange(A_BLOCK)[:, None]
    kj = np.arange(3 * A_BLOCK)[None, :]
    bucket = _t5_bucket_np(kj - A_BLOCK - qi)
    bias_a = t5_table[bucket].transpose(2, 0, 1).astype(F32)
    bias_a = bias_a[np.array(A_PERM)].reshape(4, 2, A_BLOCK, 3 * A_BLOCK).transpose(1, 0, 2, 3)
    sink_p = a_sink[:, np.array(A_PERM)].astype(F32)

    cq = np.arange(GRID_W)[:, None]
    ck = np.arange(GRID_W)[None, :]
    dc = np.clip(ck - cq + NA_COLS - 1, 0, 2 * NA_COLS - 2)
    t_blocks = b_rpb[:, :, :, dc].astype(F32)
    t_pair = jnp.concatenate([t_blocks[:, :, :-1], t_blocks[:, :, 1:]], axis=-1)

    cos_t, sdn_t, sup_t = _rope_tables(s_len)
    two = lambda g: jnp.concatenate([g, g], axis=-1)
    n_qblk, n_kblk = c_w // LANES, c_kv_w // LANES

    xf = x.reshape(n, d_model)
    for l in range(depth):
        proj = _inproj(xf, norm_mix[l], w_in_p[l], tm=min(1024, n), tn=in_width // 3)
        oa = _attn_a(proj, sink_p[l], bias_a, bsz=bsz, s_len=s_len, tq=min(512, s_len))
        ob = _attn_b(proj, t_pair[l], bsz=bsz, s_len=s_len, rblk=8)
        gains = jnp.concatenate([jnp.tile(two(c_q_gain[l] * scale)[None], (n_qblk, 1)),
                                 jnp.tile(two(c_k_gain[l])[None], (n_kblk, 1))], axis=0)
        qk = _c_prep(proj, gains.reshape(n_qblk + n_kblk, 1, LANES).astype(F32), cos_t, sdn_t, sup_t,
                     s_len=s_len, tr=min(1024, s_len), first=off_qc // LANES)
        tk_c = min(512, s_len)
        vt = _c_vprep(proj, bsz=bsz, s_len=s_len, tk=tk_c)
        oc = _attn_c(qk, vt, bsz=bsz, s_len=s_len, tq=min(128, s_len), tk=tk_c)
        xf = _mix_out(xf, oa, ob, oc, gain_a_p[l], out_gain_b[l], gain_c_p[l], w_o_p[l], tm=min(512, n))
        xf = _mlp(xf, norm_mlp[l], w_up_b[l], w_down_b[l], norm_final,
                  tm=min(512, n), tf=512, final_norm=(l == depth - 1))
    return xf.reshape(bsz, s_len, d_model)
```

```python
import functools
import math

import numpy as np
import jax
import jax.numpy as jnp
from jax import lax
from jax.experimental import pallas as pl
from jax.experimental.pallas import tpu as pltpu

HEAD_DIM = 64
LANES = 128
WINDOW = 128
A_BLOCK = 128
T5_BUCKETS = 32
T5_MAX_DIST = 128
GRID_W = 64
NA_ROWS = 8
NA_COLS = 16
ROPE_THETA = 10000.0
EPS = 1e-6
MASK_VALUE = -1e30
LOG2E = math.log2(math.e)
VMEM_LIMIT = 56 * 1024 * 1024

F32 = jnp.float32
BF16 = jnp.bfloat16

A_PERM = (0, 4, 1, 5, 2, 6, 3, 7)
C_PERM = (0, 4, 1, 5, 2, 6, 3, 7, 8, 12, 9, 13, 10, 14, 11, 15)


def _params(*sem):
    return pltpu.CompilerParams(dimension_semantics=sem, vmem_limit_bytes=VMEM_LIMIT)


def _rms_scale(x):
    return lax.rsqrt(jnp.mean(x * x, axis=-1, keepdims=True) + EPS)


def _lo_lanes():
    return lax.broadcasted_iota(jnp.int32, (1, LANES), 1) < HEAD_DIM


def _inproj_kernel(x_ref, g_ref, w_ref, o_ref, h_ref):
    @pl.when(pl.program_id(1) == 0)
    def _():
        x = x_ref[...]
        h_ref[...] = (x * _rms_scale(x) * g_ref[...]).astype(BF16)

    o_ref[...] = jnp.dot(h_ref[...], w_ref[...], preferred_element_type=F32).astype(o_ref.dtype)


def _inproj(x, gain, w, *, tm, tn):
    n, d = x.shape
    e = w.shape[1]
    return pl.pallas_call(
        _inproj_kernel,
        grid=(n // tm, e // tn),
        in_specs=[pl.BlockSpec((tm, d), lambda i, j: (i, 0)),
                  pl.BlockSpec((1, d), lambda i, j: (0, 0)),
                  pl.BlockSpec((d, tn), lambda i, j: (0, j))],
        out_specs=pl.BlockSpec((tm, tn), lambda i, j: (i, j)),
        out_shape=jax.ShapeDtypeStruct((n, e), BF16),
        scratch_shapes=[pltpu.VMEM((tm, d), BF16)],
        compiler_params=_params("parallel", "arbitrary"),
        name="inproj",
    )(x, gain.reshape(1, d), w)


def _attn_a_kernel(sink_ref, q_ref, k_ref, v_ref, bias_ref, o_ref, *, tq, s_len):
    i = pl.program_id(1)
    lo = _lo_lanes()
    qi = lax.broadcasted_iota(jnp.int32, (A_BLOCK, 3 * A_BLOCK), 0)
    kj = lax.broadcasted_iota(jnp.int32, (A_BLOCK, 3 * A_BLOCK), 1)
    in_window = jnp.abs(kj - A_BLOCK - qi) <= WINDOW
    for j in range(tq // A_BLOCK):
        p0 = i * tq + j * A_BLOCK
        k_parts, v_parts = [], []
        for o in (-1, 0, 1):
            st = pl.multiple_of(jnp.clip(p0 + o * A_BLOCK, 0, s_len - A_BLOCK), A_BLOCK)
            k_parts.append(k_ref[pl.ds(st, A_BLOCK), :])
            v_parts.append(v_ref[pl.ds(st, A_BLOCK), :])
        k3 = jnp.concatenate(k_parts, axis=0)
        v3 = jnp.concatenate(v_parts, axis=0)
        key_pos = p0 - A_BLOCK + kj
        valid = in_window & (key_pos >= 0) & (key_pos < s_len)
        rows = slice(j * A_BLOCK, (j + 1) * A_BLOCK)
        q_blocks = [q_ref[rows, r * LANES:(r + 1) * LANES] for r in range(4)]
        halves = []
        for half in (0, 1):
            keep = lo if half == 0 else jnp.logical_not(lo)
            qs = jnp.concatenate([jnp.where(keep, qb, jnp.zeros_like(qb)) for qb in q_blocks], axis=0)
            s = lax.dot_general(qs, k3, (((1,), (1,)), ((), ())), preferred_element_type=F32)
            s = s.reshape(4, A_BLOCK, 3 * A_BLOCK) + bias_ref[half]
            s = jnp.where(valid[None], s, MASK_VALUE)
            sink = jnp.stack([jnp.full((A_BLOCK, 1), sink_ref[2 * r + half], F32) for r in range(4)])
            m = jnp.maximum(jnp.max(s, axis=-1, keepdims=True), sink)
            e = jnp.exp(s - m)
            den = jnp.sum(e, axis=-1, keepdims=True) + jnp.exp(sink - m)
            pv = jnp.dot(e.reshape(4 * A_BLOCK, 3 * A_BLOCK).astype(BF16), v3,
                         preferred_element_type=F32)
            halves.append(pv.reshape(4, A_BLOCK, LANES) * (1.0 / den))
        for r in range(4):
            o_ref[rows, r * LANES:(r + 1) * LANES] = jnp.where(lo, halves[0][r], halves[1][r])


def _attn_a(proj, sink_p, bias_p, *, bsz, s_len, tq):
    n = proj.shape[0]
    nq = s_len // tq
    kern = functools.partial(_attn_a_kernel, tq=tq, s_len=s_len)
    return pl.pallas_call(
        kern,
        grid=(bsz, nq),
        in_specs=[pl.BlockSpec(memory_space=pltpu.SMEM),
                  pl.BlockSpec((tq, 4 * LANES), lambda b, i: (b * nq + i, 0)),
                  pl.BlockSpec((s_len, LANES), lambda b, i: (b, 4)),
                  pl.BlockSpec((s_len, LANES), lambda b, i: (b, 5)),
                  pl.BlockSpec((2, 4, A_BLOCK, 3 * A_BLOCK), lambda b, i: (0, 0, 0, 0))],
        out_specs=pl.BlockSpec((tq, 4 * LANES), lambda b, i: (b * nq + i, 0)),
        out_shape=jax.ShapeDtypeStruct((n, 4 * LANES), F32),
        compiler_params=_params("parallel", "arbitrary"),
        name="attn_a",
    )(sink_p, proj, proj, proj, bias_p)


def _attn_b_kernel(q_ref, k_ref, v_ref, t_ref, o_ref, *, rblk, rows):
    r0 = pl.program_id(2) * rblk
    lo = _lo_lanes()
    nkeys = NA_ROWS * GRID_W
    cq = lax.broadcasted_iota(jnp.int32, (GRID_W, nkeys), 0)
    ck = lax.broadcasted_iota(jnp.int32, (GRID_W, nkeys), 1) % GRID_W
    cs = jnp.clip(cq - NA_COLS // 2, 0, GRID_W - NA_COLS)
    col_valid = (ck >= cs) & (ck < cs + NA_COLS)
    for a in range(rblk):
        rq = r0 + a
        rs = jnp.clip(rq - NA_ROWS // 2, 0, rows - NA_ROWS)
        off = rs - rq + (NA_ROWS - 1)
        st = pl.multiple_of(rs * GRID_W, GRID_W)
        kwin = k_ref[pl.ds(st, nkeys), :]
        vwin = v_ref[pl.ds(st, nkeys), :]
        qrow = q_ref[a * GRID_W:(a + 1) * GRID_W, :]
        zero = jnp.zeros_like(qrow)
        qs = jnp.concatenate([jnp.where(lo, qrow, zero), jnp.where(lo, zero, qrow)], axis=0)
        s = lax.dot_general(qs, kwin, (((1,), (1,)), ((), ())), preferred_element_type=F32)
        bias = jnp.concatenate(
            [jnp.concatenate([t_ref[half, off + 2 * j] for j in range(NA_ROWS // 2)], axis=1)
             for half in (0, 1)], axis=0)
        s = s.reshape(2, GRID_W, nkeys) + bias.reshape(2, GRID_W, nkeys)
        s = jnp.where(col_valid[None], s, MASK_VALUE)
        m = jnp.max(s, axis=-1, keepdims=True)
        e = jnp.exp(s - m)
        den = jnp.sum(e, axis=-1, keepdims=True)
        pv = jnp.dot(e.reshape(2 * GRID_W, nkeys).astype(BF16), vwin, preferred_element_type=F32)
        pv = pv.reshape(2, GRID_W, LANES) * (1.0 / den)
        o_ref[a * GRID_W:(a + 1) * GRID_W, :] = jnp.where(lo, pv[0], pv[1])


def _attn_b(proj, t_pair, *, bsz, s_len, rblk):
    n = proj.shape[0]
    rows = s_len // GRID_W
    nrb = rows // rblk
    kern = functools.partial(_attn_b_kernel, rblk=rblk, rows=rows)
    tq = rblk * GRID_W
    return pl.pallas_call(
        kern,
        grid=(bsz, 4, nrb),
        in_specs=[pl.BlockSpec((tq, LANES), lambda b, h, i: (b * nrb + i, 6 + h)),
                  pl.BlockSpec((s_len, LANES), lambda b, h, i: (b, 10 + h)),
                  pl.BlockSpec((s_len, LANES), lambda b, h, i: (b, 14 + h)),
                  pl.BlockSpec((2, 2 * NA_ROWS - 2, GRID_W, LANES), lambda b, h, i: (h, 0, 0, 0))],
        out_specs=pl.BlockSpec((tq, LANES), lambda b, h, i: (b * nrb + i, h)),
        out_shape=jax.ShapeDtypeStruct((n, 4 * LANES), F32),
        compiler_params=_params("parallel", "parallel", "arbitrary"),
        name="attn_b",
    )(proj, proj, proj, t_pair)


def _c_prep_kernel(x_ref, g_ref, cos_ref, sdn_ref, sup_ref, o_ref):
    x = x_ref[...].astype(F32)
    lo = _lo_lanes()
    xx = x * x
    ms_lo = jnp.sum(jnp.where(lo, xx, 0.0), axis=-1, keepdims=True)
    ms_hi = jnp.sum(jnp.where(lo, 0.0, xx), axis=-1, keepdims=True)
    ms = jnp.where(lo, ms_lo, ms_hi) * (1.0 / HEAD_DIM)
    y = x * lax.rsqrt(ms + EPS) * g_ref[0]
    quarter = HEAD_DIM // 4
    y_dn = pltpu.roll(y, LANES - quarter, 1)
    y_up = pltpu.roll(y, quarter, 1)
    o_ref[...] = (y * cos_ref[...] + y_dn * sdn_ref[...] + y_up * sup_ref[...]).astype(o_ref.dtype)


def _c_prep(proj, gains, cos_t, sdn_t, sup_t, *, s_len, tr, first):
    n = proj.shape[0]
    nblk = gains.shape[0]
    npos = s_len // tr
    pos_spec = pl.BlockSpec((tr, LANES), lambda i, j: (i % npos, 0))
    return pl.pallas_call(
        _c_prep_kernel,
        grid=(n // tr, nblk),
        in_specs=[pl.BlockSpec((tr, LANES), lambda i, j: (i, first + j)),
                  pl.BlockSpec((1, 1, LANES), lambda i, j: (j, 0, 0)),
                  pos_spec, pos_spec, pos_spec],
        out_specs=pl.BlockSpec((tr, LANES), lambda i, j: (i, j)),
        out_shape=jax.ShapeDtypeStruct((n, nblk * LANES), BF16),
        compiler_params=_params("parallel", "arbitrary"),
        name="c_prep",
    )(proj, gains, cos_t, sdn_t, sup_t)


def _c_vprep_kernel(x_ref, o_ref):
    xt = x_ref[...].astype(F32).T
    first = lax.broadcasted_iota(jnp.int32, (LANES, 1), 0) < HEAD_DIM
    o_ref[:LANES, :] = jnp.where(first, xt, 1.0).astype(o_ref.dtype)
    o_ref[LANES:, :] = jnp.where(first, 1.0, xt).astype(o_ref.dtype)


def _c_vprep(proj, *, bsz, s_len, tk):
    nk = s_len // tk
    v_first = proj.shape[1] // LANES - 2
    return pl.pallas_call(
        _c_vprep_kernel,
        grid=(bsz, 2, nk),
        in_specs=[pl.BlockSpec((tk, LANES), lambda b, p, i: (b * nk + i, v_first + p))],
        out_specs=pl.BlockSpec((None, None, None, 2 * LANES, tk), lambda b, p, i: (b, p, i, 0, 0)),
        out_shape=jax.ShapeDtypeStruct((bsz, 2, nk, 2 * LANES, tk), BF16),
        compiler_params=_params("parallel", "parallel", "arbitrary"),
        name="c_vprep",
    )(proj)


C_GROUP = 256


def _attn_c_kernel(q_ref, k_ref, vt_ref, o_ref, qs_ref, acc_ref, s_ref, *, tq, tk, s_len):
    lo = _lo_lanes()
    per_blk = tq // C_GROUP
    n_groups = 8 * per_blk
    for r in range(4):
        for j in range(per_blk):
            qb = q_ref[j * C_GROUP:(j + 1) * C_GROUP, r * LANES:(r + 1) * LANES]
            zero = jnp.zeros_like(qb)
            qs_ref[r * per_blk + j] = jnp.where(lo, qb, zero)
            qs_ref[(4 + r) * per_blk + j] = jnp.where(lo, zero, qb)
    acc_ref[...] = jnp.zeros(acc_ref.shape, F32)

    nk = s_len // tk

    def scores(t, slot, groups):
        kt = k_ref[pl.ds(pl.multiple_of(t * tk, tk), tk), :]
        for g in groups:
            s_ref[slot, g] = lax.dot_general(kt, qs_ref[g], (((1,), (1,)), ((), ())),
                                             preferred_element_type=F32)

    def step(t, slot, t_next, m):
        m_out = []
        for g in range(n_groups):
            scores(t_next, 1 - slot, (g,))
            st = s_ref[slot, g]
            m_new = jnp.maximum(m[g], jnp.max(st, axis=0, keepdims=True))
            alpha = jnp.exp2(m[g] - m_new)
            pt = jnp.exp2(st - m_new).astype(BF16)
            half = 0 if g < n_groups // 2 else 1
            vt = vt_ref[t, half * LANES:(half + 1) * LANES, :]
            acc_ref[g] = alpha * acc_ref[g] + jnp.dot(vt, pt, preferred_element_type=F32)
            m_out.append(m_new)
        return tuple(m_out)

    scores(0, 0, range(n_groups))

    def two_steps(u, m):
        t0 = 2 * u
        m = step(t0, 0, t0 + 1, m)
        return step(t0 + 1, 1, jnp.minimum(t0 + 2, nk - 1), m)

    m0 = tuple(jnp.full((1, C_GROUP), -jnp.inf, F32) for _ in range(n_groups))
    lax.fori_loop(0, nk // 2, two_steps, m0)
    for r in range(4):
        for j in range(per_blk):
            a_lo = acc_ref[r * per_blk + j]
            a_hi = acc_ref[(4 + r) * per_blk + j]
            o_lo = a_lo[:HEAD_DIM] * (1.0 / a_lo[HEAD_DIM:HEAD_DIM + 1])
            o_hi = a_hi[HEAD_DIM:] * (1.0 / a_hi[:1])
            o_ref[j * C_GROUP:(j + 1) * C_GROUP, r * LANES:(r + 1) * LANES] = (
                jnp.concatenate([o_lo, o_hi], axis=0).T)


def _attn_c(qk, vt, *, bsz, s_len, tq, tk):
    n = qk.shape[0]
    nq = s_len // tq
    nk = s_len // tk
    assert tq % C_GROUP == 0 and nk % 2 == 0
    n_groups = 8 * tq // C_GROUP
    kern = functools.partial(_attn_c_kernel, tq=tq, tk=tk, s_len=s_len)
    return pl.pallas_call(
        kern,
        grid=(bsz, 2, nq),
        in_specs=[pl.BlockSpec((tq, 4 * LANES), lambda b, p, i: (b * nq + i, p)),
                  pl.BlockSpec((s_len, LANES), lambda b, p, i: (b, 8 + p)),
                  pl.BlockSpec((None, None, nk, 2 * LANES, tk), lambda b, p, i: (b, p, 0, 0, 0))],
        out_specs=pl.BlockSpec((tq, 4 * LANES), lambda b, p, i: (b * nq + i, p)),
        out_shape=jax.ShapeDtypeStruct((n, 8 * LANES), F32),
        scratch_shapes=[pltpu.VMEM((n_groups, C_GROUP, LANES), BF16),
                        pltpu.VMEM((n_groups, LANES, C_GROUP), F32),
                        pltpu.VMEM((2, n_groups, tk, C_GROUP), F32)],
        compiler_params=_params("parallel", "parallel", "arbitrary"),
        name="attn_c",
    )(qk, qk, vt)


def _mix_out_kernel(x_ref, oa_ref, ob_ref, oc_ref, ga_ref, gb_ref, gc_ref, w_ref, o_ref):
    parts = []
    for o_r, g_r in ((oa_ref, ga_ref), (ob_ref, gb_ref), (oc_ref, gc_ref)):
        o = o_r[...]
        parts.append((o * _rms_scale(o) * g_r[...]).astype(BF16))
    mix = jnp.concatenate(parts, axis=-1)
    o_ref[...] = x_ref[...] + jnp.dot(mix, w_ref[...], preferred_element_type=F32)


def _mix_out(x, oa, ob, oc, ga, gb, gc, w, *, tm):
    n, d = x.shape
    row = lambda width: pl.BlockSpec((tm, width), lambda i: (i, 0))
    const = lambda shape: pl.BlockSpec(shape, lambda i: (0, 0))
    return pl.pallas_call(
        _mix_out_kernel,
        grid=(n // tm,),
        in_specs=[row(d), row(oa.shape[1]), row(ob.shape[1]), row(oc.shape[1]),
                  const((1, oa.shape[1])), const((1, ob.shape[1])), const((1, oc.shape[1])),
                  const(w.shape)],
        out_specs=row(d),
        out_shape=jax.ShapeDtypeStruct((n, d), F32),
        compiler_params=_params("parallel"),
        name="mix_out",
    )(x, oa, ob, oc, ga.reshape(1, -1), gb.reshape(1, -1), gc.reshape(1, -1), w)


def _mlp_kernel(x_ref, g_ref, wu_ref, wd_ref, gf_ref, o_ref, h_ref, *, final_norm):
    f = pl.program_id(1)

    @pl.when(f == 0)
    def _():
        x = x_ref[...]
        h_ref[...] = (x * _rms_scale(x) * g_ref[...]).astype(BF16)
        o_ref[...] = x

    u = jnp.maximum(jnp.dot(h_ref[...], wu_ref[...], preferred_element_type=F32), 0.0)
    o_ref[...] += jnp.dot((u * u).astype(BF16), wd_ref[...], preferred_element_type=F32)

    if final_norm:
        @pl.when(f == pl.num_programs(1) - 1)
        def _():
            y = o_ref[...]
            o_ref[...] = y * _rms_scale(y) * gf_ref[...]


def _mlp(x, gain, w_up, w_down, final_gain, *, tm, tf, final_norm):
    n, d = x.shape
    ff = w_up.shape[1]
    kern = functools.partial(_mlp_kernel, final_norm=final_norm)
    return pl.pallas_call(
        kern,
        grid=(n // tm, ff // tf),
        in_specs=[pl.BlockSpec((tm, d), lambda i, f: (i, 0)),
                  pl.BlockSpec((1, d), lambda i, f: (0, 0)),
                  pl.BlockSpec((d, tf), lambda i, f: (0, f)),
                  pl.BlockSpec((tf, d), lambda i, f: (f, 0)),
                  pl.BlockSpec((1, d), lambda i, f: (0, 0))],
        out_specs=pl.BlockSpec((tm, d), lambda i, f: (i, 0)),
        out_shape=jax.ShapeDtypeStruct((n, d), F32),
        scratch_shapes=[pltpu.VMEM((tm, d), BF16)],
        compiler_params=_params("parallel", "arbitrary"),
        name="mlp",
    )(x, gain.reshape(1, d), w_up, w_down, final_gain.reshape(1, d))


def _t5_bucket_np(rel):
    nb = T5_BUCKETS // 2
    max_exact = nb // 2
    base = np.where(rel > 0, nb, 0)
    n = np.abs(rel)
    nf = np.maximum(n, 1).astype(np.float32)
    large = max_exact + (np.log(nf / np.float32(max_exact)) / np.float32(math.log(T5_MAX_DIST / max_exact))
                         * np.float32(nb - max_exact)).astype(np.int32)
    large = np.minimum(large, nb - 1)
    return base + np.where(n < max_exact, n, large)


def _head_cols(perm):
    return np.concatenate([np.arange(h * HEAD_DIM, (h + 1) * HEAD_DIM) for h in perm])


def _rope_tables(s_len):
    axis_dim = HEAD_DIM // 2
    quarter = axis_dim // 2
    t = jnp.arange(s_len)
    row = (t // GRID_W).astype(F32)
    col = (t % GRID_W).astype(F32)
    freqs = ROPE_THETA ** (-jnp.arange(0, axis_dim, 2, dtype=F32) / axis_dim)
    ang_row = row[:, None] * freqs[None, :]
    ang_col = col[:, None] * freqs[None, :]
    ang = jnp.concatenate([ang_row, ang_row, ang_col, ang_col], axis=-1)
    first = (np.arange(HEAD_DIM) % axis_dim) < quarter
    cos = jnp.cos(ang)
    sin = jnp.sin(ang)
    sdn = jnp.where(first[None, :], -sin, 0.0)
    sup = jnp.where(first[None, :], 0.0, sin)
    two = lambda a: jnp.concatenate([a, a], axis=-1)
    return two(cos), two(sdn), two(sup)


def kernel(x, norm_mix, w_in, a_sink, t5_table, b_rpb, c_q_gain, c_k_gain, out_gain_a, out_gain_b,
           out_gain_c, w_o, norm_mlp, w_up, w_down, norm_final):
    bsz, s_len, d_model = x.shape
    depth = w_in.shape[0]
    n = bsz * s_len
    scale = HEAD_DIM ** -0.5

    a_w = len(A_PERM) * HEAD_DIM
    c_w = len(C_PERM) * HEAD_DIM
    a_kv_w = a_w // 4
    b_w = a_w
    c_kv_w = c_w // 4
    off_qa, off_ka = 0, a_w
    off_qb = a_w + 2 * a_kv_w
    off_qc = off_qb + 3 * b_w
    off_kc = off_qc + c_w
    in_width = off_kc + 2 * c_kv_w
    assert in_width == w_in.shape[2]

    cols = np.arange(in_width)
    cols[off_qa:off_qa + a_w] = off_qa + _head_cols(A_PERM)
    cols[off_qc:off_qc + c_w] = off_qc + _head_cols(C_PERM)
    col_scale = np.ones((in_width,), np.float32)
    col_scale[off_qa:off_qa + a_w] = scale
    col_scale[off_qb:off_qb + b_w] = scale
    w_in_p = (w_in[:, :, cols] * col_scale).astype(BF16)

    rows_o = np.arange(w_o.shape[1])
    rows_o[:a_w] = _head_cols(A_PERM)
    rows_o[a_w + b_w:] = a_w + b_w + _head_cols(C_PERM)
    w_o_p = w_o[:, rows_o, :].astype(BF16)
    gain_a_p = out_gain_a[:, _head_cols(A_PERM)]
    gain_c_p = out_gain_c[:, _head_cols(C_PERM)]
    w_up_b = w_up.astype(BF16)
    w_down_b = w_down.astype(BF16)

    qi = np.arange(A_BLOCK)[:, None]
    kj = np.arange(3 * A_BLOCK)[None, :]
    bucket = _t5_bucket_np(kj - A_BLOCK - qi)
    bias_a = t5_table[bucket].transpose(2, 0, 1).astype(F32)
    bias_a = bias_a[np.array(A_PERM)].reshape(4, 2, A_BLOCK, 3 * A_BLOCK).transpose(1, 0, 2, 3)
    sink_p = a_sink[:, np.array(A_PERM)].astype(F32)

    cq = np.arange(GRID_W)[:, None]
    ck = np.arange(GRID_W)[None, :]
    dc = np.clip(ck - cq + NA_COLS - 1, 0, 2 * NA_COLS - 2)
    t_blocks = b_rpb[:, :, :, dc].astype(F32)
    t_pair = jnp.concatenate([t_blocks[:, :, :-1], t_blocks[:, :, 1:]], axis=-1)

    cos_t, sdn_t, sup_t = _rope_tables(s_len)
    two = lambda g: jnp.concatenate([g, g], axis=-1)
    n_qblk, n_kblk = c_w // LANES, c_kv_w // LANES

    xf = x.reshape(n, d_model)
    for l in range(depth):
        proj = _inproj(xf, norm_mix[l], w_in_p[l], tm=min(1024, n), tn=in_width // 3)
        oa = _attn_a(proj, sink_p[l], bias_a, bsz=bsz, s_len=s_len, tq=min(512, s_len))
        ob = _attn_b(proj, t_pair[l], bsz=bsz, s_len=s_len, rblk=8)
        gains = jnp.concatenate([jnp.tile(two(c_q_gain[l] * (scale * LOG2E))[None], (n_qblk, 1)),
                                 jnp.tile(two(c_k_gain[l])[None], (n_kblk, 1))], axis=0)
        qk = _c_prep(proj, gains.reshape(n_qblk + n_kblk, 1, LANES).astype(F32), cos_t, sdn_t, sup_t,
                     s_len=s_len, tr=min(1024, s_len), first=off_qc // LANES)
        tk_c = min(512, s_len // 2)
        vt = _c_vprep(proj, bsz=bsz, s_len=s_len, tk=tk_c)
        oc = _attn_c(qk, vt, bsz=bsz, s_len=s_len, tq=min(256, s_len), tk=tk_c)
        xf = _mix_out(xf, oa, ob, oc, gain_a_p[l], out_gain_b[l], gain_c_p[l], w_o_p[l], tm=min(512, n))
        xf = _mlp(xf, norm_mlp[l], w_up_b[l], w_down_b[l], norm_final,
                  tm=min(512, n), tf=512, final_norm=(l == depth - 1))
    return xf.reshape(bsz, s_len, d_model)
```

```python
import functools
import math

import numpy as np
import jax
import jax.numpy as jnp
from jax import lax
from jax.experimental import pallas as pl
from jax.experimental.pallas import tpu as pltpu

HEAD_DIM = 64
LANES = 128
WINDOW = 128
A_BLOCK = 128
T5_BUCKETS = 32
T5_MAX_DIST = 128
GRID_W = 64
NA_ROWS = 8
NA_COLS = 16
ROPE_THETA = 10000.0
EPS = 1e-6
MASK_VALUE = -1e30
LOG2E = math.log2(math.e)
VMEM_LIMIT = 56 * 1024 * 1024

F32 = jnp.float32
BF16 = jnp.bfloat16

A_PERM = (0, 4, 1, 5, 2, 6, 3, 7)
C_PERM = (0, 4, 1, 5, 2, 6, 3, 7, 8, 12, 9, 13, 10, 14, 11, 15)


def _params(*sem):
    return pltpu.CompilerParams(dimension_semantics=sem, vmem_limit_bytes=VMEM_LIMIT)


def _rms_scale(x):
    return lax.rsqrt(jnp.mean(x * x, axis=-1, keepdims=True) + EPS)


def _lo_lanes():
    return lax.broadcasted_iota(jnp.int32, (1, LANES), 1) < HEAD_DIM


def _inproj_kernel(x_ref, g_ref, w_ref, o_ref, h_ref):
    @pl.when(pl.program_id(1) == 0)
    def _():
        x = x_ref[...]
        h_ref[...] = (x * _rms_scale(x) * g_ref[...]).astype(BF16)

    o_ref[...] = jnp.dot(h_ref[...], w_ref[...], preferred_element_type=F32).astype(o_ref.dtype)


def _inproj(x, gain, w, *, tm, tn):
    n, d = x.shape
    e = w.shape[1]
    return pl.pallas_call(
        _inproj_kernel,
        grid=(n // tm, e // tn),
        in_specs=[pl.BlockSpec((tm, d), lambda i, j: (i, 0)),
                  pl.BlockSpec((1, d), lambda i, j: (0, 0)),
                  pl.BlockSpec((d, tn), lambda i, j: (0, j))],
        out_specs=pl.BlockSpec((tm, tn), lambda i, j: (i, j)),
        out_shape=jax.ShapeDtypeStruct((n, e), BF16),
        scratch_shapes=[pltpu.VMEM((tm, d), BF16)],
        compiler_params=_params("parallel", "arbitrary"),
        name="inproj",
    )(x, gain.reshape(1, d), w)


def _attn_a_kernel(sink_ref, q_ref, k_ref, v_ref, bias_ref, o_ref, *, tq, s_len):
    i = pl.program_id(1)
    lo = _lo_lanes()
    qi = lax.broadcasted_iota(jnp.int32, (A_BLOCK, 3 * A_BLOCK), 0)
    kj = lax.broadcasted_iota(jnp.int32, (A_BLOCK, 3 * A_BLOCK), 1)
    in_window = jnp.abs(kj - A_BLOCK - qi) <= WINDOW

    def window(ref, j):
        p0 = i * tq + j * A_BLOCK
        starts = [pl.multiple_of(jnp.clip(p0 + o * A_BLOCK, 0, s_len - A_BLOCK), A_BLOCK)
                  for o in (-1, 0, 1)]
        return jnp.concatenate([ref[pl.ds(st, A_BLOCK), :] for st in starts], axis=0)

    def scores(j, half):
        keep = lo if half == 0 else jnp.logical_not(lo)
        rows = slice(j * A_BLOCK, (j + 1) * A_BLOCK)
        qs = jnp.concatenate(
            [jnp.where(keep, q_ref[rows, r * LANES:(r + 1) * LANES], jnp.zeros((), BF16)) for r in range(4)],
            axis=0)
        return lax.dot_general(qs, window(k_ref, j), (((1,), (1,)), ((), ())), preferred_element_type=F32)

    def finish(j, half, s):
        key_pos = i * tq + (j - 1) * A_BLOCK + kj
        valid = in_window & (key_pos >= 0) & (key_pos < s_len)
        s = s.reshape(4, A_BLOCK, 3 * A_BLOCK) + bias_ref[half]
        s = jnp.where(valid[None], s, MASK_VALUE)
        sink = jnp.stack([jnp.full((A_BLOCK, 1), sink_ref[2 * r + half], F32) for r in range(4)])
        m = jnp.maximum(jnp.max(s, axis=-1, keepdims=True), sink)
        e = jnp.exp(s - m)
        den = jnp.sum(e, axis=-1, keepdims=True) + jnp.exp(sink - m)
        pv = jnp.dot(e.reshape(4 * A_BLOCK, 3 * A_BLOCK).astype(BF16), window(v_ref, j),
                     preferred_element_type=F32)
        return pv.reshape(4, A_BLOCK, LANES) * (1.0 / den)

    units = [(j, half) for j in range(tq // A_BLOCK) for half in (0, 1)]
    s_next = scores(*units[0])
    lo_half = None
    for n, (j, half) in enumerate(units):
        s_cur = s_next
        if n + 1 < len(units):
            s_next = scores(*units[n + 1])
        out = finish(j, half, s_cur)
        if half == 0:
            lo_half = out
        else:
            rows = slice(j * A_BLOCK, (j + 1) * A_BLOCK)
            for r in range(4):
                o_ref[rows, r * LANES:(r + 1) * LANES] = jnp.where(lo, lo_half[r], out[r])


def _attn_a(proj, sink_p, bias_p, *, bsz, s_len, tq):
    n = proj.shape[0]
    nq = s_len // tq
    kern = functools.partial(_attn_a_kernel, tq=tq, s_len=s_len)
    return pl.pallas_call(
        kern,
        grid=(bsz, nq),
        in_specs=[pl.BlockSpec(memory_space=pltpu.SMEM),
                  pl.BlockSpec((tq, 4 * LANES), lambda b, i: (b * nq + i, 0)),
                  pl.BlockSpec((s_len, LANES), lambda b, i: (b, 4)),
                  pl.BlockSpec((s_len, LANES), lambda b, i: (b, 5)),
                  pl.BlockSpec((2, 4, A_BLOCK, 3 * A_BLOCK), lambda b, i: (0, 0, 0, 0))],
        out_specs=pl.BlockSpec((tq, 4 * LANES), lambda b, i: (b * nq + i, 0)),
        out_shape=jax.ShapeDtypeStruct((n, 4 * LANES), F32),
        compiler_params=_params("parallel", "arbitrary"),
        name="attn_a",
    )(sink_p, proj, proj, proj, bias_p)


def _attn_b_kernel(q_ref, k_ref, v_ref, t_ref, o_ref, *, rblk, rows):
    r0 = pl.program_id(2) * rblk
    lo = _lo_lanes()
    nkeys = NA_ROWS * GRID_W
    cq = lax.broadcasted_iota(jnp.int32, (GRID_W, nkeys), 0)
    ck = lax.broadcasted_iota(jnp.int32, (GRID_W, nkeys), 1) % GRID_W
    cs = jnp.clip(cq - NA_COLS // 2, 0, GRID_W - NA_COLS)
    col_valid = (ck >= cs) & (ck < cs + NA_COLS)

    def first_key_row(a):
        return jnp.clip(r0 + a - NA_ROWS // 2, 0, rows - NA_ROWS)

    def window(ref, a):
        return ref[pl.ds(pl.multiple_of(first_key_row(a) * GRID_W, GRID_W), nkeys), :]

    def scores(a):
        qrow = q_ref[a * GRID_W:(a + 1) * GRID_W, :]
        zero = jnp.zeros_like(qrow)
        qs = jnp.concatenate([jnp.where(lo, qrow, zero), jnp.where(lo, zero, qrow)], axis=0)
        return lax.dot_general(qs, window(k_ref, a), (((1,), (1,)), ((), ())), preferred_element_type=F32)

    def finish(a, s):
        off = first_key_row(a) - (r0 + a) + (NA_ROWS - 1)
        bias = jnp.concatenate(
            [jnp.concatenate([t_ref[half, off + 2 * j] for j in range(NA_ROWS // 2)], axis=1)
             for half in (0, 1)], axis=0)
        s = s.reshape(2, GRID_W, nkeys) + bias.reshape(2, GRID_W, nkeys)
        s = jnp.where(col_valid[None], s, MASK_VALUE)
        m = jnp.max(s, axis=-1, keepdims=True)
        e = jnp.exp(s - m)
        den = jnp.sum(e, axis=-1, keepdims=True)
        pv = jnp.dot(e.reshape(2 * GRID_W, nkeys).astype(BF16), window(v_ref, a),
                     preferred_element_type=F32)
        pv = pv.reshape(2, GRID_W, LANES) * (1.0 / den)
        o_ref[a * GRID_W:(a + 1) * GRID_W, :] = jnp.where(lo, pv[0], pv[1])

    s_next = scores(0)
    for a in range(rblk):
        s_cur = s_next
        if a + 1 < rblk:
            s_next = scores(a + 1)
        finish(a, s_cur)


def _attn_b(proj, t_pair, *, bsz, s_len, rblk):
    n = proj.shape[0]
    rows = s_len // GRID_W
    nrb = rows // rblk
    kern = functools.partial(_attn_b_kernel, rblk=rblk, rows=rows)
    tq = rblk * GRID_W
    return pl.pallas_call(
        kern,
        grid=(bsz, 4, nrb),
        in_specs=[pl.BlockSpec((tq, LANES), lambda b, h, i: (b * nrb + i, 6 + h)),
                  pl.BlockSpec((s_len, LANES), lambda b, h, i: (b, 10 + h)),
                  pl.BlockSpec((s_len, LANES), lambda b, h, i: (b, 14 + h)),
                  pl.BlockSpec((2, 2 * NA_ROWS - 2, GRID_W, LANES), lambda b, h, i: (h, 0, 0, 0))],
        out_specs=pl.BlockSpec((tq, LANES), lambda b, h, i: (b * nrb + i, h)),
        out_shape=jax.ShapeDtypeStruct((n, 4 * LANES), F32),
        compiler_params=_params("parallel", "parallel", "arbitrary"),
        name="attn_b",
    )(proj, proj, proj, t_pair)


def _c_prep_kernel(x_ref, g_ref, cos_ref, sdn_ref, sup_ref, o_ref):
    x = x_ref[...].astype(F32)
    lo = _lo_lanes()
    xx = x * x
    ms_lo = jnp.sum(jnp.where(lo, xx, 0.0), axis=-1, keepdims=True)
    ms_hi = jnp.sum(jnp.where(lo, 0.0, xx), axis=-1, keepdims=True)
    ms = jnp.where(lo, ms_lo, ms_hi) * (1.0 / HEAD_DIM)
    y = x * lax.rsqrt(ms + EPS) * g_ref[0]
    quarter = HEAD_DIM // 4
    y_dn = pltpu.roll(y, LANES - quarter, 1)
    y_up = pltpu.roll(y, quarter, 1)
    o_ref[...] = (y * cos_ref[...] + y_dn * sdn_ref[...] + y_up * sup_ref[...]).astype(o_ref.dtype)


def _c_prep(proj, gains, cos_t, sdn_t, sup_t, *, s_len, tr, first):
    n = proj.shape[0]
    nblk = gains.shape[0]
    npos = s_len // tr
    pos_spec = pl.BlockSpec((tr, LANES), lambda i, j: (i % npos, 0))
    return pl.pallas_call(
        _c_prep_kernel,
        grid=(n // tr, nblk),
        in_specs=[pl.BlockSpec((tr, LANES), lambda i, j: (i, first + j)),
                  pl.BlockSpec((1, 1, LANES), lambda i, j: (j, 0, 0)),
                  pos_spec, pos_spec, pos_spec],
        out_specs=pl.BlockSpec((tr, LANES), lambda i, j: (i, j)),
        out_shape=jax.ShapeDtypeStruct((n, nblk * LANES), BF16),
        compiler_params=_params("parallel", "arbitrary"),
        name="c_prep",
    )(proj, gains, cos_t, sdn_t, sup_t)


def _c_vprep_kernel(x_ref, o_ref):
    xt = x_ref[...].astype(F32).T
    first = lax.broadcasted_iota(jnp.int32, (LANES, 1), 0) < HEAD_DIM
    o_ref[:LANES, :] = jnp.where(first, xt, 1.0).astype(o_ref.dtype)
    o_ref[LANES:, :] = jnp.where(first, 1.0, xt).astype(o_ref.dtype)


def _c_vprep(proj, *, bsz, s_len, tk):
    nk = s_len // tk
    v_first = proj.shape[1] // LANES - 2
    return pl.pallas_call(
        _c_vprep_kernel,
        grid=(bsz, 2, nk),
        in_specs=[pl.BlockSpec((tk, LANES), lambda b, p, i: (b * nk + i, v_first + p))],
        out_specs=pl.BlockSpec((None, None, None, 2 * LANES, tk), lambda b, p, i: (b, p, i, 0, 0)),
        out_shape=jax.ShapeDtypeStruct((bsz, 2, nk, 2 * LANES, tk), BF16),
        compiler_params=_params("parallel", "parallel", "arbitrary"),
        name="c_vprep",
    )(proj)


C_GROUP = 256


def _attn_c_kernel(q_ref, k_ref, vt_ref, o_ref, qs_ref, acc_ref, s_ref, *, tq, tk, s_len):
    lo = _lo_lanes()
    per_blk = tq // C_GROUP
    n_groups = 8 * per_blk
    for r in range(4):
        for j in range(per_blk):
            qb = q_ref[j * C_GROUP:(j + 1) * C_GROUP, r * LANES:(r + 1) * LANES]
            zero = jnp.zeros_like(qb)
            qs_ref[r * per_blk + j] = jnp.where(lo, qb, zero)
            qs_ref[(4 + r) * per_blk + j] = jnp.where(lo, zero, qb)
    acc_ref[...] = jnp.zeros(acc_ref.shape, F32)

    nk = s_len // tk

    def scores(t, slot, groups):
        kt = k_ref[pl.ds(pl.multiple_of(t * tk, tk), tk), :]
        for g in groups:
            s_ref[slot, g] = lax.dot_general(kt, qs_ref[g], (((1,), (1,)), ((), ())),
                                             preferred_element_type=F32)

    def step(t, slot, t_next, m):
        m_out = []
        for g in range(n_groups):
            scores(t_next, 1 - slot, (g,))
            st = s_ref[slot, g]
            m_new = jnp.maximum(m[g], jnp.max(st, axis=0, keepdims=True))
            alpha = jnp.exp2(m[g] - m_new)
            pt = jnp.exp2(st - m_new).astype(BF16)
            half = 0 if g < n_groups // 2 else 1
            vt = vt_ref[t, half * LANES:(half + 1) * LANES, :]
            acc_ref[g] = alpha * acc_ref[g] + jnp.dot(vt, pt, preferred_element_type=F32)
            m_out.append(m_new)
        return tuple(m_out)

    scores(0, 0, range(n_groups))

    def two_steps(u, m):
        t0 = 2 * u
        m = step(t0, 0, t0 + 1, m)
        return step(t0 + 1, 1, jnp.minimum(t0 + 2, nk - 1), m)

    m0 = tuple(jnp.full((1, C_GROUP), -jnp.inf, F32) for _ in range(n_groups))
    lax.fori_loop(0, nk // 2, two_steps, m0)
    for r in range(4):
        for j in range(per_blk):
            a_lo = acc_ref[r * per_blk + j]
            a_hi = acc_ref[(4 + r) * per_blk + j]
            o_lo = a_lo[:HEAD_DIM] * (1.0 / a_lo[HEAD_DIM:HEAD_DIM + 1])
            o_hi = a_hi[HEAD_DIM:] * (1.0 / a_hi[:1])
            o_ref[j * C_GROUP:(j + 1) * C_GROUP, r * LANES:(r + 1) * LANES] = (
                jnp.concatenate([o_lo, o_hi], axis=0).T)


def _attn_c(qk, vt, *, bsz, s_len, tq, tk):
    n = qk.shape[0]
    nq = s_len // tq
    nk = s_len // tk
    assert tq % C_GROUP == 0 and nk % 2 == 0
    n_groups = 8 * tq // C_GROUP
    kern = functools.partial(_attn_c_kernel, tq=tq, tk=tk, s_len=s_len)
    return pl.pallas_call(
        kern,
        grid=(bsz, 2, nq),
        in_specs=[pl.BlockSpec((tq, 4 * LANES), lambda b, p, i: (b * nq + i, p)),
                  pl.BlockSpec((s_len, LANES), lambda b, p, i: (b, 8 + p)),
                  pl.BlockSpec((None, None, nk, 2 * LANES, tk), lambda b, p, i: (b, p, 0, 0, 0))],
        out_specs=pl.BlockSpec((tq, 4 * LANES), lambda b, p, i: (b * nq + i, p)),
        out_shape=jax.ShapeDtypeStruct((n, 8 * LANES), F32),
        scratch_shapes=[pltpu.VMEM((n_groups, C_GROUP, LANES), BF16),
                        pltpu.VMEM((n_groups, LANES, C_GROUP), F32),
                        pltpu.VMEM((2, n_groups, tk, C_GROUP), F32)],
        compiler_params=_params("parallel", "parallel", "arbitrary"),
        name="attn_c",
    )(qk, qk, vt)


def _mix_out_kernel(x_ref, oa_ref, ob_ref, oc_ref, ga_ref, gb_ref, gc_ref, w_ref, o_ref):
    parts = []
    for o_r, g_r in ((oa_ref, ga_ref), (ob_ref, gb_ref), (oc_ref, gc_ref)):
        o = o_r[...]
        parts.append((o * _rms_scale(o) * g_r[...]).astype(BF16))
    mix = jnp.concatenate(parts, axis=-1)
    o_ref[...] = x_ref[...] + jnp.dot(mix, w_ref[...], preferred_element_type=F32)


def _mix_out(x, oa, ob, oc, ga, gb, gc, w, *, tm):
    n, d = x.shape
    row = lambda width: pl.BlockSpec((tm, width), lambda i: (i, 0))
    const = lambda shape: pl.BlockSpec(shape, lambda i: (0, 0))
    return pl.pallas_call(
        _mix_out_kernel,
        grid=(n // tm,),
        in_specs=[row(d), row(oa.shape[1]), row(ob.shape[1]), row(oc.shape[1]),
                  const((1, oa.shape[1])), const((1, ob.shape[1])), const((1, oc.shape[1])),
                  const(w.shape)],
        out_specs=row(d),
        out_shape=jax.ShapeDtypeStruct((n, d), F32),
        compiler_params=_params("parallel"),
        name="mix_out",
    )(x, oa, ob, oc, ga.reshape(1, -1), gb.reshape(1, -1), gc.reshape(1, -1), w)


def _mlp_kernel(x_ref, g_ref, wu_ref, wd_ref, gf_ref, o_ref, h_ref, *, final_norm):
    f = pl.program_id(1)

    @pl.when(f == 0)
    def _():
        x = x_ref[...]
        h_ref[...] = (x * _rms_scale(x) * g_ref[...]).astype(BF16)
        o_ref[...] = x

    u = jnp.maximum(jnp.dot(h_ref[...], wu_ref[...], preferred_element_type=F32), 0.0)
    o_ref[...] += jnp.dot((u * u).astype(BF16), wd_ref[...], preferred_element_type=F32)

    if final_norm:
        @pl.when(f == pl.num_programs(1) - 1)
        def _():
            y = o_ref[...]
            o_ref[...] = y * _rms_scale(y) * gf_ref[...]


def _mlp(x, gain, w_up, w_down, final_gain, *, tm, tf, final_norm):
    n, d = x.shape
    ff = w_up.shape[1]
    kern = functools.partial(_mlp_kernel, final_norm=final_norm)
    return pl.pallas_call(
        kern,
        grid=(n // tm, ff // tf),
        in_specs=[pl.BlockSpec((tm, d), lambda i, f: (i, 0)),
                  pl.BlockSpec((1, d), lambda i, f: (0, 0)),
                  pl.BlockSpec((d, tf), lambda i, f: (0, f)),
                  pl.BlockSpec((tf, d), lambda i, f: (f, 0)),
                  pl.BlockSpec((1, d), lambda i, f: (0, 0))],
        out_specs=pl.BlockSpec((tm, d), lambda i, f: (i, 0)),
        out_shape=jax.ShapeDtypeStruct((n, d), F32),
        scratch_shapes=[pltpu.VMEM((tm, d), BF16)],
        compiler_params=_params("parallel", "arbitrary"),
        name="mlp",
    )(x, gain.reshape(1, d), w_up, w_down, final_gain.reshape(1, d))


def _t5_bucket_np(rel):
    nb = T5_BUCKETS // 2
    max_exact = nb // 2
    base = np.where(rel > 0, nb, 0)
    n = np.abs(rel)
    nf = np.maximum(n, 1).astype(np.float32)
    large = max_exact + (np.log(nf / np.float32(max_exact)) / np.float32(math.log(T5_MAX_DIST / max_exact))
                         * np.float32(nb - max_exact)).astype(np.int32)
    large = np.minimum(large, nb - 1)
    return base + np.where(n < max_exact, n, large)


def _head_cols(perm):
    return np.concatenate([np.arange(h * HEAD_DIM, (h + 1) * HEAD_DIM) for h in perm])


def _rope_tables(s_len):
    axis_dim = HEAD_DIM // 2
    quarter = axis_dim // 2
    t = jnp.arange(s_len)
    row = (t // GRID_W).astype(F32)
    col = (t % GRID_W).astype(F32)
    freqs = ROPE_THETA ** (-jnp.arange(0, axis_dim, 2, dtype=F32) / axis_dim)
    ang_row = row[:, None] * freqs[None, :]
    ang_col = col[:, None] * freqs[None, :]
    ang = jnp.concatenate([ang_row, ang_row, ang_col, ang_col], axis=-1)
    first = (np.arange(HEAD_DIM) % axis_dim) < quarter
    cos = jnp.cos(ang)
    sin = jnp.sin(ang)
    sdn = jnp.where(first[None, :], -sin, 0.0)
    sup = jnp.where(first[None, :], 0.0, sin)
    two = lambda a: jnp.concatenate([a, a], axis=-1)
    return two(cos), two(sdn), two(sup)


def kernel(x, norm_mix, w_in, a_sink, t5_table, b_rpb, c_q_gain, c_k_gain, out_gain_a, out_gain_b,
           out_gain_c, w_o, norm_mlp, w_up, w_down, norm_final):
    bsz, s_len, d_model = x.shape
    depth = w_in.shape[0]
    n = bsz * s_len
    scale = HEAD_DIM ** -0.5

    a_w = len(A_PERM) * HEAD_DIM
    c_w = len(C_PERM) * HEAD_DIM
    a_kv_w = a_w // 4
    b_w = a_w
    c_kv_w = c_w // 4
    off_qa, off_ka = 0, a_w
    off_qb = a_w + 2 * a_kv_w
    off_qc = off_qb + 3 * b_w
    off_kc = off_qc + c_w
    in_width = off_kc + 2 * c_kv_w
    assert in_width == w_in.shape[2]

    cols = np.arange(in_width)
    cols[off_qa:off_qa + a_w] = off_qa + _head_cols(A_PERM)
    cols[off_qc:off_qc + c_w] = off_qc + _head_cols(C_PERM)
    col_scale = np.ones((in_width,), np.float32)
    col_scale[off_qa:off_qa + a_w] = scale
    col_scale[off_qb:off_qb + b_w] = scale
    w_in_p = (w_in[:, :, cols] * col_scale).astype(BF16)

    rows_o = np.arange(w_o.shape[1])
    rows_o[:a_w] = _head_cols(A_PERM)
    rows_o[a_w + b_w:] = a_w + b_w + _head_cols(C_PERM)
    w_o_p = w_o[:, rows_o, :].astype(BF16)
    gain_a_p = out_gain_a[:, _head_cols(A_PERM)]
    gain_c_p = out_gain_c[:, _head_cols(C_PERM)]
    w_up_b = w_up.astype(BF16)
    w_down_b = w_down.astype(BF16)

    qi = np.arange(A_BLOCK)[:, None]
    kj = np.arange(3 * A_BLOCK)[None, :]
    bucket = _t5_bucket_np(kj - A_BLOCK - qi)
    bias_a = t5_table[bucket].transpose(2, 0, 1).astype(F32)
    bias_a = bias_a[np.array(A_PERM)].reshape(4, 2, A_BLOCK, 3 * A_BLOCK).transpose(1, 0, 2, 3)
    sink_p = a_sink[:, np.array(A_PERM)].astype(F32)

    cq = np.arange(GRID_W)[:, None]
    ck = np.arange(GRID_W)[None, :]
    dc = np.clip(ck - cq + NA_COLS - 1, 0, 2 * NA_COLS - 2)
    t_blocks = b_rpb[:, :, :, dc].astype(F32)
    t_pair = jnp.concatenate([t_blocks[:, :, :-1], t_blocks[:, :, 1:]], axis=-1)

    cos_t, sdn_t, sup_t = _rope_tables(s_len)
    two = lambda g: jnp.concatenate([g, g], axis=-1)
    n_qblk, n_kblk = c_w // LANES, c_kv_w // LANES

    xf = x.reshape(n, d_model)
    for l in range(depth):
        proj = _inproj(xf, norm_mix[l], w_in_p[l], tm=min(1024, n), tn=in_width // 3)
        oa = _attn_a(proj, sink_p[l], bias_a, bsz=bsz, s_len=s_len, tq=min(512, s_len))
        ob = _attn_b(proj, t_pair[l], bsz=bsz, s_len=s_len, rblk=8)
        gains = jnp.concatenate([jnp.tile(two(c_q_gain[l] * (scale * LOG2E))[None], (n_qblk, 1)),
                                 jnp.tile(two(c_k_gain[l])[None], (n_kblk, 1))], axis=0)
        qk = _c_prep(proj, gains.reshape(n_qblk + n_kblk, 1, LANES).astype(F32), cos_t, sdn_t, sup_t,
                     s_len=s_len, tr=min(1024, s_len), first=off_qc // LANES)
        tk_c = min(512, s_len // 2)
        vt = _c_vprep(proj, bsz=bsz, s_len=s_len, tk=tk_c)
        oc = _attn_c(qk, vt, bsz=bsz, s_len=s_len, tq=min(256, s_len), tk=tk_c)
        xf = _mix_out(xf, oa, ob, oc, gain_a_p[l], out_gain_b[l], gain_c_p[l], w_o_p[l], tm=min(512, n))
        xf = _mlp(xf, norm_mlp[l], w_up_b[l], w_down_b[l], norm_final,
                  tm=min(512, n), tf=512, final_norm=(l == depth - 1))
    return xf.reshape(bsz, s_len, d_model)
```

```python
import functools
import math

import numpy as np
import jax
import jax.numpy as jnp
from jax import lax
from jax.experimental import pallas as pl
from jax.experimental.pallas import tpu as pltpu

HEAD_DIM = 64
LANES = 128
WINDOW = 128
A_BLOCK = 128
T5_BUCKETS = 32
T5_MAX_DIST = 128
GRID_W = 64
NA_ROWS = 8
NA_COLS = 16
ROPE_THETA = 10000.0
EPS = 1e-6
MASK_VALUE = -1e30
LOG2E = math.log2(math.e)
VMEM_LIMIT = 56 * 1024 * 1024

F32 = jnp.float32
BF16 = jnp.bfloat16


def _params(*sem):
    return pltpu.CompilerParams(dimension_semantics=sem, vmem_limit_bytes=VMEM_LIMIT)


def _rms_scale(x):
    return lax.rsqrt(jnp.mean(x * x, axis=-1, keepdims=True) + EPS)


def _lo_lanes():
    return lax.broadcasted_iota(jnp.int32, (1, LANES), 1) < HEAD_DIM


def _inproj_kernel(x_ref, g_ref, w_ref, o_ref, h_ref):
    @pl.when(pl.program_id(1) == 0)
    def _():
        x = x_ref[...]
        h_ref[...] = (x * _rms_scale(x) * g_ref[...]).astype(BF16)

    o_ref[...] = jnp.dot(h_ref[...], w_ref[...], preferred_element_type=F32).astype(o_ref.dtype)


def _inproj(x, gain, w, *, tm, tn):
    n, d = x.shape
    e = w.shape[1]
    return pl.pallas_call(
        _inproj_kernel,
        grid=(n // tm, e // tn),
        in_specs=[pl.BlockSpec((tm, d), lambda i, j: (i, 0)),
                  pl.BlockSpec((1, d), lambda i, j: (0, 0)),
                  pl.BlockSpec((d, tn), lambda i, j: (0, j))],
        out_specs=pl.BlockSpec((tm, tn), lambda i, j: (i, j)),
        out_shape=jax.ShapeDtypeStruct((n, e), BF16),
        scratch_shapes=[pltpu.VMEM((tm, d), BF16)],
        compiler_params=_params("parallel", "arbitrary"),
        name="inproj",
    )(x, gain.reshape(1, d), w)


def _attn_a_kernel(sink_ref, q_ref, k_ref, v_ref, bias_ref, o_ref, *, tq, s_len):
    i = pl.program_id(1)
    lo = _lo_lanes()
    qi = lax.broadcasted_iota(jnp.int32, (A_BLOCK, 3 * A_BLOCK), 0)
    kj = lax.broadcasted_iota(jnp.int32, (A_BLOCK, 3 * A_BLOCK), 1)
    in_window = jnp.abs(kj - A_BLOCK - qi) <= WINDOW

    def window(ref, j):
        p0 = i * tq + j * A_BLOCK
        starts = [pl.multiple_of(jnp.clip(p0 + o * A_BLOCK, 0, s_len - A_BLOCK), A_BLOCK)
                  for o in (-1, 0, 1)]
        return jnp.concatenate([ref[pl.ds(st, A_BLOCK), :] for st in starts], axis=0)

    def scores(j, half):
        keep = lo if half == 0 else jnp.logical_not(lo)
        rows = slice(j * A_BLOCK, (j + 1) * A_BLOCK)
        qs = jnp.concatenate(
            [jnp.where(keep, q_ref[rows, r * LANES:(r + 1) * LANES], jnp.zeros((), BF16)) for r in range(4)],
            axis=0)
        return lax.dot_general(qs, window(k_ref, j), (((1,), (1,)), ((), ())), preferred_element_type=F32)

    def finish(j, half, s):
        key_pos = i * tq + (j - 1) * A_BLOCK + kj
        valid = in_window & (key_pos >= 0) & (key_pos < s_len)
        s = s.reshape(4, A_BLOCK, 3 * A_BLOCK) + bias_ref[half]
        s = jnp.where(valid[None], s, MASK_VALUE)
        sink = jnp.stack([jnp.full((A_BLOCK, 1), sink_ref[4 * half + r], F32) for r in range(4)])
        m = jnp.maximum(jnp.max(s, axis=-1, keepdims=True), sink)
        e = jnp.exp(s - m)
        den = jnp.sum(e, axis=-1, keepdims=True) + jnp.exp(sink - m)
        pv = jnp.dot(e.reshape(4 * A_BLOCK, 3 * A_BLOCK).astype(BF16), window(v_ref, j),
                     preferred_element_type=F32)
        return pv.reshape(4, A_BLOCK, LANES) * (1.0 / den)

    units = [(j, half) for j in range(tq // A_BLOCK) for half in (0, 1)]
    s_next = scores(*units[0])
    lo_half = None
    for n, (j, half) in enumerate(units):
        s_cur = s_next
        if n + 1 < len(units):
            s_next = scores(*units[n + 1])
        out = finish(j, half, s_cur)
        if half == 0:
            lo_half = out
        else:
            rows = slice(j * A_BLOCK, (j + 1) * A_BLOCK)
            for r in range(4):
                o_ref[rows, r * LANES:(r + 1) * LANES] = jnp.where(lo, lo_half[r], out[r])


def _attn_a(proj, sink_p, bias_p, *, bsz, s_len, tq):
    n = proj.shape[0]
    nq = s_len // tq
    kern = functools.partial(_attn_a_kernel, tq=tq, s_len=s_len)
    return pl.pallas_call(
        kern,
        grid=(bsz, nq),
        in_specs=[pl.BlockSpec(memory_space=pltpu.SMEM),
                  pl.BlockSpec((tq, 4 * LANES), lambda b, i: (b * nq + i, 0)),
                  pl.BlockSpec((s_len, LANES), lambda b, i: (b, 4)),
                  pl.BlockSpec((s_len, LANES), lambda b, i: (b, 5)),
                  pl.BlockSpec((2, 4, A_BLOCK, 3 * A_BLOCK), lambda b, i: (0, 0, 0, 0))],
        out_specs=pl.BlockSpec((tq, 4 * LANES), lambda b, i: (b * nq + i, 0)),
        out_shape=jax.ShapeDtypeStruct((n, 4 * LANES), F32),
        compiler_params=_params("parallel", "arbitrary"),
        name="attn_a",
    )(sink_p, proj, proj, proj, bias_p)


def _attn_b_kernel(q_ref, k_ref, v_ref, t_ref, o_ref, *, rblk, rows):
    r0 = pl.program_id(2) * rblk
    lo = _lo_lanes()
    nkeys = NA_ROWS * GRID_W
    cq = lax.broadcasted_iota(jnp.int32, (GRID_W, nkeys), 0)
    ck = lax.broadcasted_iota(jnp.int32, (GRID_W, nkeys), 1) % GRID_W
    cs = jnp.clip(cq - NA_COLS // 2, 0, GRID_W - NA_COLS)
    col_valid = (ck >= cs) & (ck < cs + NA_COLS)

    def first_key_row(a):
        return jnp.clip(r0 + a - NA_ROWS // 2, 0, rows - NA_ROWS)

    def window(ref, a):
        return ref[pl.ds(pl.multiple_of(first_key_row(a) * GRID_W, GRID_W), nkeys), :]

    def scores(a):
        qrow = q_ref[a * GRID_W:(a + 1) * GRID_W, :]
        zero = jnp.zeros_like(qrow)
        qs = jnp.concatenate([jnp.where(lo, qrow, zero), jnp.where(lo, zero, qrow)], axis=0)
        return lax.dot_general(qs, window(k_ref, a), (((1,), (1,)), ((), ())), preferred_element_type=F32)

    def finish(a, s):
        off = first_key_row(a) - (r0 + a) + (NA_ROWS - 1)
        bias = jnp.concatenate(
            [jnp.concatenate([t_ref[half, off + 2 * j] for j in range(NA_ROWS // 2)], axis=1)
             for half in (0, 1)], axis=0)
        s = s.reshape(2, GRID_W, nkeys) + bias.reshape(2, GRID_W, nkeys)
        s = jnp.where(col_valid[None], s, MASK_VALUE)
        m = jnp.max(s, axis=-1, keepdims=True)
        e = jnp.exp(s - m)
        den = jnp.sum(e, axis=-1, keepdims=True)
        pv = jnp.dot(e.reshape(2 * GRID_W, nkeys).astype(BF16), window(v_ref, a),
                     preferred_element_type=F32)
        pv = pv.reshape(2, GRID_W, LANES) * (1.0 / den)
        o_ref[a * GRID_W:(a + 1) * GRID_W, :] = jnp.where(lo, pv[0], pv[1])

    s_next = scores(0)
    for a in range(rblk):
        s_cur = s_next
        if a + 1 < rblk:
            s_next = scores(a + 1)
        finish(a, s_cur)


def _attn_b(proj, t_pair, *, bsz, s_len, rblk):
    n = proj.shape[0]
    rows = s_len // GRID_W
    nrb = rows // rblk
    kern = functools.partial(_attn_b_kernel, rblk=rblk, rows=rows)
    tq = rblk * GRID_W
    return pl.pallas_call(
        kern,
        grid=(bsz, 4, nrb),
        in_specs=[pl.BlockSpec((tq, LANES), lambda b, h, i: (b * nrb + i, 6 + h)),
                  pl.BlockSpec((s_len, LANES), lambda b, h, i: (b, 10 + h)),
                  pl.BlockSpec((s_len, LANES), lambda b, h, i: (b, 14 + h)),
                  pl.BlockSpec((2, 2 * NA_ROWS - 2, GRID_W, LANES), lambda b, h, i: (h, 0, 0, 0))],
        out_specs=pl.BlockSpec((tq, LANES), lambda b, h, i: (b * nrb + i, h)),
        out_shape=jax.ShapeDtypeStruct((n, 4 * LANES), F32),
        compiler_params=_params("parallel", "parallel", "arbitrary"),
        name="attn_b",
    )(proj, proj, proj, t_pair)


def _c_prep_kernel(x_ref, g_ref, e_ref, cos_ref, sin_ref, o_ref):
    x = x_ref[...].astype(F32)
    xx = x * x
    xx_hi = xx.astype(BF16)
    xx_lo = (xx - xx_hi.astype(F32)).astype(BF16)
    ssq = jnp.dot(jnp.concatenate([xx_hi, xx_lo], axis=1), e_ref[...], preferred_element_type=F32)
    y = x * lax.rsqrt(ssq * (1.0 / HEAD_DIM) + EPS) * g_ref[0]
    o_ref[...] = (y * cos_ref[...] + pltpu.roll(y, LANES // 2, 1) * sin_ref[...]).astype(o_ref.dtype)


def _c_prep(proj, gains, same_head, cos_t, sin_t, *, s_len, tr, first):
    n = proj.shape[0]
    nblk = gains.shape[0]
    npos = s_len // tr
    pos_spec = pl.BlockSpec((tr, LANES), lambda i, j: (i % npos, 0))
    return pl.pallas_call(
        _c_prep_kernel,
        grid=(n // tr, nblk),
        in_specs=[pl.BlockSpec((tr, LANES), lambda i, j: (i, first + j)),
                  pl.BlockSpec((1, 1, LANES), lambda i, j: (j, 0, 0)),
                  pl.BlockSpec((2 * LANES, LANES), lambda i, j: (0, 0)),
                  pos_spec, pos_spec],
        out_specs=pl.BlockSpec((tr, LANES), lambda i, j: (i, j)),
        out_shape=jax.ShapeDtypeStruct((n, nblk * LANES), BF16),
        compiler_params=_params("parallel", "arbitrary"),
        name="c_prep",
    )(proj, gains, same_head, cos_t, sin_t)


VT_ROWS = HEAD_DIM + 16


def _c_vprep_kernel(x_ref, o_ref):
    xt = x_ref[...].astype(F32).T
    ones = jnp.ones((VT_ROWS - HEAD_DIM, xt.shape[1]), o_ref.dtype)
    for half in (0, 1):
        o_ref[half * VT_ROWS:half * VT_ROWS + HEAD_DIM, :] = (
            xt[half * HEAD_DIM:(half + 1) * HEAD_DIM].astype(o_ref.dtype))
        o_ref[half * VT_ROWS + HEAD_DIM:(half + 1) * VT_ROWS, :] = ones


def _c_vprep(proj, *, bsz, s_len, tk):
    nk = s_len // tk
    v_first = proj.shape[1] // LANES - 2
    return pl.pallas_call(
        _c_vprep_kernel,
        grid=(bsz, 2, nk),
        in_specs=[pl.BlockSpec((tk, LANES), lambda b, p, i: (b * nk + i, v_first + p))],
        out_specs=pl.BlockSpec((None, None, None, 2 * VT_ROWS, tk), lambda b, p, i: (b, p, i, 0, 0)),
        out_shape=jax.ShapeDtypeStruct((bsz, 2, nk, 2 * VT_ROWS, tk), BF16),
        compiler_params=_params("parallel", "parallel", "arbitrary"),
        name="c_vprep",
    )(proj)


C_GROUP = 256


def _attn_c_kernel(q_ref, k_ref, vt_ref, o_ref, qs_ref, acc_ref, s_ref, *, tq, tk, s_len):
    lo = _first_head_lanes()
    per_blk = tq // C_GROUP
    n_groups = 8 * per_blk
    for r in range(4):
        for j in range(per_blk):
            qb = q_ref[j * C_GROUP:(j + 1) * C_GROUP, r * LANES:(r + 1) * LANES]
            zero = jnp.zeros_like(qb)
            qs_ref[r * per_blk + j] = jnp.where(lo, qb, zero)
            qs_ref[(4 + r) * per_blk + j] = jnp.where(lo, zero, qb)
    acc_ref[...] = jnp.zeros(acc_ref.shape, F32)

    nk = s_len // tk

    def scores(t, slot, groups):
        kt = k_ref[pl.ds(pl.multiple_of(t * tk, tk), tk), :]
        for g in groups:
            s_ref[slot, g] = lax.dot_general(kt, qs_ref[g], (((1,), (1,)), ((), ())),
                                             preferred_element_type=F32)

    def step(t, slot, t_next, m):
        m_out = []
        for g in range(n_groups):
            scores(t_next, 1 - slot, (g,))
            st = s_ref[slot, g]
            m_new = jnp.maximum(m[g], jnp.max(st, axis=0, keepdims=True))
            alpha = jnp.exp2(m[g] - m_new)
            pt = jnp.exp2(st - m_new).astype(BF16)
            half = 0 if g < n_groups // 2 else 1
            vt = vt_ref[t, half * VT_ROWS:(half + 1) * VT_ROWS, :]
            acc_ref[g] = alpha * acc_ref[g] + jnp.dot(vt, pt, preferred_element_type=F32)
            m_out.append(m_new)
        return tuple(m_out)

    scores(0, 0, range(n_groups))

    def two_steps(u, m):
        t0 = 2 * u
        m = step(t0, 0, t0 + 1, m)
        return step(t0 + 1, 1, jnp.minimum(t0 + 2, nk - 1), m)

    m0 = tuple(jnp.full((1, C_GROUP), -jnp.inf, F32) for _ in range(n_groups))
    lax.fori_loop(0, nk // 2, two_steps, m0)
    for r in range(4):
        for j in range(per_blk):
            a_lo = acc_ref[r * per_blk + j]
            a_hi = acc_ref[(4 + r) * per_blk + j]
            o_lo = a_lo[:HEAD_DIM] * (1.0 / a_lo[HEAD_DIM:HEAD_DIM + 1])
            o_hi = a_hi[:HEAD_DIM] * (1.0 / a_hi[HEAD_DIM:HEAD_DIM + 1])
            o_ref[j * C_GROUP:(j + 1) * C_GROUP, r * LANES:(r + 1) * LANES] = (
                jnp.concatenate([o_lo, o_hi], axis=0).T)


def _attn_c(qk, vt, *, bsz, s_len, tq, tk):
    n = qk.shape[0]
    nq = s_len // tq
    nk = s_len // tk
    assert tq % C_GROUP == 0 and nk % 2 == 0
    n_groups = 8 * tq // C_GROUP
    kern = functools.partial(_attn_c_kernel, tq=tq, tk=tk, s_len=s_len)
    return pl.pallas_call(
        kern,
        grid=(bsz, 2, nq),
        in_specs=[pl.BlockSpec((tq, 4 * LANES), lambda b, p, i: (b * nq + i, p)),
                  pl.BlockSpec((s_len, LANES), lambda b, p, i: (b, 8 + p)),
                  pl.BlockSpec((None, None, nk, 2 * VT_ROWS, tk), lambda b, p, i: (b, p, 0, 0, 0))],
        out_specs=pl.BlockSpec((tq, 4 * LANES), lambda b, p, i: (b * nq + i, p)),
        out_shape=jax.ShapeDtypeStruct((n, 8 * LANES), F32),
        scratch_shapes=[pltpu.VMEM((n_groups, C_GROUP, LANES), BF16),
                        pltpu.VMEM((n_groups, VT_ROWS, C_GROUP), F32),
                        pltpu.VMEM((2, n_groups, tk, C_GROUP), F32)],
        compiler_params=_params("parallel", "parallel", "arbitrary"),
        name="attn_c",
    )(qk, qk, vt)


def _mix_out_kernel(x_ref, oa_ref, ob_ref, oc_ref, ga_ref, gb_ref, gc_ref, w_ref, o_ref):
    parts = []
    for o_r, g_r in ((oa_ref, ga_ref), (ob_ref, gb_ref), (oc_ref, gc_ref)):
        o = o_r[...]
        parts.append((o * _rms_scale(o) * g_r[...]).astype(BF16))
    mix = jnp.concatenate(parts, axis=-1)
    o_ref[...] = x_ref[...] + jnp.dot(mix, w_ref[...], preferred_element_type=F32)


def _mix_out(x, oa, ob, oc, ga, gb, gc, w, *, tm):
    n, d = x.shape
    row = lambda width: pl.BlockSpec((tm, width), lambda i: (i, 0))
    const = lambda shape: pl.BlockSpec(shape, lambda i: (0, 0))
    return pl.pallas_call(
        _mix_out_kernel,
        grid=(n // tm,),
        in_specs=[row(d), row(oa.shape[1]), row(ob.shape[1]), row(oc.shape[1]),
                  const((1, oa.shape[1])), const((1, ob.shape[1])), const((1, oc.shape[1])),
                  const(w.shape)],
        out_specs=row(d),
        out_shape=jax.ShapeDtypeStruct((n, d), F32),
        compiler_params=_params("parallel"),
        name="mix_out",
    )(x, oa, ob, oc, ga.reshape(1, -1), gb.reshape(1, -1), gc.reshape(1, -1), w)


def _mlp_kernel(x_ref, g_ref, wu_ref, wd_ref, gf_ref, o_ref, h_ref, *, final_norm):
    f = pl.program_id(1)

    @pl.when(f == 0)
    def _():
        x = x_ref[...]
        h_ref[...] = (x * _rms_scale(x) * g_ref[...]).astype(BF16)
        o_ref[...] = x

    u = jnp.maximum(jnp.dot(h_ref[...], wu_ref[...], preferred_element_type=F32), 0.0)
    o_ref[...] += jnp.dot((u * u).astype(BF16), wd_ref[...], preferred_element_type=F32)

    if final_norm:
        @pl.when(f == pl.num_programs(1) - 1)
        def _():
            y = o_ref[...]
            o_ref[...] = y * _rms_scale(y) * gf_ref[...]


def _mlp(x, gain, w_up, w_down, final_gain, *, tm, tf, final_norm):
    n, d = x.shape
    ff = w_up.shape[1]
    kern = functools.partial(_mlp_kernel, final_norm=final_norm)
    return pl.pallas_call(
        kern,
        grid=(n // tm, ff // tf),
        in_specs=[pl.BlockSpec((tm, d), lambda i, f: (i, 0)),
                  pl.BlockSpec((1, d), lambda i, f: (0, 0)),
                  pl.BlockSpec((d, tf), lambda i, f: (0, f)),
                  pl.BlockSpec((tf, d), lambda i, f: (f, 0)),
                  pl.BlockSpec((1, d), lambda i, f: (0, 0))],
        out_specs=pl.BlockSpec((tm, d), lambda i, f: (i, 0)),
        out_shape=jax.ShapeDtypeStruct((n, d), F32),
        scratch_shapes=[pltpu.VMEM((tm, d), BF16)],
        compiler_params=_params("parallel", "arbitrary"),
        name="mlp",
    )(x, gain.reshape(1, d), w_up, w_down, final_gain.reshape(1, d))


def _t5_bucket_np(rel):
    nb = T5_BUCKETS // 2
    max_exact = nb // 2
    base = np.where(rel > 0, nb, 0)
    n = np.abs(rel)
    nf = np.maximum(n, 1).astype(np.float32)
    large = max_exact + (np.log(nf / np.float32(max_exact)) / np.float32(math.log(T5_MAX_DIST / max_exact))
                         * np.float32(nb - max_exact)).astype(np.int32)
    large = np.minimum(large, nb - 1)
    return base + np.where(n < max_exact, n, large)


def _pair_heads(a, axis):
    shape = a.shape
    halves = shape[axis] // (8 * HEAD_DIM)
    a = a.reshape(shape[:axis] + (halves, 2, 4, HEAD_DIM) + shape[axis + 1:])
    return jnp.swapaxes(a, axis + 1, axis + 2).reshape(shape)


def _split_rotary(a, axis):
    shape = a.shape
    a = a.reshape(shape[:axis] + (shape[axis] // LANES, 2, 2, 2, HEAD_DIM // 4) + shape[axis + 1:])
    perm = (tuple(range(axis + 1)) + (axis + 3, axis + 1, axis + 2, axis + 4)
            + tuple(range(axis + 5, a.ndim)))
    return jnp.transpose(a, perm).reshape(shape)


def _first_head_lanes():
    lane = lax.broadcasted_iota(jnp.int32, (1, LANES), 1)
    return (lane // (HEAD_DIM // 2)) % 2 == 0


def _rope_tables(s_len):
    axis_dim = HEAD_DIM // 2
    t = jnp.arange(s_len)
    row = (t // GRID_W).astype(F32)
    col = (t % GRID_W).astype(F32)
    freqs = ROPE_THETA ** (-jnp.arange(0, axis_dim, 2, dtype=F32) / axis_dim)
    ang = jnp.concatenate([row[:, None] * freqs[None, :], col[:, None] * freqs[None, :]], axis=-1)
    ang = jnp.tile(ang, (1, 4))
    sign = np.where(np.arange(LANES) < LANES // 2, -1.0, 1.0).astype(np.float32)
    return jnp.cos(ang), jnp.sin(ang) * sign


def kernel(x, norm_mix, w_in, a_sink, t5_table, b_rpb, c_q_gain, c_k_gain, out_gain_a, out_gain_b,
           out_gain_c, w_o, norm_mlp, w_up, w_down, norm_final):
    bsz, s_len, d_model = x.shape
    depth = w_in.shape[0]
    n = bsz * s_len
    scale = HEAD_DIM ** -0.5

    a_w = out_gain_a.shape[1]
    b_w = out_gain_b.shape[1]
    c_w = out_gain_c.shape[1]
    a_kv_w = a_w // 4
    c_kv_w = c_w // 4
    off_qb = a_w + 2 * a_kv_w
    off_qc = off_qb + 3 * b_w
    in_width = off_qc + c_w + 2 * c_kv_w
    assert in_width == w_in.shape[2] and a_w == 8 * HEAD_DIM and c_w == 16 * HEAD_DIM

    def in_weight(w):
        off_vc = off_qc + c_w + c_kv_w
        parts = [_pair_heads(w[:, :a_w], 1) * scale, w[:, a_w:off_qb],
                 w[:, off_qb:off_qb + b_w] * scale, w[:, off_qb + b_w:off_qc],
                 _split_rotary(_pair_heads(w[:, off_qc:off_qc + c_w], 1), 1),
                 _split_rotary(w[:, off_qc + c_w:off_vc], 1), w[:, off_vc:]]
        return jnp.concatenate(parts, axis=1).astype(BF16)

    def out_weight(w):
        parts = [_pair_heads(w[:a_w], 0), w[a_w:a_w + b_w], _pair_heads(w[a_w + b_w:], 0)]
        return jnp.concatenate(parts, axis=0).astype(BF16)

    qi = np.arange(A_BLOCK)[:, None]
    kj = np.arange(3 * A_BLOCK)[None, :]
    bucket = _t5_bucket_np(kj - A_BLOCK - qi)
    bias_a = t5_table[bucket].transpose(2, 0, 1).astype(F32)
    bias_a = bias_a.reshape(2, 4, A_BLOCK, 3 * A_BLOCK)

    cq = np.arange(GRID_W)[:, None]
    ck = np.arange(GRID_W)[None, :]
    dc = np.clip(ck - cq + NA_COLS - 1, 0, 2 * NA_COLS - 2)
    t_blocks = b_rpb[:, :, :, dc].astype(F32)
    t_pair = jnp.concatenate([t_blocks[:, :, :-1], t_blocks[:, :, 1:]], axis=-1)

    cos_t, sin_t = _rope_tables(s_len)
    two = lambda g: _split_rotary(jnp.concatenate([g, g], axis=-1), 0)
    n_qblk, n_kblk = c_w // LANES, c_kv_w // LANES
    head_of_lane = (np.arange(LANES) // (HEAD_DIM // 2)) % 2
    same_head = jnp.asarray(np.tile(head_of_lane[:, None] == head_of_lane[None, :], (2, 1)), BF16)

    xf = x.reshape(n, d_model)
    for l in range(depth):
        proj = _inproj(xf, norm_mix[l], in_weight(w_in[l]), tm=min(1024, n), tn=in_width // 3)
        oa = _attn_a(proj, a_sink[l].astype(F32), bias_a, bsz=bsz, s_len=s_len, tq=min(512, s_len))
        ob = _attn_b(proj, t_pair[l], bsz=bsz, s_len=s_len, rblk=8)
        gains = jnp.concatenate([jnp.tile(two(c_q_gain[l] * (scale * LOG2E))[None], (n_qblk, 1)),
                                 jnp.tile(two(c_k_gain[l])[None], (n_kblk, 1))], axis=0)
        qk = _c_prep(proj, gains.reshape(n_qblk + n_kblk, 1, LANES).astype(F32), same_head, cos_t, sin_t,
                     s_len=s_len, tr=min(1024, s_len), first=off_qc // LANES)
        tk_c = min(512, s_len // 2)
        vt = _c_vprep(proj, bsz=bsz, s_len=s_len, tk=tk_c)
        oc = _attn_c(qk, vt, bsz=bsz, s_len=s_len, tq=min(256, s_len), tk=tk_c)
        xf = _mix_out(xf, oa, ob, oc, _pair_heads(out_gain_a[l], 0), out_gain_b[l],
                      _pair_heads(out_gain_c[l], 0), out_weight(w_o[l]), tm=min(512, n))
        xf = _mlp(xf, norm_mlp[l], w_up[l].astype(BF16), w_down[l].astype(BF16), norm_final,
                  tm=min(512, n), tf=512, final_norm=(l == depth - 1))
    return xf.reshape(bsz, s_len, d_model)
```

```python
import functools
import math

import numpy as np
import jax
import jax.numpy as jnp
from jax import lax
from jax.experimental import pallas as pl
from jax.experimental.pallas import tpu as pltpu

HEAD_DIM = 64
LANES = 128
WINDOW = 128
A_BLOCK = 128
T5_BUCKETS = 32
T5_MAX_DIST = 128
GRID_W = 64
NA_ROWS = 8
NA_COLS = 16
ROPE_THETA = 10000.0
EPS = 1e-6
MASK_VALUE = -1e30
LOG2E = math.log2(math.e)
VMEM_LIMIT = 56 * 1024 * 1024

F32 = jnp.float32
BF16 = jnp.bfloat16


def _params(*sem):
    return pltpu.CompilerParams(dimension_semantics=sem, vmem_limit_bytes=VMEM_LIMIT)


def _rms_scale(x):
    return lax.rsqrt(jnp.mean(x * x, axis=-1, keepdims=True) + EPS)


def _lo_lanes():
    return lax.broadcasted_iota(jnp.int32, (1, LANES), 1) < HEAD_DIM


def _inproj_kernel(x_ref, g_ref, w_ref, o_ref, h_ref):
    @pl.when(pl.program_id(1) == 0)
    def _():
        x = x_ref[...]
        h_ref[...] = (x * _rms_scale(x) * g_ref[...]).astype(BF16)

    o_ref[...] = jnp.dot(h_ref[...], w_ref[...], preferred_element_type=F32).astype(o_ref.dtype)


def _inproj(x, gain, w, layer, *, tm, tn):
    n, d = x.shape
    e = w.shape[2]
    return pl.pallas_call(
        _inproj_kernel,
        grid=(n // tm, e // tn),
        in_specs=[pl.BlockSpec((tm, d), lambda i, j: (i, 0)),
                  pl.BlockSpec((1, d), lambda i, j: (0, 0)),
                  pl.BlockSpec((None, d, tn), lambda i, j: (layer, 0, j))],
        out_specs=pl.BlockSpec((tm, tn), lambda i, j: (i, j)),
        out_shape=jax.ShapeDtypeStruct((n, e), BF16),
        scratch_shapes=[pltpu.VMEM((tm, d), BF16)],
        compiler_params=_params("parallel", "arbitrary"),
        name="inproj",
    )(x, gain.reshape(1, d), w)


def _attn_a_kernel(sink_ref, q_ref, k_ref, v_ref, bias_ref, o_ref, *, tq, s_len):
    i = pl.program_id(1)
    lo = _lo_lanes()
    qi = lax.broadcasted_iota(jnp.int32, (A_BLOCK, 3 * A_BLOCK), 0)
    kj = lax.broadcasted_iota(jnp.int32, (A_BLOCK, 3 * A_BLOCK), 1)
    in_window = jnp.abs(kj - A_BLOCK - qi) <= WINDOW

    def window(ref, j):
        p0 = i * tq + j * A_BLOCK
        starts = [pl.multiple_of(jnp.clip(p0 + o * A_BLOCK, 0, s_len - A_BLOCK), A_BLOCK)
                  for o in (-1, 0, 1)]
        return jnp.concatenate([ref[pl.ds(st, A_BLOCK), :] for st in starts], axis=0)

    def scores(j, half):
        keep = lo if half == 0 else jnp.logical_not(lo)
        rows = slice(j * A_BLOCK, (j + 1) * A_BLOCK)
        qs = jnp.concatenate(
            [jnp.where(keep, q_ref[rows, r * LANES:(r + 1) * LANES], jnp.zeros((), BF16)) for r in range(4)],
            axis=0)
        return lax.dot_general(qs, window(k_ref, j), (((1,), (1,)), ((), ())), preferred_element_type=F32)

    def finish(j, half, s):
        key_pos = i * tq + (j - 1) * A_BLOCK + kj
        valid = in_window & (key_pos >= 0) & (key_pos < s_len)
        s = s.reshape(4, A_BLOCK, 3 * A_BLOCK) + bias_ref[half]
        s = jnp.where(valid[None], s, MASK_VALUE)
        sink = jnp.stack([jnp.full((A_BLOCK, 1), sink_ref[4 * half + r], F32) for r in range(4)])
        m = jnp.maximum(jnp.max(s, axis=-1, keepdims=True), sink)
        e = jnp.exp(s - m)
        den = jnp.sum(e, axis=-1, keepdims=True) + jnp.exp(sink - m)
        pv = jnp.dot(e.reshape(4 * A_BLOCK, 3 * A_BLOCK).astype(BF16), window(v_ref, j),
                     preferred_element_type=F32)
        return pv.reshape(4, A_BLOCK, LANES) * (1.0 / den)

    units = [(j, half) for j in range(tq // A_BLOCK) for half in (0, 1)]
    s_next = scores(*units[0])
    lo_half = None
    for n, (j, half) in enumerate(units):
        s_cur = s_next
        if n + 1 < len(units):
            s_next = scores(*units[n + 1])
        out = finish(j, half, s_cur)
        if half == 0:
            lo_half = out
        else:
            rows = slice(j * A_BLOCK, (j + 1) * A_BLOCK)
            for r in range(4):
                o_ref[rows, r * LANES:(r + 1) * LANES] = jnp.where(lo, lo_half[r], out[r])


def _attn_a(proj, sink_p, bias_p, *, bsz, s_len, tq):
    n = proj.shape[0]
    nq = s_len // tq
    kern = functools.partial(_attn_a_kernel, tq=tq, s_len=s_len)
    return pl.pallas_call(
        kern,
        grid=(bsz, nq),
        in_specs=[pl.BlockSpec(memory_space=pltpu.SMEM),
                  pl.BlockSpec((tq, 4 * LANES), lambda b, i: (b * nq + i, 0)),
                  pl.BlockSpec((s_len, LANES), lambda b, i: (b, 4)),
                  pl.BlockSpec((s_len, LANES), lambda b, i: (b, 5)),
                  pl.BlockSpec((2, 4, A_BLOCK, 3 * A_BLOCK), lambda b, i: (0, 0, 0, 0))],
        out_specs=pl.BlockSpec((tq, 4 * LANES), lambda b, i: (b * nq + i, 0)),
        out_shape=jax.ShapeDtypeStruct((n, 4 * LANES), F32),
        compiler_params=_params("parallel", "arbitrary"),
        name="attn_a",
    )(sink_p, proj, proj, proj, bias_p)


def _attn_b_kernel(q_ref, k_ref, v_ref, t_ref, o_ref, *, rblk, rows):
    r0 = pl.program_id(2) * rblk
    lo = _lo_lanes()
    nkeys = NA_ROWS * GRID_W
    cq = lax.broadcasted_iota(jnp.int32, (GRID_W, nkeys), 0)
    ck = lax.broadcasted_iota(jnp.int32, (GRID_W, nkeys), 1) % GRID_W
    cs = jnp.clip(cq - NA_COLS // 2, 0, GRID_W - NA_COLS)
    col_valid = (ck >= cs) & (ck < cs + NA_COLS)

    def first_key_row(a):
        return jnp.clip(r0 + a - NA_ROWS // 2, 0, rows - NA_ROWS)

    def window(ref, a):
        return ref[pl.ds(pl.multiple_of(first_key_row(a) * GRID_W, GRID_W), nkeys), :]

    def scores(a):
        qrow = q_ref[a * GRID_W:(a + 1) * GRID_W, :]
        zero = jnp.zeros_like(qrow)
        qs = jnp.concatenate([jnp.where(lo, qrow, zero), jnp.where(lo, zero, qrow)], axis=0)
        return lax.dot_general(qs, window(k_ref, a), (((1,), (1,)), ((), ())), preferred_element_type=F32)

    def finish(a, s):
        off = first_key_row(a) - (r0 + a) + (NA_ROWS - 1)
        bias = jnp.concatenate(
            [jnp.concatenate([t_ref[half, off + 2 * j] for j in range(NA_ROWS // 2)], axis=1)
             for half in (0, 1)], axis=0)
        s = s.reshape(2, GRID_W, nkeys) + bias.reshape(2, GRID_W, nkeys)
        s = jnp.where(col_valid[None], s, MASK_VALUE)
        m = jnp.max(s, axis=-1, keepdims=True)
        e = jnp.exp(s - m)
        den = jnp.sum(e, axis=-1, keepdims=True)
        pv = jnp.dot(e.reshape(2 * GRID_W, nkeys).astype(BF16), window(v_ref, a),
                     preferred_element_type=F32)
        pv = pv.reshape(2, GRID_W, LANES) * (1.0 / den)
        o_ref[a * GRID_W:(a + 1) * GRID_W, :] = jnp.where(lo, pv[0], pv[1])

    s_next = scores(0)
    for a in range(rblk):
        s_cur = s_next
        if a + 1 < rblk:
            s_next = scores(a + 1)
        finish(a, s_cur)


def _attn_b(proj, t_pair, *, bsz, s_len, rblk):
    n = proj.shape[0]
    rows = s_len // GRID_W
    nrb = rows // rblk
    kern = functools.partial(_attn_b_kernel, rblk=rblk, rows=rows)
    tq = rblk * GRID_W
    return pl.pallas_call(
        kern,
        grid=(bsz, 4, nrb),
        in_specs=[pl.BlockSpec((tq, LANES), lambda b, h, i: (b * nrb + i, 6 + h)),
                  pl.BlockSpec((s_len, LANES), lambda b, h, i: (b, 10 + h)),
                  pl.BlockSpec((s_len, LANES), lambda b, h, i: (b, 14 + h)),
                  pl.BlockSpec((2, 2 * NA_ROWS - 2, GRID_W, LANES), lambda b, h, i: (h, 0, 0, 0))],
        out_specs=pl.BlockSpec((tq, LANES), lambda b, h, i: (b * nrb + i, h)),
        out_shape=jax.ShapeDtypeStruct((n, 4 * LANES), F32),
        compiler_params=_params("parallel", "parallel", "arbitrary"),
        name="attn_b",
    )(proj, proj, proj, t_pair)


def _c_prep_kernel(x_ref, g_ref, e_ref, cos_ref, sin_ref, o_ref):
    x = x_ref[...].astype(F32)
    xx = x * x
    xx_hi = xx.astype(BF16)
    xx_lo = (xx - xx_hi.astype(F32)).astype(BF16)
    ssq = jnp.dot(jnp.concatenate([xx_hi, xx_lo], axis=1), e_ref[...], preferred_element_type=F32)
    y = x * lax.rsqrt(ssq * (1.0 / HEAD_DIM) + EPS) * g_ref[0]
    o_ref[...] = (y * cos_ref[...] + pltpu.roll(y, LANES // 2, 1) * sin_ref[...]).astype(o_ref.dtype)


def _c_prep(proj, gains, same_head, cos_t, sin_t, *, s_len, tr, first):
    n = proj.shape[0]
    nblk = gains.shape[0]
    npos = s_len // tr
    pos_spec = pl.BlockSpec((tr, LANES), lambda i, j: (i % npos, 0))
    return pl.pallas_call(
        _c_prep_kernel,
        grid=(n // tr, nblk),
        in_specs=[pl.BlockSpec((tr, LANES), lambda i, j: (i, first + j)),
                  pl.BlockSpec((1, 1, LANES), lambda i, j: (j, 0, 0)),
                  pl.BlockSpec((2 * LANES, LANES), lambda i, j: (0, 0)),
                  pos_spec, pos_spec],
        out_specs=pl.BlockSpec((tr, LANES), lambda i, j: (i, j)),
        out_shape=jax.ShapeDtypeStruct((n, nblk * LANES), BF16),
        compiler_params=_params("parallel", "arbitrary"),
        name="c_prep",
    )(proj, gains, same_head, cos_t, sin_t)


VT_ROWS = 2 * HEAD_DIM


def _c_vprep_kernel(x_ref, o_ref):
    xt = x_ref[...].astype(F32).T
    ones = jnp.ones((VT_ROWS - HEAD_DIM, xt.shape[1]), o_ref.dtype)
    for half in (0, 1):
        o_ref[half * VT_ROWS:half * VT_ROWS + HEAD_DIM, :] = (
            xt[half * HEAD_DIM:(half + 1) * HEAD_DIM].astype(o_ref.dtype))
        o_ref[half * VT_ROWS + HEAD_DIM:(half + 1) * VT_ROWS, :] = ones


def _c_vprep(proj, *, bsz, s_len, tk):
    nk = s_len // tk
    v_first = proj.shape[1] // LANES - 2
    return pl.pallas_call(
        _c_vprep_kernel,
        grid=(bsz, 2, nk),
        in_specs=[pl.BlockSpec((tk, LANES), lambda b, p, i: (b * nk + i, v_first + p))],
        out_specs=pl.BlockSpec((None, None, None, 2 * VT_ROWS, tk), lambda b, p, i: (b, p, i, 0, 0)),
        out_shape=jax.ShapeDtypeStruct((bsz, 2, nk, 2 * VT_ROWS, tk), BF16),
        compiler_params=_params("parallel", "parallel", "arbitrary"),
        name="c_vprep",
    )(proj)


C_GROUP = 256


def _attn_c_kernel(q_ref, k_ref, vt_ref, o_ref, qs_ref, acc_ref, s_ref, *, tq, tk, s_len):
    lo = _first_head_lanes()
    per_blk = tq // C_GROUP
    n_groups = 8 * per_blk
    for r in range(4):
        for j in range(per_blk):
            qb = q_ref[j * C_GROUP:(j + 1) * C_GROUP, r * LANES:(r + 1) * LANES]
            zero = jnp.zeros_like(qb)
            qs_ref[r * per_blk + j] = jnp.where(lo, qb, zero)
            qs_ref[(4 + r) * per_blk + j] = jnp.where(lo, zero, qb)
    acc_ref[...] = jnp.zeros(acc_ref.shape, F32)

    nk = s_len // tk

    def scores(t, slot, groups):
        kt = k_ref[pl.ds(pl.multiple_of(t * tk, tk), tk), :]
        for g in groups:
            s_ref[slot, g] = lax.dot_general(kt, qs_ref[g], (((1,), (1,)), ((), ())),
                                             preferred_element_type=F32)

    def step(t, slot, t_next, m):
        m_out = []
        for g in range(n_groups):
            scores(t_next, 1 - slot, (g,))
            st = s_ref[slot, g]
            m_new = jnp.maximum(m[g], jnp.max(st, axis=0, keepdims=True))
            alpha = jnp.exp2(m[g] - m_new)
            pt = jnp.exp2(st - m_new).astype(BF16)
            half = 0 if g < n_groups // 2 else 1
            vt = vt_ref[t, half * VT_ROWS:(half + 1) * VT_ROWS, :]
            acc_ref[g] = alpha * acc_ref[g] + jnp.dot(vt, pt, preferred_element_type=F32)
            m_out.append(m_new)
        return tuple(m_out)

    scores(0, 0, range(n_groups))

    def two_steps(u, m):
        t0 = 2 * u
        m = step(t0, 0, t0 + 1, m)
        return step(t0 + 1, 1, jnp.minimum(t0 + 2, nk - 1), m)

    m0 = tuple(jnp.full((1, C_GROUP), -jnp.inf, F32) for _ in range(n_groups))
    lax.fori_loop(0, nk // 2, two_steps, m0)
    for r in range(4):
        for j in range(per_blk):
            a_lo = acc_ref[r * per_blk + j]
            a_hi = acc_ref[(4 + r) * per_blk + j]
            o_lo = a_lo[:HEAD_DIM] * (1.0 / a_lo[HEAD_DIM:HEAD_DIM + 1])
            o_hi = a_hi[:HEAD_DIM] * (1.0 / a_hi[HEAD_DIM:HEAD_DIM + 1])
            o_ref[j * C_GROUP:(j + 1) * C_GROUP, r * LANES:(r + 1) * LANES] = (
                jnp.concatenate([o_lo, o_hi], axis=0).T)


def _attn_c(qk, vt, *, bsz, s_len, tq, tk):
    n = qk.shape[0]
    nq = s_len // tq
    nk = s_len // tk
    assert tq % C_GROUP == 0 and nk % 2 == 0
    n_groups = 8 * tq // C_GROUP
    kern = functools.partial(_attn_c_kernel, tq=tq, tk=tk, s_len=s_len)
    return pl.pallas_call(
        kern,
        grid=(bsz, 2, nq),
        in_specs=[pl.BlockSpec((tq, 4 * LANES), lambda b, p, i: (b * nq + i, p)),
                  pl.BlockSpec((s_len, LANES), lambda b, p, i: (b, 8 + p)),
                  pl.BlockSpec((None, None, nk, 2 * VT_ROWS, tk), lambda b, p, i: (b, p, 0, 0, 0))],
        out_specs=pl.BlockSpec((tq, 4 * LANES), lambda b, p, i: (b * nq + i, p)),
        out_shape=jax.ShapeDtypeStruct((n, 8 * LANES), F32),
        scratch_shapes=[pltpu.VMEM((n_groups, C_GROUP, LANES), BF16),
                        pltpu.VMEM((n_groups, VT_ROWS, C_GROUP), F32),
                        pltpu.VMEM((2, n_groups, tk, C_GROUP), F32)],
        compiler_params=_params("parallel", "parallel", "arbitrary"),
        name="attn_c",
    )(qk, qk, vt)


def _mix_out_kernel(x_ref, oa_ref, ob_ref, oc_ref, ga_ref, gb_ref, gc_ref, w_ref, o_ref):
    parts = []
    for o_r, g_r in ((oa_ref, ga_ref), (ob_ref, gb_ref), (oc_ref, gc_ref)):
        o = o_r[...]
        parts.append((o * _rms_scale(o) * g_r[...]).astype(BF16))
    mix = jnp.concatenate(parts, axis=-1)
    o_ref[...] = x_ref[...] + jnp.dot(mix, w_ref[...], preferred_element_type=F32)


def _mix_out(x, oa, ob, oc, ga, gb, gc, w, layer, *, tm):
    n, d = x.shape
    row = lambda width: pl.BlockSpec((tm, width), lambda i: (i, 0))
    const = lambda shape: pl.BlockSpec(shape, lambda i: (0, 0))
    return pl.pallas_call(
        _mix_out_kernel,
        grid=(n // tm,),
        in_specs=[row(d), row(oa.shape[1]), row(ob.shape[1]), row(oc.shape[1]),
                  const((1, oa.shape[1])), const((1, ob.shape[1])), const((1, oc.shape[1])),
                  pl.BlockSpec((None,) + w.shape[1:], lambda i: (layer, 0, 0))],
        out_specs=row(d),
        out_shape=jax.ShapeDtypeStruct((n, d), F32),
        compiler_params=_params("parallel"),
        name="mix_out",
    )(x, oa, ob, oc, ga.reshape(1, -1), gb.reshape(1, -1), gc.reshape(1, -1), w)


def _mlp_kernel(x_ref, g_ref, wu_ref, wd_ref, gf_ref, o_ref, h_ref, *, final_norm):
    f = pl.program_id(1)

    @pl.when(f == 0)
    def _():
        x = x_ref[...]
        h_ref[...] = (x * _rms_scale(x) * g_ref[...]).astype(BF16)
        o_ref[...] = x

    u = jnp.maximum(jnp.dot(h_ref[...], wu_ref[...], preferred_element_type=F32), 0.0)
    o_ref[...] += jnp.dot((u * u).astype(BF16), wd_ref[...], preferred_element_type=F32)

    if final_norm:
        @pl.when(f == pl.num_programs(1) - 1)
        def _():
            y = o_ref[...]
            o_ref[...] = y * _rms_scale(y) * gf_ref[...]


def _mlp(x, gain, w_up, w_down, layer, final_gain, *, tm, tf, final_norm):
    n, d = x.shape
    ff = w_up.shape[2]
    kern = functools.partial(_mlp_kernel, final_norm=final_norm)
    return pl.pallas_call(
        kern,
        grid=(n // tm, ff // tf),
        in_specs=[pl.BlockSpec((tm, d), lambda i, f: (i, 0)),
                  pl.BlockSpec((1, d), lambda i, f: (0, 0)),
                  pl.BlockSpec((None, d, tf), lambda i, f: (layer, 0, f)),
                  pl.BlockSpec((None, tf, d), lambda i, f: (layer, f, 0)),
                  pl.BlockSpec((1, d), lambda i, f: (0, 0))],
        out_specs=pl.BlockSpec((tm, d), lambda i, f: (i, 0)),
        out_shape=jax.ShapeDtypeStruct((n, d), F32),
        scratch_shapes=[pltpu.VMEM((tm, d), BF16)],
        compiler_params=_params("parallel", "arbitrary"),
        name="mlp",
    )(x, gain.reshape(1, d), w_up, w_down, final_gain.reshape(1, d))


def _t5_bucket_np(rel):
    nb = T5_BUCKETS // 2
    max_exact = nb // 2
    base = np.where(rel > 0, nb, 0)
    n = np.abs(rel)
    nf = np.maximum(n, 1).astype(np.float32)
    large = max_exact + (np.log(nf / np.float32(max_exact)) / np.float32(math.log(T5_MAX_DIST / max_exact))
                         * np.float32(nb - max_exact)).astype(np.int32)
    large = np.minimum(large, nb - 1)
    return base + np.where(n < max_exact, n, large)


def _pair_heads(a, axis):
    shape = a.shape
    halves = shape[axis] // (8 * HEAD_DIM)
    a = a.reshape(shape[:axis] + (halves, 2, 4, HEAD_DIM) + shape[axis + 1:])
    return jnp.swapaxes(a, axis + 1, axis + 2).reshape(shape)


def _split_rotary(a, axis):
    shape = a.shape
    a = a.reshape(shape[:axis] + (shape[axis] // LANES, 2, 2, 2, HEAD_DIM // 4) + shape[axis + 1:])
    perm = (tuple(range(axis + 1)) + (axis + 3, axis + 1, axis + 2, axis + 4)
            + tuple(range(axis + 5, a.ndim)))
    return jnp.transpose(a, perm).reshape(shape)


def _first_head_lanes():
    lane = lax.broadcasted_iota(jnp.int32, (1, LANES), 1)
    return (lane // (HEAD_DIM // 2)) % 2 == 0


def _rope_tables(s_len):
    axis_dim = HEAD_DIM // 2
    t = jnp.arange(s_len)
    row = (t // GRID_W).astype(F32)
    col = (t % GRID_W).astype(F32)
    freqs = ROPE_THETA ** (-jnp.arange(0, axis_dim, 2, dtype=F32) / axis_dim)
    ang = jnp.concatenate([row[:, None] * freqs[None, :], col[:, None] * freqs[None, :]], axis=-1)
    ang = jnp.tile(ang, (1, 4))
    sign = np.where(np.arange(LANES) < LANES // 2, -1.0, 1.0).astype(np.float32)
    return jnp.cos(ang), jnp.sin(ang) * sign


def kernel(x, norm_mix, w_in, a_sink, t5_table, b_rpb, c_q_gain, c_k_gain, out_gain_a, out_gain_b,
           out_gain_c, w_o, norm_mlp, w_up, w_down, norm_final):
    bsz, s_len, d_model = x.shape
    depth = w_in.shape[0]
    n = bsz * s_len
    scale = HEAD_DIM ** -0.5

    a_w = out_gain_a.shape[1]
    b_w = out_gain_b.shape[1]
    c_w = out_gain_c.shape[1]
    a_kv_w = a_w // 4
    c_kv_w = c_w // 4
    off_qb = a_w + 2 * a_kv_w
    off_qc = off_qb + 3 * b_w
    in_width = off_qc + c_w + 2 * c_kv_w
    assert in_width == w_in.shape[2] and a_w == 8 * HEAD_DIM and c_w == 16 * HEAD_DIM

    off_vc = off_qc + c_w + c_kv_w
    w_in_p = jnp.concatenate(
        [_pair_heads(w_in[..., :a_w], 2) * scale, w_in[..., a_w:off_qb],
         w_in[..., off_qb:off_qb + b_w] * scale, w_in[..., off_qb + b_w:off_qc],
         _split_rotary(_pair_heads(w_in[..., off_qc:off_qc + c_w], 2), 2),
         _split_rotary(w_in[..., off_qc + c_w:off_vc], 2), w_in[..., off_vc:]], axis=2).astype(BF16)
    w_o_p = jnp.concatenate(
        [_pair_heads(w_o[:, :a_w], 1), w_o[:, a_w:a_w + b_w], _pair_heads(w_o[:, a_w + b_w:], 1)],
        axis=1).astype(BF16)
    w_up_b = w_up.astype(BF16)
    w_down_b = w_down.astype(BF16)

    qi = np.arange(A_BLOCK)[:, None]
    kj = np.arange(3 * A_BLOCK)[None, :]
    bucket = _t5_bucket_np(kj - A_BLOCK - qi)
    bias_a = t5_table[bucket].transpose(2, 0, 1).astype(F32)
    bias_a = bias_a.reshape(2, 4, A_BLOCK, 3 * A_BLOCK)

    cq = np.arange(GRID_W)[:, None]
    ck = np.arange(GRID_W)[None, :]
    dc = np.clip(ck - cq + NA_COLS - 1, 0, 2 * NA_COLS - 2)
    t_blocks = b_rpb[:, :, :, dc].astype(F32)
    t_pair = jnp.concatenate([t_blocks[:, :, :-1], t_blocks[:, :, 1:]], axis=-1)

    cos_t, sin_t = _rope_tables(s_len)
    two = lambda g: _split_rotary(jnp.concatenate([g, g], axis=-1), 0)
    n_qblk, n_kblk = c_w // LANES, c_kv_w // LANES
    head_of_lane = (np.arange(LANES) // (HEAD_DIM // 2)) % 2
    same_head = jnp.asarray(np.tile(head_of_lane[:, None] == head_of_lane[None, :], (2, 1)), BF16)

    xf = x.reshape(n, d_model)
    for l in range(depth):
        proj = _inproj(xf, norm_mix[l], w_in_p, l, tm=min(1024, n), tn=in_width // 3)
        oa = _attn_a(proj, a_sink[l].astype(F32), bias_a, bsz=bsz, s_len=s_len, tq=min(512, s_len))
        ob = _attn_b(proj, t_pair[l], bsz=bsz, s_len=s_len, rblk=8)
        gains = jnp.concatenate([jnp.tile(two(c_q_gain[l] * (scale * LOG2E))[None], (n_qblk, 1)),
                                 jnp.tile(two(c_k_gain[l])[None], (n_kblk, 1))], axis=0)
        qk = _c_prep(proj, gains.reshape(n_qblk + n_kblk, 1, LANES).astype(F32), same_head, cos_t, sin_t,
                     s_len=s_len, tr=min(2048, s_len), first=off_qc // LANES)
        tk_c = min(512, s_len // 2)
        vt = _c_vprep(proj, bsz=bsz, s_len=s_len, tk=tk_c)
        oc = _attn_c(qk, vt, bsz=bsz, s_len=s_len, tq=min(256, s_len), tk=tk_c)
        xf = _mix_out(xf, oa, ob, oc, _pair_heads(out_gain_a[l], 0), out_gain_b[l],
                      _pair_heads(out_gain_c[l], 0), w_o_p, l, tm=min(512, n))
        xf = _mlp(xf, norm_mlp[l], w_up_b, w_down_b, l, norm_final,
                  tm=min(512, n), tf=512, final_norm=(l == depth - 1))
    return xf.reshape(bsz, s_len, d_model)
```

```python
import functools
import math

import numpy as np
import jax
import jax.numpy as jnp
from jax import lax
from jax.experimental import pallas as pl
from jax.experimental.pallas import tpu as pltpu

HEAD_DIM = 64
LANES = 128
WINDOW = 128
A_BLOCK = 128
T5_BUCKETS = 32
T5_MAX_DIST = 128
GRID_W = 64
NA_ROWS = 8
NA_COLS = 16
ROPE_THETA = 10000.0
EPS = 1e-6
MASK_VALUE = -1e30
LOG2E = math.log2(math.e)
VMEM_LIMIT = 56 * 1024 * 1024

F32 = jnp.float32
BF16 = jnp.bfloat16


def _params(*sem):
    return pltpu.CompilerParams(dimension_semantics=sem, vmem_limit_bytes=VMEM_LIMIT)


def _rms_scale(x):
    return lax.rsqrt(jnp.mean(x * x, axis=-1, keepdims=True) + EPS)


def _lo_lanes():
    return lax.broadcasted_iota(jnp.int32, (1, LANES), 1) < HEAD_DIM


def _inproj_kernel(x_ref, g_ref, w_ref, o_ref, h_ref):
    @pl.when(pl.program_id(1) == 0)
    def _():
        x = x_ref[...]
        h_ref[...] = (x * _rms_scale(x) * g_ref[...]).astype(BF16)

    o_ref[...] = jnp.dot(h_ref[...], w_ref[...], preferred_element_type=F32).astype(o_ref.dtype)


def _inproj(x, gain, w, layer, *, tm, tn):
    n, d = x.shape
    e = w.shape[2]
    return pl.pallas_call(
        _inproj_kernel,
        grid=(n // tm, e // tn),
        in_specs=[pl.BlockSpec((tm, d), lambda i, j: (i, 0)),
                  pl.BlockSpec((1, d), lambda i, j: (0, 0)),
                  pl.BlockSpec((None, d, tn), lambda i, j: (layer, 0, j))],
        out_specs=pl.BlockSpec((tm, tn), lambda i, j: (i, j)),
        out_shape=jax.ShapeDtypeStruct((n, e), BF16),
        scratch_shapes=[pltpu.VMEM((tm, d), BF16)],
        compiler_params=_params("parallel", "arbitrary"),
        name="inproj",
    )(x, gain.reshape(1, d), w)


def _vt_prep_kernel(x_ref, o_ref):
    for t in range(o_ref.shape[0]):
        o_ref[t] = x_ref[t * LANES:(t + 1) * LANES, :].astype(F32).T.astype(o_ref.dtype)


def _vt_prep(proj, first, count, *, bsz, s_len, tr):
    nt = s_len // tr
    return pl.pallas_call(
        _vt_prep_kernel,
        grid=(bsz, count, nt),
        in_specs=[pl.BlockSpec((tr, LANES), lambda b, c, i: (b * nt + i, first + c))],
        out_specs=pl.BlockSpec((None, None, tr // LANES, LANES, LANES), lambda b, c, i: (b, c, i, 0, 0)),
        out_shape=jax.ShapeDtypeStruct((bsz, count, s_len // LANES, LANES, LANES), BF16),
        compiler_params=_params("parallel", "parallel", "arbitrary"),
        name="vt_prep",
    )(proj)


def _attn_a_kernel(q_ref, k_ref, vt_ref, bias_ref, sink_ref, o_ref, *, tq, s_len):
    i = pl.program_id(1)
    lo = _lo_lanes()
    first_rows = lax.broadcasted_iota(jnp.int32, (LANES, 1), 0) < HEAD_DIM
    kj = lax.broadcasted_iota(jnp.int32, (3 * A_BLOCK, A_BLOCK), 0)
    qi = lax.broadcasted_iota(jnp.int32, (3 * A_BLOCK, A_BLOCK), 1)
    in_window = jnp.abs(kj - A_BLOCK - qi) <= WINDOW
    n_blocks = s_len // A_BLOCK

    def key_blocks(j):
        blk = i * (tq // A_BLOCK) + j
        return [jnp.clip(blk + o, 0, n_blocks - 1) for o in (-1, 0, 1)]

    def scores(j, half):
        keep = lo if half == 0 else jnp.logical_not(lo)
        rows = slice(j * A_BLOCK, (j + 1) * A_BLOCK)
        qs = jnp.concatenate(
            [jnp.where(keep, q_ref[rows, r * LANES:(r + 1) * LANES], jnp.zeros((), BF16)) for r in range(4)],
            axis=0)
        k3 = jnp.concatenate([k_ref[pl.ds(pl.multiple_of(b * A_BLOCK, A_BLOCK), A_BLOCK), :]
                              for b in key_blocks(j)], axis=0)
        return lax.dot_general(k3, qs, (((1,), (1,)), ((), ())), preferred_element_type=F32)

    def finish(j, half, s):
        key_pos = (i * (tq // A_BLOCK) + j - 1) * A_BLOCK + kj
        valid = in_window & (key_pos >= 0) & (key_pos < s_len)
        heads = [slice(r * A_BLOCK, (r + 1) * A_BLOCK) for r in range(4)]
        s = jnp.concatenate([jnp.where(valid, s[:, c] + bias_ref[half, :, c], MASK_VALUE) for c in heads], axis=1)
        sink = sink_ref[half]
        m = jnp.maximum(jnp.max(s, axis=0, keepdims=True), sink)
        p = jnp.exp2(s - m).astype(BF16)
        v3t = jnp.concatenate([vt_ref[b] for b in key_blocks(j)], axis=1)
        ones = jnp.ones((), BF16)
        lhs = jnp.where(first_rows, v3t, ones) if half == 0 else jnp.where(first_rows, ones, v3t)
        acc = jnp.dot(lhs, p, preferred_element_type=F32)
        num, den = (acc[:HEAD_DIM], acc[HEAD_DIM:HEAD_DIM + 1]) if half == 0 else (acc[HEAD_DIM:], acc[:1])
        return num * (1.0 / (den + jnp.exp2(sink - m)))

    units = [(j, half) for j in range(tq // A_BLOCK) for half in (0, 1)]
    s_next = scores(*units[0])
    lo_half = None
    for n, (j, half) in enumerate(units):
        s_cur = s_next
        if n + 1 < len(units):
            s_next = scores(*units[n + 1])
        out = finish(j, half, s_cur)
        if half == 0:
            lo_half = out
        else:
            rows = slice(j * A_BLOCK, (j + 1) * A_BLOCK)
            for r in range(4):
                cols = slice(r * A_BLOCK, (r + 1) * A_BLOCK)
                o_ref[rows, r * LANES:(r + 1) * LANES] = jnp.concatenate([lo_half[:, cols], out[:, cols]], axis=0).T


def _attn_a(proj, vt, bias_t, sink_t, *, bsz, s_len, tq, k_block):
    n = proj.shape[0]
    nq = s_len // tq
    kern = functools.partial(_attn_a_kernel, tq=tq, s_len=s_len)
    return pl.pallas_call(
        kern,
        grid=(bsz, nq),
        in_specs=[pl.BlockSpec((tq, 4 * LANES), lambda b, i: (b * nq + i, 0)),
                  pl.BlockSpec((s_len, LANES), lambda b, i: (b, k_block)),
                  pl.BlockSpec((None, None, s_len // LANES, LANES, LANES), lambda b, i: (b, 0, 0, 0, 0)),
                  pl.BlockSpec(bias_t.shape, lambda b, i: (0, 0, 0)),
                  pl.BlockSpec(sink_t.shape, lambda b, i: (0, 0, 0))],
        out_specs=pl.BlockSpec((tq, 4 * LANES), lambda b, i: (b * nq + i, 0)),
        out_shape=jax.ShapeDtypeStruct((n, 4 * LANES), F32),
        compiler_params=_params("parallel", "arbitrary"),
        name="attn_a",
    )(proj, proj, vt, bias_t, sink_t)


def _attn_b_kernel(q_ref, k_ref, v_ref, t_ref, o_ref, *, rblk, rows):
    r0 = pl.program_id(2) * rblk
    lo = _lo_lanes()
    nkeys = NA_ROWS * GRID_W
    cq = lax.broadcasted_iota(jnp.int32, (GRID_W, nkeys), 0)
    ck = lax.broadcasted_iota(jnp.int32, (GRID_W, nkeys), 1) % GRID_W
    cs = jnp.clip(cq - NA_COLS // 2, 0, GRID_W - NA_COLS)
    col_valid = (ck >= cs) & (ck < cs + NA_COLS)

    def first_key_row(a):
        return jnp.clip(r0 + a - NA_ROWS // 2, 0, rows - NA_ROWS)

    def window(ref, a):
        return ref[pl.ds(pl.multiple_of(first_key_row(a) * GRID_W, GRID_W), nkeys), :]

    def scores(a):
        qrow = q_ref[a * GRID_W:(a + 1) * GRID_W, :]
        zero = jnp.zeros_like(qrow)
        qs = jnp.concatenate([jnp.where(lo, qrow, zero), jnp.where(lo, zero, qrow)], axis=0)
        return lax.dot_general(qs, window(k_ref, a), (((1,), (1,)), ((), ())), preferred_element_type=F32)

    def finish(a, s):
        off = first_key_row(a) - (r0 + a) + (NA_ROWS - 1)
        bias = jnp.concatenate(
            [jnp.concatenate([t_ref[half, off + 2 * j] for j in range(NA_ROWS // 2)], axis=1)
             for half in (0, 1)], axis=0)
        s = s.reshape(2, GRID_W, nkeys) + bias.reshape(2, GRID_W, nkeys)
        s = jnp.where(col_valid[None], s, MASK_VALUE)
        m = jnp.max(s, axis=-1, keepdims=True)
        e = jnp.exp(s - m)
        den = jnp.sum(e, axis=-1, keepdims=True)
        pv = jnp.dot(e.reshape(2 * GRID_W, nkeys).astype(BF16), window(v_ref, a),
                     preferred_element_type=F32)
        pv = pv.reshape(2, GRID_W, LANES) * (1.0 / den)
        o_ref[a * GRID_W:(a + 1) * GRID_W, :] = jnp.where(lo, pv[0], pv[1])

    s_next = scores(0)
    for a in range(rblk):
        s_cur = s_next
        if a + 1 < rblk:
            s_next = scores(a + 1)
        finish(a, s_cur)


def _attn_b(proj, t_pair, *, bsz, s_len, rblk):
    n = proj.shape[0]
    rows = s_len // GRID_W
    nrb = rows // rblk
    kern = functools.partial(_attn_b_kernel, rblk=rblk, rows=rows)
    tq = rblk * GRID_W
    return pl.pallas_call(
        kern,
        grid=(bsz, 4, nrb),
        in_specs=[pl.BlockSpec((tq, LANES), lambda b, h, i: (b * nrb + i, 6 + h)),
                  pl.BlockSpec((s_len, LANES), lambda b, h, i: (b, 10 + h)),
                  pl.BlockSpec((s_len, LANES), lambda b, h, i: (b, 14 + h)),
                  pl.BlockSpec((2, 2 * NA_ROWS - 2, GRID_W, LANES), lambda b, h, i: (h, 0, 0, 0))],
        out_specs=pl.BlockSpec((tq, LANES), lambda b, h, i: (b * nrb + i, h)),
        out_shape=jax.ShapeDtypeStruct((n, 4 * LANES), F32),
        compiler_params=_params("parallel", "parallel", "arbitrary"),
        name="attn_b",
    )(proj, proj, proj, t_pair)


def _c_prep_kernel(x_ref, g_ref, e_ref, cos_ref, sin_ref, o_ref):
    x = x_ref[...].astype(F32)
    xx = x * x
    xx_hi = xx.astype(BF16)
    xx_lo = (xx - xx_hi.astype(F32)).astype(BF16)
    ssq = jnp.dot(jnp.concatenate([xx_hi, xx_lo], axis=1), e_ref[...], preferred_element_type=F32)
    y = x * lax.rsqrt(ssq * (1.0 / HEAD_DIM) + EPS) * g_ref[0]
    o_ref[...] = (y * cos_ref[...] + pltpu.roll(y, LANES // 2, 1) * sin_ref[...]).astype(o_ref.dtype)


def _c_prep(proj, gains, same_head, cos_t, sin_t, *, s_len, tr, first):
    n = proj.shape[0]
    nblk = gains.shape[0]
    npos = s_len // tr
    pos_spec = pl.BlockSpec((tr, LANES), lambda i, j: (i % npos, 0))
    return pl.pallas_call(
        _c_prep_kernel,
        grid=(n // tr, nblk),
        in_specs=[pl.BlockSpec((tr, LANES), lambda i, j: (i, first + j)),
                  pl.BlockSpec((1, 1, LANES), lambda i, j: (j, 0, 0)),
                  pl.BlockSpec((2 * LANES, LANES), lambda i, j: (0, 0)),
                  pos_spec, pos_spec],
        out_specs=pl.BlockSpec((tr, LANES), lambda i, j: (i, j)),
        out_shape=jax.ShapeDtypeStruct((n, nblk * LANES), BF16),
        compiler_params=_params("parallel", "arbitrary"),
        name="c_prep",
    )(proj, gains, same_head, cos_t, sin_t)


VT_ROWS = 2 * HEAD_DIM


def _c_vprep_kernel(x_ref, o_ref):
    xt = x_ref[...].astype(F32).T
    ones = jnp.ones((VT_ROWS - HEAD_DIM, xt.shape[1]), o_ref.dtype)
    for half in (0, 1):
        o_ref[half * VT_ROWS:half * VT_ROWS + HEAD_DIM, :] = (
            xt[half * HEAD_DIM:(half + 1) * HEAD_DIM].astype(o_ref.dtype))
        o_ref[half * VT_ROWS + HEAD_DIM:(half + 1) * VT_ROWS, :] = ones


def _c_vprep(proj, *, bsz, s_len, tk):
    nk = s_len // tk
    v_first = proj.shape[1] // LANES - 2
    return pl.pallas_call(
        _c_vprep_kernel,
        grid=(bsz, 2, nk),
        in_specs=[pl.BlockSpec((tk, LANES), lambda b, p, i: (b * nk + i, v_first + p))],
        out_specs=pl.BlockSpec((None, None, None, 2 * VT_ROWS, tk), lambda b, p, i: (b, p, i, 0, 0)),
        out_shape=jax.ShapeDtypeStruct((bsz, 2, nk, 2 * VT_ROWS, tk), BF16),
        compiler_params=_params("parallel", "parallel", "arbitrary"),
        name="c_vprep",
    )(proj)


C_GROUP = 256


def _attn_c_kernel(q_ref, k_ref, vt_ref, o_ref, qs_ref, acc_ref, s_ref, *, tq, tk, s_len):
    lo = _first_head_lanes()
    per_blk = tq // C_GROUP
    n_groups = 8 * per_blk
    for r in range(4):
        for j in range(per_blk):
            qb = q_ref[j * C_GROUP:(j + 1) * C_GROUP, r * LANES:(r + 1) * LANES]
            zero = jnp.zeros_like(qb)
            qs_ref[r * per_blk + j] = jnp.where(lo, qb, zero)
            qs_ref[(4 + r) * per_blk + j] = jnp.where(lo, zero, qb)
    acc_ref[...] = jnp.zeros(acc_ref.shape, F32)

    nk = s_len // tk

    def scores(t, slot, groups):
        kt = k_ref[pl.ds(pl.multiple_of(t * tk, tk), tk), :]
        for g in groups:
            s_ref[slot, g] = lax.dot_general(kt, qs_ref[g], (((1,), (1,)), ((), ())),
                                             preferred_element_type=F32)

    def step(t, slot, t_next, m):
        m_out = []
        for g in range(n_groups):
            scores(t_next, 1 - slot, (g,))
            st = s_ref[slot, g]
            m_new = jnp.maximum(m[g], jnp.max(st, axis=0, keepdims=True))
            alpha = jnp.exp2(m[g] - m_new)
            pt = jnp.exp2(st - m_new).astype(BF16)
            half = 0 if g < n_groups // 2 else 1
            vt = vt_ref[t, half * VT_ROWS:(half + 1) * VT_ROWS, :]
            acc_ref[g] = alpha * acc_ref[g] + jnp.dot(vt, pt, preferred_element_type=F32)
            m_out.append(m_new)
        return tuple(m_out)

    scores(0, 0, range(n_groups))

    def two_steps(u, m):
        t0 = 2 * u
        m = step(t0, 0, t0 + 1, m)
        return step(t0 + 1, 1, jnp.minimum(t0 + 2, nk - 1), m)

    m0 = tuple(jnp.full((1, C_GROUP), -jnp.inf, F32) for _ in range(n_groups))
    lax.fori_loop(0, nk // 2, two_steps, m0)
    for r in range(4):
        for j in range(per_blk):
            a_lo = acc_ref[r * per_blk + j]
            a_hi = acc_ref[(4 + r) * per_blk + j]
            o_lo = a_lo[:HEAD_DIM] * (1.0 / a_lo[HEAD_DIM:HEAD_DIM + 1])
            o_hi = a_hi[:HEAD_DIM] * (1.0 / a_hi[HEAD_DIM:HEAD_DIM + 1])
            o_ref[j * C_GROUP:(j + 1) * C_GROUP, r * LANES:(r + 1) * LANES] = (
                jnp.concatenate([o_lo, o_hi], axis=0).T)


def _attn_c(qk, vt, *, bsz, s_len, tq, tk):
    n = qk.shape[0]
    nq = s_len // tq
    nk = s_len // tk
    assert tq % C_GROUP == 0 and nk % 2 == 0
    n_groups = 8 * tq // C_GROUP
    kern = functools.partial(_attn_c_kernel, tq=tq, tk=tk, s_len=s_len)
    return pl.pallas_call(
        kern,
        grid=(bsz, 2, nq),
        in_specs=[pl.BlockSpec((tq, 4 * LANES), lambda b, p, i: (b * nq + i, p)),
                  pl.BlockSpec((s_len, LANES), lambda b, p, i: (b, 8 + p)),
                  pl.BlockSpec((None, None, nk, 2 * VT_ROWS, tk), lambda b, p, i: (b, p, 0, 0, 0))],
        out_specs=pl.BlockSpec((tq, 4 * LANES), lambda b, p, i: (b * nq + i, p)),
        out_shape=jax.ShapeDtypeStruct((n, 8 * LANES), F32),
        scratch_shapes=[pltpu.VMEM((n_groups, C_GROUP, LANES), BF16),
                        pltpu.VMEM((n_groups, VT_ROWS, C_GROUP), F32),
                        pltpu.VMEM((2, n_groups, tk, C_GROUP), F32)],
        compiler_params=_params("parallel", "parallel", "arbitrary"),
        name="attn_c",
    )(qk, qk, vt)


def _mix_out_kernel(x_ref, oa_ref, ob_ref, oc_ref, ga_ref, gb_ref, gc_ref, w_ref, o_ref):
    parts = []
    for o_r, g_r in ((oa_ref, ga_ref), (ob_ref, gb_ref), (oc_ref, gc_ref)):
        o = o_r[...]
        parts.append((o * _rms_scale(o) * g_r[...]).astype(BF16))
    mix = jnp.concatenate(parts, axis=-1)
    o_ref[...] = x_ref[...] + jnp.dot(mix, w_ref[...], preferred_element_type=F32)


def _mix_out(x, oa, ob, oc, ga, gb, gc, w, layer, *, tm):
    n, d = x.shape
    row = lambda width: pl.BlockSpec((tm, width), lambda i: (i, 0))
    const = lambda shape: pl.BlockSpec(shape, lambda i: (0, 0))
    return pl.pallas_call(
        _mix_out_kernel,
        grid=(n // tm,),
        in_specs=[row(d), row(oa.shape[1]), row(ob.shape[1]), row(oc.shape[1]),
                  const((1, oa.shape[1])), const((1, ob.shape[1])), const((1, oc.shape[1])),
                  pl.BlockSpec((None,) + w.shape[1:], lambda i: (layer, 0, 0))],
        out_specs=row(d),
        out_shape=jax.ShapeDtypeStruct((n, d), F32),
        compiler_params=_params("parallel"),
        name="mix_out",
    )(x, oa, ob, oc, ga.reshape(1, -1), gb.reshape(1, -1), gc.reshape(1, -1), w)


def _mlp_kernel(x_ref, g_ref, wu_ref, wd_ref, gf_ref, o_ref, h_ref, *, final_norm):
    f = pl.program_id(1)

    @pl.when(f == 0)
    def _():
        x = x_ref[...]
        h_ref[...] = (x * _rms_scale(x) * g_ref[...]).astype(BF16)
        o_ref[...] = x

    u = jnp.maximum(jnp.dot(h_ref[...], wu_ref[...], preferred_element_type=F32), 0.0)
    o_ref[...] += jnp.dot((u * u).astype(BF16), wd_ref[...], preferred_element_type=F32)

    if final_norm:
        @pl.when(f == pl.num_programs(1) - 1)
        def _():
            y = o_ref[...]
            o_ref[...] = y * _rms_scale(y) * gf_ref[...]


def _mlp(x, gain, w_up, w_down, layer, final_gain, *, tm, tf, final_norm):
    n, d = x.shape
    ff = w_up.shape[2]
    kern = functools.partial(_mlp_kernel, final_norm=final_norm)
    return pl.pallas_call(
        kern,
        grid=(n // tm, ff // tf),
        in_specs=[pl.BlockSpec((tm, d), lambda i, f: (i, 0)),
                  pl.BlockSpec((1, d), lambda i, f: (0, 0)),
                  pl.BlockSpec((None, d, tf), lambda i, f: (layer, 0, f)),
                  pl.BlockSpec((None, tf, d), lambda i, f: (layer, f, 0)),
                  pl.BlockSpec((1, d), lambda i, f: (0, 0))],
        out_specs=pl.BlockSpec((tm, d), lambda i, f: (i, 0)),
        out_shape=jax.ShapeDtypeStruct((n, d), F32),
        scratch_shapes=[pltpu.VMEM((tm, d), BF16)],
        compiler_params=_params("parallel", "arbitrary"),
        name="mlp",
    )(x, gain.reshape(1, d), w_up, w_down, final_gain.reshape(1, d))


def _t5_bucket_np(rel):
    nb = T5_BUCKETS // 2
    max_exact = nb // 2
    base = np.where(rel > 0, nb, 0)
    n = np.abs(rel)
    nf = np.maximum(n, 1).astype(np.float32)
    large = max_exact + (np.log(nf / np.float32(max_exact)) / np.float32(math.log(T5_MAX_DIST / max_exact))
                         * np.float32(nb - max_exact)).astype(np.int32)
    large = np.minimum(large, nb - 1)
    return base + np.where(n < max_exact, n, large)


def _pair_heads(a, axis):
    shape = a.shape
    halves = shape[axis] // (8 * HEAD_DIM)
    a = a.reshape(shape[:axis] + (halves, 2, 4, HEAD_DIM) + shape[axis + 1:])
    return jnp.swapaxes(a, axis + 1, axis + 2).reshape(shape)


def _split_rotary(a, axis):
    shape = a.shape
    a = a.reshape(shape[:axis] + (shape[axis] // LANES, 2, 2, 2, HEAD_DIM // 4) + shape[axis + 1:])
    perm = (tuple(range(axis + 1)) + (axis + 3, axis + 1, axis + 2, axis + 4)
            + tuple(range(axis + 5, a.ndim)))
    return jnp.transpose(a, perm).reshape(shape)


def _first_head_lanes():
    lane = lax.broadcasted_iota(jnp.int32, (1, LANES), 1)
    return (lane // (HEAD_DIM // 2)) % 2 == 0


def _rope_tables(s_len):
    axis_dim = HEAD_DIM // 2
    t = jnp.arange(s_len)
    row = (t // GRID_W).astype(F32)
    col = (t % GRID_W).astype(F32)
    freqs = ROPE_THETA ** (-jnp.arange(0, axis_dim, 2, dtype=F32) / axis_dim)
    ang = jnp.concatenate([row[:, None] * freqs[None, :], col[:, None] * freqs[None, :]], axis=-1)
    ang = jnp.tile(ang, (1, 4))
    sign = np.where(np.arange(LANES) < LANES // 2, -1.0, 1.0).astype(np.float32)
    return jnp.cos(ang), jnp.sin(ang) * sign


def kernel(x, norm_mix, w_in, a_sink, t5_table, b_rpb, c_q_gain, c_k_gain, out_gain_a, out_gain_b,
           out_gain_c, w_o, norm_mlp, w_up, w_down, norm_final):
    bsz, s_len, d_model = x.shape
    depth = w_in.shape[0]
    n = bsz * s_len
    scale = HEAD_DIM ** -0.5

    a_w = out_gain_a.shape[1]
    b_w = out_gain_b.shape[1]
    c_w = out_gain_c.shape[1]
    a_kv_w = a_w // 4
    c_kv_w = c_w // 4
    off_qb = a_w + 2 * a_kv_w
    off_qc = off_qb + 3 * b_w
    in_width = off_qc + c_w + 2 * c_kv_w
    assert in_width == w_in.shape[2] and a_w == 8 * HEAD_DIM and c_w == 16 * HEAD_DIM

    off_vc = off_qc + c_w + c_kv_w
    w_in_p = jnp.concatenate(
        [_pair_heads(w_in[..., :a_w], 2) * (scale * LOG2E), w_in[..., a_w:off_qb],
         w_in[..., off_qb:off_qb + b_w] * scale, w_in[..., off_qb + b_w:off_qc],
         _split_rotary(_pair_heads(w_in[..., off_qc:off_qc + c_w], 2), 2),
         _split_rotary(w_in[..., off_qc + c_w:off_vc], 2), w_in[..., off_vc:]], axis=2).astype(BF16)
    w_o_p = jnp.concatenate(
        [_pair_heads(w_o[:, :a_w], 1), w_o[:, a_w:a_w + b_w], _pair_heads(w_o[:, a_w + b_w:], 1)],
        axis=1).astype(BF16)
    w_up_b = w_up.astype(BF16)
    w_down_b = w_down.astype(BF16)

    qi = np.arange(A_BLOCK)[:, None]
    kj = np.arange(3 * A_BLOCK)[None, :]
    bucket = _t5_bucket_np(kj - A_BLOCK - qi)
    bias_a = t5_table[bucket].astype(F32) * LOG2E
    bias_t = bias_a.reshape(A_BLOCK, 3 * A_BLOCK, 2, 4).transpose(2, 1, 3, 0)
    bias_t = bias_t.reshape(2, 3 * A_BLOCK, 4 * A_BLOCK)
    sink_t = jnp.repeat(a_sink.astype(F32) * LOG2E, A_BLOCK, axis=1).reshape(depth, 2, 1, 4 * A_BLOCK)

    cq = np.arange(GRID_W)[:, None]
    ck = np.arange(GRID_W)[None, :]
    dc = np.clip(ck - cq + NA_COLS - 1, 0, 2 * NA_COLS - 2)
    t_blocks = b_rpb[:, :, :, dc].astype(F32)
    t_pair = jnp.concatenate([t_blocks[:, :, :-1], t_blocks[:, :, 1:]], axis=-1)

    cos_t, sin_t = _rope_tables(s_len)
    two = lambda g: _split_rotary(jnp.concatenate([g, g], axis=-1), 0)
    n_qblk, n_kblk = c_w // LANES, c_kv_w // LANES
    head_of_lane = (np.arange(LANES) // (HEAD_DIM // 2)) % 2
    same_head = jnp.asarray(np.tile(head_of_lane[:, None] == head_of_lane[None, :], (2, 1)), BF16)

    xf = x.reshape(n, d_model)
    for l in range(depth):
        proj = _inproj(xf, norm_mix[l], w_in_p, l, tm=min(1024, n), tn=in_width // 3)
        vt_a = _vt_prep(proj, a_w // LANES + 1, 1, bsz=bsz, s_len=s_len, tr=min(1024, s_len))
        oa = _attn_a(proj, vt_a, bias_t, sink_t[l], bsz=bsz, s_len=s_len, tq=min(512, s_len),
                     k_block=a_w // LANES)
        ob = _attn_b(proj, t_pair[l], bsz=bsz, s_len=s_len, rblk=8)
        gains = jnp.concatenate([jnp.tile(two(c_q_gain[l] * (scale * LOG2E))[None], (n_qblk, 1)),
                                 jnp.tile(two(c_k_gain[l])[None], (n_kblk, 1))], axis=0)
        qk = _c_prep(proj, gains.reshape(n_qblk + n_kblk, 1, LANES).astype(F32), same_head, cos_t, sin_t,
                     s_len=s_len, tr=min(2048, s_len), first=off_qc // LANES)
        tk_c = min(512, s_len // 2)
        vt = _c_vprep(proj, bsz=bsz, s_len=s_len, tk=tk_c)
        oc = _attn_c(qk, vt, bsz=bsz, s_len=s_len, tq=min(512, s_len), tk=tk_c)
        xf = _mix_out(xf, oa, ob, oc, _pair_heads(out_gain_a[l], 0), out_gain_b[l],
                      _pair_heads(out_gain_c[l], 0), w_o_p, l, tm=min(512, n))
        xf = _mlp(xf, norm_mlp[l], w_up_b, w_down_b, l, norm_final,
                  tm=min(512, n), tf=512, final_norm=(l == depth - 1))
    return xf.reshape(bsz, s_len, d_model)
```

```python
import functools
import math

import numpy as np
import jax
import jax.numpy as jnp
from jax import lax
from jax.experimental import pallas as pl
from jax.experimental.pallas import tpu as pltpu

HEAD_DIM = 64
LANES = 128
WINDOW = 128
A_BLOCK = 128
T5_BUCKETS = 32
T5_MAX_DIST = 128
GRID_W = 64
NA_ROWS = 8
NA_COLS = 16
ROPE_THETA = 10000.0
EPS = 1e-6
MASK_VALUE = -1e30
LOG2E = math.log2(math.e)
VMEM_LIMIT = 56 * 1024 * 1024

F32 = jnp.float32
BF16 = jnp.bfloat16


def _params(*sem):
    return pltpu.CompilerParams(dimension_semantics=sem, vmem_limit_bytes=VMEM_LIMIT)


def _rms_scale(x):
    return lax.rsqrt(jnp.mean(x * x, axis=-1, keepdims=True) + EPS)


def _lo_lanes():
    return lax.broadcasted_iota(jnp.int32, (1, LANES), 1) < HEAD_DIM


def _inproj_kernel(x_ref, g_ref, w_ref, o_ref, h_ref):
    @pl.when(pl.program_id(1) == 0)
    def _():
        x = x_ref[...]
        h_ref[...] = (x * _rms_scale(x) * g_ref[...]).astype(BF16)

    o_ref[...] = jnp.dot(h_ref[...], w_ref[...], preferred_element_type=F32).astype(o_ref.dtype)


def _inproj(x, gain, w, layer, *, tm, tn):
    n, d = x.shape
    e = w.shape[2]
    return pl.pallas_call(
        _inproj_kernel,
        grid=(n // tm, e // tn),
        in_specs=[pl.BlockSpec((tm, d), lambda i, j: (i, 0)),
                  pl.BlockSpec((1, d), lambda i, j: (0, 0)),
                  pl.BlockSpec((None, d, tn), lambda i, j: (layer, 0, j))],
        out_specs=pl.BlockSpec((tm, tn), lambda i, j: (i, j)),
        out_shape=jax.ShapeDtypeStruct((n, e), BF16),
        scratch_shapes=[pltpu.VMEM((tm, d), BF16)],
        compiler_params=_params("parallel", "arbitrary"),
        name="inproj",
    )(x, gain.reshape(1, d), w)


def _vt_prep_kernel(x_ref, o_ref):
    for t in range(o_ref.shape[0]):
        o_ref[t] = x_ref[t * LANES:(t + 1) * LANES, :].astype(F32).T.astype(o_ref.dtype)


def _vt_prep(proj, first, count, *, bsz, s_len, tr):
    nt = s_len // tr
    return pl.pallas_call(
        _vt_prep_kernel,
        grid=(bsz, count, nt),
        in_specs=[pl.BlockSpec((tr, LANES), lambda b, c, i: (b * nt + i, first + c))],
        out_specs=pl.BlockSpec((None, None, tr // LANES, LANES, LANES), lambda b, c, i: (b, c, i, 0, 0)),
        out_shape=jax.ShapeDtypeStruct((bsz, count, s_len // LANES, LANES, LANES), BF16),
        compiler_params=_params("parallel", "parallel", "arbitrary"),
        name="vt_prep",
    )(proj)


def _attn_a_kernel(q_ref, k_ref, vt_ref, bias_ref, sink_ref, o_ref, *, tq, s_len):
    i = pl.program_id(1)
    lo = _lo_lanes()
    first_rows = lax.broadcasted_iota(jnp.int32, (LANES, 1), 0) < HEAD_DIM
    kj = lax.broadcasted_iota(jnp.int32, (3 * A_BLOCK, A_BLOCK), 0)
    qi = lax.broadcasted_iota(jnp.int32, (3 * A_BLOCK, A_BLOCK), 1)
    in_window = jnp.abs(kj - A_BLOCK - qi) <= WINDOW
    n_blocks = s_len // A_BLOCK

    def key_blocks(j):
        blk = i * (tq // A_BLOCK) + j
        return [jnp.clip(blk + o, 0, n_blocks - 1) for o in (-1, 0, 1)]

    def scores(j, half):
        keep = lo if half == 0 else jnp.logical_not(lo)
        rows = slice(j * A_BLOCK, (j + 1) * A_BLOCK)
        qs = jnp.concatenate(
            [jnp.where(keep, q_ref[rows, r * LANES:(r + 1) * LANES], jnp.zeros((), BF16)) for r in range(4)],
            axis=0)
        k3 = jnp.concatenate([k_ref[pl.ds(pl.multiple_of(b * A_BLOCK, A_BLOCK), A_BLOCK), :]
                              for b in key_blocks(j)], axis=0)
        return lax.dot_general(k3, qs, (((1,), (1,)), ((), ())), preferred_element_type=F32)

    def finish(j, half, s):
        key_pos = (i * (tq // A_BLOCK) + j - 1) * A_BLOCK + kj
        valid = in_window & (key_pos >= 0) & (key_pos < s_len)
        heads = [slice(r * A_BLOCK, (r + 1) * A_BLOCK) for r in range(4)]
        s = jnp.concatenate([jnp.where(valid, s[:, c] + bias_ref[half, :, c], MASK_VALUE) for c in heads], axis=1)
        sink = sink_ref[half]
        m = jnp.maximum(jnp.max(s, axis=0, keepdims=True), sink)
        p = jnp.exp2(s - m).astype(BF16)
        v3t = jnp.concatenate([vt_ref[b] for b in key_blocks(j)], axis=1)
        ones = jnp.ones((), BF16)
        lhs = jnp.where(first_rows, v3t, ones) if half == 0 else jnp.where(first_rows, ones, v3t)
        acc = jnp.dot(lhs, p, preferred_element_type=F32)
        num, den = (acc[:HEAD_DIM], acc[HEAD_DIM:HEAD_DIM + 1]) if half == 0 else (acc[HEAD_DIM:], acc[:1])
        return num * (1.0 / (den + jnp.exp2(sink - m)))

    units = [(j, half) for j in range(tq // A_BLOCK) for half in (0, 1)]
    ahead = 2
    pending = [scores(*u) for u in units[:ahead]]
    lo_half = None
    for n, (j, half) in enumerate(units):
        s_cur = pending.pop(0)
        if n + ahead < len(units):
            pending.append(scores(*units[n + ahead]))
        out = finish(j, half, s_cur)
        if half == 0:
            lo_half = out
        else:
            rows = slice(j * A_BLOCK, (j + 1) * A_BLOCK)
            for r in range(4):
                cols = slice(r * A_BLOCK, (r + 1) * A_BLOCK)
                o_ref[rows, r * LANES:(r + 1) * LANES] = jnp.concatenate([lo_half[:, cols], out[:, cols]], axis=0).T


def _attn_a(proj, vt, bias_t, sink_t, *, bsz, s_len, tq, k_block):
    n = proj.shape[0]
    nq = s_len // tq
    kern = functools.partial(_attn_a_kernel, tq=tq, s_len=s_len)
    return pl.pallas_call(
        kern,
        grid=(bsz, nq),
        in_specs=[pl.BlockSpec((tq, 4 * LANES), lambda b, i: (b * nq + i, 0)),
                  pl.BlockSpec((s_len, LANES), lambda b, i: (b, k_block)),
                  pl.BlockSpec((None, None, s_len // LANES, LANES, LANES), lambda b, i: (b, 0, 0, 0, 0)),
                  pl.BlockSpec(bias_t.shape, lambda b, i: (0, 0, 0)),
                  pl.BlockSpec(sink_t.shape, lambda b, i: (0, 0, 0))],
        out_specs=pl.BlockSpec((tq, 4 * LANES), lambda b, i: (b * nq + i, 0)),
        out_shape=jax.ShapeDtypeStruct((n, 4 * LANES), F32),
        compiler_params=_params("parallel", "arbitrary"),
        name="attn_a",
    )(proj, proj, vt, bias_t, sink_t)


def _attn_b_kernel(q_ref, k_ref, v_ref, t_ref, o_ref, *, rblk, rows):
    r0 = pl.program_id(2) * rblk
    lo = _lo_lanes()
    nkeys = NA_ROWS * GRID_W
    cq = lax.broadcasted_iota(jnp.int32, (GRID_W, nkeys), 0)
    ck = lax.broadcasted_iota(jnp.int32, (GRID_W, nkeys), 1) % GRID_W
    cs = jnp.clip(cq - NA_COLS // 2, 0, GRID_W - NA_COLS)
    col_valid = (ck >= cs) & (ck < cs + NA_COLS)

    def first_key_row(a):
        return jnp.clip(r0 + a - NA_ROWS // 2, 0, rows - NA_ROWS)

    def window(ref, a):
        return ref[pl.ds(pl.multiple_of(first_key_row(a) * GRID_W, GRID_W), nkeys), :]

    def scores(a):
        qrow = q_ref[a * GRID_W:(a + 1) * GRID_W, :]
        zero = jnp.zeros_like(qrow)
        qs = jnp.concatenate([jnp.where(lo, qrow, zero), jnp.where(lo, zero, qrow)], axis=0)
        return lax.dot_general(qs, window(k_ref, a), (((1,), (1,)), ((), ())), preferred_element_type=F32)

    def finish(a, s):
        off = first_key_row(a) - (r0 + a) + (NA_ROWS - 1)
        bias = jnp.concatenate(
            [jnp.concatenate([t_ref[half, off + 2 * j] for j in range(NA_ROWS // 2)], axis=1)
             for half in (0, 1)], axis=0)
        s = s.reshape(2, GRID_W, nkeys) + bias.reshape(2, GRID_W, nkeys)
        s = jnp.where(col_valid[None], s, MASK_VALUE)
        m = jnp.max(s, axis=-1, keepdims=True)
        e = jnp.exp2(s - m)
        den = jnp.sum(e, axis=-1, keepdims=True)
        pv = jnp.dot(e.reshape(2 * GRID_W, nkeys).astype(BF16), window(v_ref, a),
                     preferred_element_type=F32)
        pv = pv.reshape(2, GRID_W, LANES) * (1.0 / den)
        o_ref[a * GRID_W:(a + 1) * GRID_W, :] = jnp.where(lo, pv[0], pv[1])

    ahead = min(4, rblk)
    pending = [scores(a) for a in range(ahead)]
    for a in range(rblk):
        s_cur = pending.pop(0)
        if a + ahead < rblk:
            pending.append(scores(a + ahead))
        finish(a, s_cur)


def _attn_b(proj, t_pair, *, bsz, s_len, rblk):
    n = proj.shape[0]
    rows = s_len // GRID_W
    nrb = rows // rblk
    kern = functools.partial(_attn_b_kernel, rblk=rblk, rows=rows)
    tq = rblk * GRID_W
    return pl.pallas_call(
        kern,
        grid=(bsz, 4, nrb),
        in_specs=[pl.BlockSpec((tq, LANES), lambda b, h, i: (b * nrb + i, 6 + h)),
                  pl.BlockSpec((s_len, LANES), lambda b, h, i: (b, 10 + h)),
                  pl.BlockSpec((s_len, LANES), lambda b, h, i: (b, 14 + h)),
                  pl.BlockSpec((2, 2 * NA_ROWS - 2, GRID_W, LANES), lambda b, h, i: (h, 0, 0, 0))],
        out_specs=pl.BlockSpec((tq, LANES), lambda b, h, i: (b * nrb + i, h)),
        out_shape=jax.ShapeDtypeStruct((n, 4 * LANES), F32),
        compiler_params=_params("parallel", "parallel", "arbitrary"),
        name="attn_b",
    )(proj, proj, proj, t_pair)


def _c_prep_kernel(x_ref, g_ref, e_ref, cos_ref, sin_ref, o_ref):
    x = x_ref[...].astype(F32)
    xx = x * x
    xx_hi = xx.astype(BF16)
    xx_lo = (xx - xx_hi.astype(F32)).astype(BF16)
    ssq = jnp.dot(jnp.concatenate([xx_hi, xx_lo], axis=1), e_ref[...], preferred_element_type=F32)
    y = x * lax.rsqrt(ssq * (1.0 / HEAD_DIM) + EPS) * g_ref[0]
    o_ref[...] = (y * cos_ref[...] + pltpu.roll(y, LANES // 2, 1) * sin_ref[...]).astype(o_ref.dtype)


def _c_prep(proj, gains, same_head, cos_t, sin_t, *, s_len, tr, first):
    n = proj.shape[0]
    nblk = gains.shape[0]
    npos = s_len // tr
    pos_spec = pl.BlockSpec((tr, LANES), lambda i, j: (i % npos, 0))
    return pl.pallas_call(
        _c_prep_kernel,
        grid=(n // tr, nblk),
        in_specs=[pl.BlockSpec((tr, LANES), lambda i, j: (i, first + j)),
                  pl.BlockSpec((1, 1, LANES), lambda i, j: (j, 0, 0)),
                  pl.BlockSpec((2 * LANES, LANES), lambda i, j: (0, 0)),
                  pos_spec, pos_spec],
        out_specs=pl.BlockSpec((tr, LANES), lambda i, j: (i, j)),
        out_shape=jax.ShapeDtypeStruct((n, nblk * LANES), BF16),
        compiler_params=_params("parallel", "arbitrary"),
        name="c_prep",
    )(proj, gains, same_head, cos_t, sin_t)


VT_ROWS = 2 * HEAD_DIM


def _c_vprep_kernel(x_ref, o_ref):
    xt = x_ref[...].astype(F32).T
    ones = jnp.ones((VT_ROWS - HEAD_DIM, xt.shape[1]), o_ref.dtype)
    for half in (0, 1):
        o_ref[half * VT_ROWS:half * VT_ROWS + HEAD_DIM, :] = (
            xt[half * HEAD_DIM:(half + 1) * HEAD_DIM].astype(o_ref.dtype))
        o_ref[half * VT_ROWS + HEAD_DIM:(half + 1) * VT_ROWS, :] = ones


def _c_vprep(proj, *, bsz, s_len, tk):
    nk = s_len // tk
    v_first = proj.shape[1] // LANES - 2
    return pl.pallas_call(
        _c_vprep_kernel,
        grid=(bsz, 2, nk),
        in_specs=[pl.BlockSpec((tk, LANES), lambda b, p, i: (b * nk + i, v_first + p))],
        out_specs=pl.BlockSpec((None, None, None, 2 * VT_ROWS, tk), lambda b, p, i: (b, p, i, 0, 0)),
        out_shape=jax.ShapeDtypeStruct((bsz, 2, nk, 2 * VT_ROWS, tk), BF16),
        compiler_params=_params("parallel", "parallel", "arbitrary"),
        name="c_vprep",
    )(proj)


C_GROUP = 256


def _attn_c_kernel(q_ref, k_ref, vt_ref, o_ref, qs_ref, acc_ref, s_ref, *, tq, tk, s_len):
    lo = _first_head_lanes()
    per_blk = tq // C_GROUP
    n_groups = 8 * per_blk
    for r in range(4):
        for j in range(per_blk):
            qb = q_ref[j * C_GROUP:(j + 1) * C_GROUP, r * LANES:(r + 1) * LANES]
            zero = jnp.zeros_like(qb)
            qs_ref[r * per_blk + j] = jnp.where(lo, qb, zero)
            qs_ref[(4 + r) * per_blk + j] = jnp.where(lo, zero, qb)
    acc_ref[...] = jnp.zeros(acc_ref.shape, F32)

    nk = s_len // tk

    def scores(t, slot, groups):
        kt = k_ref[pl.ds(pl.multiple_of(t * tk, tk), tk), :]
        for g in groups:
            s_ref[slot, g] = lax.dot_general(kt, qs_ref[g], (((1,), (1,)), ((), ())),
                                             preferred_element_type=F32)

    def step(t, slot, t_next, m):
        m_out = []
        for g in range(n_groups):
            scores(t_next, 1 - slot, (g,))
            st = s_ref[slot, g]
            m_new = jnp.maximum(m[g], jnp.max(st, axis=0, keepdims=True))
            alpha = jnp.exp2(m[g] - m_new)
            pt = jnp.exp2(st - m_new).astype(BF16)
            half = 0 if g < n_groups // 2 else 1
            vt = vt_ref[t, half * VT_ROWS:(half + 1) * VT_ROWS, :]
            acc_ref[g] = alpha * acc_ref[g] + jnp.dot(vt, pt, preferred_element_type=F32)
            m_out.append(m_new)
        return tuple(m_out)

    scores(0, 0, range(n_groups))

    def two_steps(u, m):
        t0 = 2 * u
        m = step(t0, 0, t0 + 1, m)
        return step(t0 + 1, 1, jnp.minimum(t0 + 2, nk - 1), m)

    m0 = tuple(jnp.full((1, C_GROUP), -jnp.inf, F32) for _ in range(n_groups))
    lax.fori_loop(0, nk // 2, two_steps, m0)
    for r in range(4):
        for j in range(per_blk):
            a_lo = acc_ref[r * per_blk + j]
            a_hi = acc_ref[(4 + r) * per_blk + j]
            o_lo = a_lo[:HEAD_DIM] * (1.0 / a_lo[HEAD_DIM:HEAD_DIM + 1])
            o_hi = a_hi[:HEAD_DIM] * (1.0 / a_hi[HEAD_DIM:HEAD_DIM + 1])
            o_ref[j * C_GROUP:(j + 1) * C_GROUP, r * LANES:(r + 1) * LANES] = (
                jnp.concatenate([o_lo, o_hi], axis=0).T)


def _attn_c(qk, vt, *, bsz, s_len, tq, tk):
    n = qk.shape[0]
    nq = s_len // tq
    nk = s_len // tk
    assert tq % C_GROUP == 0 and nk % 2 == 0
    n_groups = 8 * tq // C_GROUP
    kern = functools.partial(_attn_c_kernel, tq=tq, tk=tk, s_len=s_len)
    return pl.pallas_call(
        kern,
        grid=(bsz, 2, nq),
        in_specs=[pl.BlockSpec((tq, 4 * LANES), lambda b, p, i: (b * nq + i, p)),
                  pl.BlockSpec((s_len, LANES), lambda b, p, i: (b, 8 + p)),
                  pl.BlockSpec((None, None, nk, 2 * VT_ROWS, tk), lambda b, p, i: (b, p, 0, 0, 0))],
        out_specs=pl.BlockSpec((tq, 4 * LANES), lambda b, p, i: (b * nq + i, p)),
        out_shape=jax.ShapeDtypeStruct((n, 8 * LANES), F32),
        scratch_shapes=[pltpu.VMEM((n_groups, C_GROUP, LANES), BF16),
                        pltpu.VMEM((n_groups, VT_ROWS, C_GROUP), F32),
                        pltpu.VMEM((2, n_groups, tk, C_GROUP), F32)],
        compiler_params=_params("parallel", "parallel", "arbitrary"),
        name="attn_c",
    )(qk, qk, vt)


def _mix_out_kernel(x_ref, oa_ref, ob_ref, oc_ref, ga_ref, gb_ref, gc_ref, w_ref, o_ref):
    parts = []
    for o_r, g_r in ((oa_ref, ga_ref), (ob_ref, gb_ref), (oc_ref, gc_ref)):
        o = o_r[...]
        parts.append((o * _rms_scale(o) * g_r[...]).astype(BF16))
    mix = jnp.concatenate(parts, axis=-1)
    o_ref[...] = x_ref[...] + jnp.dot(mix, w_ref[...], preferred_element_type=F32)


def _mix_out(x, oa, ob, oc, ga, gb, gc, w, layer, *, tm):
    n, d = x.shape
    row = lambda width: pl.BlockSpec((tm, width), lambda i: (i, 0))
    const = lambda shape: pl.BlockSpec(shape, lambda i: (0, 0))
    return pl.pallas_call(
        _mix_out_kernel,
        grid=(n // tm,),
        in_specs=[row(d), row(oa.shape[1]), row(ob.shape[1]), row(oc.shape[1]),
                  const((1, oa.shape[1])), const((1, ob.shape[1])), const((1, oc.shape[1])),
                  pl.BlockSpec((None,) + w.shape[1:], lambda i: (layer, 0, 0))],
        out_specs=row(d),
        out_shape=jax.ShapeDtypeStruct((n, d), F32),
        compiler_params=_params("parallel"),
        name="mix_out",
    )(x, oa, ob, oc, ga.reshape(1, -1), gb.reshape(1, -1), gc.reshape(1, -1), w)


def _mlp_kernel(x_ref, g_ref, wu_ref, wd_ref, gf_ref, o_ref, h_ref, *, final_norm):
    f = pl.program_id(1)

    @pl.when(f == 0)
    def _():
        x = x_ref[...]
        h_ref[...] = (x * _rms_scale(x) * g_ref[...]).astype(BF16)
        o_ref[...] = x

    u = jnp.maximum(jnp.dot(h_ref[...], wu_ref[...], preferred_element_type=F32), 0.0)
    o_ref[...] += jnp.dot((u * u).astype(BF16), wd_ref[...], preferred_element_type=F32)

    if final_norm:
        @pl.when(f == pl.num_programs(1) - 1)
        def _():
            y = o_ref[...]
            o_ref[...] = y * _rms_scale(y) * gf_ref[...]


def _mlp(x, gain, w_up, w_down, layer, final_gain, *, tm, tf, final_norm):
    n, d = x.shape
    ff = w_up.shape[2]
    kern = functools.partial(_mlp_kernel, final_norm=final_norm)
    return pl.pallas_call(
        kern,
        grid=(n // tm, ff // tf),
        in_specs=[pl.BlockSpec((tm, d), lambda i, f: (i, 0)),
                  pl.BlockSpec((1, d), lambda i, f: (0, 0)),
                  pl.BlockSpec((None, d, tf), lambda i, f: (layer, 0, f)),
                  pl.BlockSpec((None, tf, d), lambda i, f: (layer, f, 0)),
                  pl.BlockSpec((1, d), lambda i, f: (0, 0))],
        out_specs=pl.BlockSpec((tm, d), lambda i, f: (i, 0)),
        out_shape=jax.ShapeDtypeStruct((n, d), F32),
        scratch_shapes=[pltpu.VMEM((tm, d), BF16)],
        compiler_params=_params("parallel", "arbitrary"),
        name="mlp",
    )(x, gain.reshape(1, d), w_up, w_down, final_gain.reshape(1, d))


def _t5_bucket_np(rel):
    nb = T5_BUCKETS // 2
    max_exact = nb // 2
    base = np.where(rel > 0, nb, 0)
    n = np.abs(rel)
    nf = np.maximum(n, 1).astype(np.float32)
    large = max_exact + (np.log(nf / np.float32(max_exact)) / np.float32(math.log(T5_MAX_DIST / max_exact))
                         * np.float32(nb - max_exact)).astype(np.int32)
    large = np.minimum(large, nb - 1)
    return base + np.where(n < max_exact, n, large)


def _pair_heads(a, axis):
    shape = a.shape
    halves = shape[axis] // (8 * HEAD_DIM)
    a = a.reshape(shape[:axis] + (halves, 2, 4, HEAD_DIM) + shape[axis + 1:])
    return jnp.swapaxes(a, axis + 1, axis + 2).reshape(shape)


def _split_rotary(a, axis):
    shape = a.shape
    a = a.reshape(shape[:axis] + (shape[axis] // LANES, 2, 2, 2, HEAD_DIM // 4) + shape[axis + 1:])
    perm = (tuple(range(axis + 1)) + (axis + 3, axis + 1, axis + 2, axis + 4)
            + tuple(range(axis + 5, a.ndim)))
    return jnp.transpose(a, perm).reshape(shape)


def _first_head_lanes():
    lane = lax.broadcasted_iota(jnp.int32, (1, LANES), 1)
    return (lane // (HEAD_DIM // 2)) % 2 == 0


def _rope_tables(s_len):
    axis_dim = HEAD_DIM // 2
    t = jnp.arange(s_len)
    row = (t // GRID_W).astype(F32)
    col = (t % GRID_W).astype(F32)
    freqs = ROPE_THETA ** (-jnp.arange(0, axis_dim, 2, dtype=F32) / axis_dim)
    ang = jnp.concatenate([row[:, None] * freqs[None, :], col[:, None] * freqs[None, :]], axis=-1)
    ang = jnp.tile(ang, (1, 4))
    sign = np.where(np.arange(LANES) < LANES // 2, -1.0, 1.0).astype(np.float32)
    return jnp.cos(ang), jnp.sin(ang) * sign


def kernel(x, norm_mix, w_in, a_sink, t5_table, b_rpb, c_q_gain, c_k_gain, out_gain_a, out_gain_b,
           out_gain_c, w_o, norm_mlp, w_up, w_down, norm_final):
    bsz, s_len, d_model = x.shape
    depth = w_in.shape[0]
    n = bsz * s_len
    scale = HEAD_DIM ** -0.5

    a_w = out_gain_a.shape[1]
    b_w = out_gain_b.shape[1]
    c_w = out_gain_c.shape[1]
    a_kv_w = a_w // 4
    c_kv_w = c_w // 4
    off_qb = a_w + 2 * a_kv_w
    off_qc = off_qb + 3 * b_w
    in_width = off_qc + c_w + 2 * c_kv_w
    assert in_width == w_in.shape[2] and a_w == 8 * HEAD_DIM and c_w == 16 * HEAD_DIM

    off_vc = off_qc + c_w + c_kv_w
    w_in_p = jnp.concatenate(
        [_pair_heads(w_in[..., :a_w], 2) * (scale * LOG2E), w_in[..., a_w:off_qb],
         w_in[..., off_qb:off_qb + b_w] * (scale * LOG2E), w_in[..., off_qb + b_w:off_qc],
         _split_rotary(_pair_heads(w_in[..., off_qc:off_qc + c_w], 2), 2),
         _split_rotary(w_in[..., off_qc + c_w:off_vc], 2), w_in[..., off_vc:]], axis=2).astype(BF16)
    w_o_p = jnp.concatenate(
        [_pair_heads(w_o[:, :a_w], 1), w_o[:, a_w:a_w + b_w], _pair_heads(w_o[:, a_w + b_w:], 1)],
        axis=1).astype(BF16)
    w_up_b = w_up.astype(BF16)
    w_down_b = w_down.astype(BF16)

    qi = np.arange(A_BLOCK)[:, None]
    kj = np.arange(3 * A_BLOCK)[None, :]
    bucket = _t5_bucket_np(kj - A_BLOCK - qi)
    bias_a = t5_table[bucket].astype(F32) * LOG2E
    bias_t = bias_a.reshape(A_BLOCK, 3 * A_BLOCK, 2, 4).transpose(2, 1, 3, 0)
    bias_t = bias_t.reshape(2, 3 * A_BLOCK, 4 * A_BLOCK)
    sink_t = jnp.repeat(a_sink.astype(F32) * LOG2E, A_BLOCK, axis=1).reshape(depth, 2, 1, 4 * A_BLOCK)

    cq = np.arange(GRID_W)[:, None]
    ck = np.arange(GRID_W)[None, :]
    dc = np.clip(ck - cq + NA_COLS - 1, 0, 2 * NA_COLS - 2)
    t_blocks = b_rpb[:, :, :, dc].astype(F32) * LOG2E
    t_pair = jnp.concatenate([t_blocks[:, :, :-1], t_blocks[:, :, 1:]], axis=-1)

    cos_t, sin_t = _rope_tables(s_len)
    two = lambda g: _split_rotary(jnp.concatenate([g, g], axis=-1), 0)
    n_qblk, n_kblk = c_w // LANES, c_kv_w // LANES
    head_of_lane = (np.arange(LANES) // (HEAD_DIM // 2)) % 2
    same_head = jnp.asarray(np.tile(head_of_lane[:, None] == head_of_lane[None, :], (2, 1)), BF16)

    xf = x.reshape(n, d_model)
    for l in range(depth):
        proj = _inproj(xf, norm_mix[l], w_in_p, l, tm=min(1024, n), tn=in_width // 3)
        vt_a = _vt_prep(proj, a_w // LANES + 1, 1, bsz=bsz, s_len=s_len, tr=min(1024, s_len))
        oa = _attn_a(proj, vt_a, bias_t, sink_t[l], bsz=bsz, s_len=s_len, tq=min(512, s_len),
                     k_block=a_w // LANES)
        ob = _attn_b(proj, t_pair[l], bsz=bsz, s_len=s_len, rblk=8)
        gains = jnp.concatenate([jnp.tile(two(c_q_gain[l] * (scale * LOG2E))[None], (n_qblk, 1)),
                                 jnp.tile(two(c_k_gain[l])[None], (n_kblk, 1))], axis=0)
        qk = _c_prep(proj, gains.reshape(n_qblk + n_kblk, 1, LANES).astype(F32), same_head, cos_t, sin_t,
                     s_len=s_len, tr=min(2048, s_len), first=off_qc // LANES)
        tk_c = min(512, s_len // 2)
        vt = _c_vprep(proj, bsz=bsz, s_len=s_len, tk=tk_c)
        oc = _attn_c(qk, vt, bsz=bsz, s_len=s_len, tq=min(512, s_len), tk=tk_c)
        xf = _mix_out(xf, oa, ob, oc, _pair_heads(out_gain_a[l], 0), out_gain_b[l],
                      _pair_heads(out_gain_c[l], 0), w_o_p, l, tm=min(512, n))
        xf = _mlp(xf, norm_mlp[l], w_up_b, w_down_b, l, norm_final,
                  tm=min(512, n), tf=512, final_norm=(l == depth - 1))
    return xf.reshape(bsz, s_len, d_model)
```

```python
import functools
import math

import numpy as np
import jax
import jax.numpy as jnp
from jax import lax
from jax.experimental import pallas as pl
from jax.experimental.pallas import tpu as pltpu

HEAD_DIM = 64
LANES = 128
WINDOW = 128
A_BLOCK = 128
T5_BUCKETS = 32
T5_MAX_DIST = 128
GRID_W = 64
NA_ROWS = 8
NA_COLS = 16
ROPE_THETA = 10000.0
EPS = 1e-6
MASK_VALUE = -1e30
LOG2E = math.log2(math.e)
VMEM_LIMIT = 56 * 1024 * 1024

F32 = jnp.float32
BF16 = jnp.bfloat16


def _params(*sem):
    return pltpu.CompilerParams(dimension_semantics=sem, vmem_limit_bytes=VMEM_LIMIT)


def _rms_scale(x):
    return lax.rsqrt(jnp.mean(x * x, axis=-1, keepdims=True) + EPS)


def _lo_lanes():
    return lax.broadcasted_iota(jnp.int32, (1, LANES), 1) < HEAD_DIM


def _inproj_kernel(x_ref, g_ref, w_ref, o_ref, h_ref):
    @pl.when(pl.program_id(1) == 0)
    def _():
        x = x_ref[...]
        h_ref[...] = (x * _rms_scale(x) * g_ref[...]).astype(BF16)

    o_ref[...] = jnp.dot(h_ref[...], w_ref[...], preferred_element_type=F32).astype(o_ref.dtype)


def _inproj(x, gain, w, layer, *, tm, tn):
    n, d = x.shape
    e = w.shape[2]
    return pl.pallas_call(
        _inproj_kernel,
        grid=(n // tm, e // tn),
        in_specs=[pl.BlockSpec((tm, d), lambda i, j: (i, 0)),
                  pl.BlockSpec((1, d), lambda i, j: (0, 0)),
                  pl.BlockSpec((None, d, tn), lambda i, j: (layer, 0, j))],
        out_specs=pl.BlockSpec((tm, tn), lambda i, j: (i, j)),
        out_shape=jax.ShapeDtypeStruct((n, e), BF16),
        scratch_shapes=[pltpu.VMEM((tm, d), BF16)],
        compiler_params=_params("parallel", "arbitrary"),
        name="inproj",
    )(x, gain.reshape(1, d), w)


def _vt_prep_kernel(x_ref, o_ref):
    for t in range(o_ref.shape[0]):
        o_ref[t] = x_ref[t * LANES:(t + 1) * LANES, :].astype(F32).T.astype(o_ref.dtype)


def _vt_prep(proj, first, count, *, bsz, s_len, tr):
    nt = s_len // tr
    return pl.pallas_call(
        _vt_prep_kernel,
        grid=(bsz, count, nt),
        in_specs=[pl.BlockSpec((tr, LANES), lambda b, c, i: (b * nt + i, first + c))],
        out_specs=pl.BlockSpec((None, None, tr // LANES, LANES, LANES), lambda b, c, i: (b, c, i, 0, 0)),
        out_shape=jax.ShapeDtypeStruct((bsz, count, s_len // LANES, LANES, LANES), BF16),
        compiler_params=_params("parallel", "parallel", "arbitrary"),
        name="vt_prep",
    )(proj)


def _attn_a_kernel(q_ref, k_ref, vt_ref, bias_ref, sink_ref, o_ref, *, tq, s_len):
    i = pl.program_id(1)
    lo = _lo_lanes()
    first_rows = lax.broadcasted_iota(jnp.int32, (LANES, 1), 0) < HEAD_DIM
    kj = lax.broadcasted_iota(jnp.int32, (3 * A_BLOCK, A_BLOCK), 0)
    qi = lax.broadcasted_iota(jnp.int32, (3 * A_BLOCK, A_BLOCK), 1)
    in_window = jnp.abs(kj - A_BLOCK - qi) <= WINDOW
    n_blocks = s_len // A_BLOCK

    def key_blocks(j):
        blk = i * (tq // A_BLOCK) + j
        return [jnp.clip(blk + o, 0, n_blocks - 1) for o in (-1, 0, 1)]

    def scores(j, half):
        keep = lo if half == 0 else jnp.logical_not(lo)
        rows = slice(j * A_BLOCK, (j + 1) * A_BLOCK)
        qs = jnp.concatenate(
            [jnp.where(keep, q_ref[rows, r * LANES:(r + 1) * LANES], jnp.zeros((), BF16)) for r in range(4)],
            axis=0)
        k3 = jnp.concatenate([k_ref[pl.ds(pl.multiple_of(b * A_BLOCK, A_BLOCK), A_BLOCK), :]
                              for b in key_blocks(j)], axis=0)
        return lax.dot_general(k3, qs, (((1,), (1,)), ((), ())), preferred_element_type=F32)

    def finish(j, half, s):
        key_pos = (i * (tq // A_BLOCK) + j - 1) * A_BLOCK + kj
        valid = in_window & (key_pos >= 0) & (key_pos < s_len)
        heads = [slice(r * A_BLOCK, (r + 1) * A_BLOCK) for r in range(4)]
        s = jnp.concatenate([jnp.where(valid, s[:, c] + bias_ref[half, :, c], MASK_VALUE) for c in heads], axis=1)
        sink = sink_ref[half]
        m = jnp.maximum(jnp.max(s, axis=0, keepdims=True), sink)
        p = jnp.exp2(s - m).astype(BF16)
        v3t = jnp.concatenate([vt_ref[b] for b in key_blocks(j)], axis=1)
        ones = jnp.ones((), BF16)
        lhs = jnp.where(first_rows, v3t, ones) if half == 0 else jnp.where(first_rows, ones, v3t)
        acc = jnp.dot(lhs, p, preferred_element_type=F32)
        num, den = (acc[:HEAD_DIM], acc[HEAD_DIM:HEAD_DIM + 1]) if half == 0 else (acc[HEAD_DIM:], acc[:1])
        return num * (1.0 / (den + jnp.exp2(sink - m)))

    units = [(j, half) for j in range(tq // A_BLOCK) for half in (0, 1)]
    ahead = 2
    pending = [scores(*u) for u in units[:ahead]]
    lo_half = None
    for n, (j, half) in enumerate(units):
        s_cur = pending.pop(0)
        if n + ahead < len(units):
            pending.append(scores(*units[n + ahead]))
        out = finish(j, half, s_cur)
        if half == 0:
            lo_half = out
        else:
            rows = slice(j * A_BLOCK, (j + 1) * A_BLOCK)
            for r in range(4):
                cols = slice(r * A_BLOCK, (r + 1) * A_BLOCK)
                o_ref[rows, r * LANES:(r + 1) * LANES] = (
                    jnp.concatenate([lo_half[:, cols], out[:, cols]], axis=0).T.astype(o_ref.dtype))


def _attn_a(proj, vt, bias_t, sink_t, *, bsz, s_len, tq, k_block):
    n = proj.shape[0]
    nq = s_len // tq
    kern = functools.partial(_attn_a_kernel, tq=tq, s_len=s_len)
    return pl.pallas_call(
        kern,
        grid=(bsz, nq),
        in_specs=[pl.BlockSpec((tq, 4 * LANES), lambda b, i: (b * nq + i, 0)),
                  pl.BlockSpec((s_len, LANES), lambda b, i: (b, k_block)),
                  pl.BlockSpec((None, None, s_len // LANES, LANES, LANES), lambda b, i: (b, 0, 0, 0, 0)),
                  pl.BlockSpec(bias_t.shape, lambda b, i: (0, 0, 0)),
                  pl.BlockSpec(sink_t.shape, lambda b, i: (0, 0, 0))],
        out_specs=pl.BlockSpec((tq, 4 * LANES), lambda b, i: (b * nq + i, 0)),
        out_shape=jax.ShapeDtypeStruct((n, 4 * LANES), BF16),
        compiler_params=_params("parallel", "arbitrary"),
        name="attn_a",
    )(proj, proj, vt, bias_t, sink_t)


def _attn_b_kernel(q_ref, k_ref, v_ref, t_ref, o_ref, *, rblk, rows):
    r0 = pl.program_id(2) * rblk
    lo = _lo_lanes()
    nkeys = NA_ROWS * GRID_W
    cq = lax.broadcasted_iota(jnp.int32, (GRID_W, nkeys), 0)
    ck = lax.broadcasted_iota(jnp.int32, (GRID_W, nkeys), 1) % GRID_W
    cs = jnp.clip(cq - NA_COLS // 2, 0, GRID_W - NA_COLS)
    col_valid = (ck >= cs) & (ck < cs + NA_COLS)

    def first_key_row(a):
        return jnp.clip(r0 + a - NA_ROWS // 2, 0, rows - NA_ROWS)

    def window(ref, a):
        return ref[pl.ds(pl.multiple_of(first_key_row(a) * GRID_W, GRID_W), nkeys), :]

    def scores(a):
        qrow = q_ref[a * GRID_W:(a + 1) * GRID_W, :]
        zero = jnp.zeros_like(qrow)
        qs = jnp.concatenate([jnp.where(lo, qrow, zero), jnp.where(lo, zero, qrow)], axis=0)
        return lax.dot_general(qs, window(k_ref, a), (((1,), (1,)), ((), ())), preferred_element_type=F32)

    def finish(a, s):
        off = first_key_row(a) - (r0 + a) + (NA_ROWS - 1)
        bias = jnp.concatenate(
            [jnp.concatenate([t_ref[half, off + 2 * j] for j in range(NA_ROWS // 2)], axis=1)
             for half in (0, 1)], axis=0)
        s = s.reshape(2, GRID_W, nkeys) + bias.reshape(2, GRID_W, nkeys)
        s = jnp.where(col_valid[None], s, MASK_VALUE)
        m = jnp.max(s, axis=-1, keepdims=True)
        e = jnp.exp2(s - m)
        den = jnp.sum(e, axis=-1, keepdims=True)
        pv = jnp.dot(e.reshape(2 * GRID_W, nkeys).astype(BF16), window(v_ref, a),
                     preferred_element_type=F32)
        pv = pv.reshape(2, GRID_W, LANES) * (1.0 / den)
        o_ref[a * GRID_W:(a + 1) * GRID_W, :] = jnp.where(lo, pv[0], pv[1]).astype(o_ref.dtype)

    ahead = min(4, rblk)
    pending = [scores(a) for a in range(ahead)]
    for a in range(rblk):
        s_cur = pending.pop(0)
        if a + ahead < rblk:
            pending.append(scores(a + ahead))
        finish(a, s_cur)


def _attn_b(proj, t_pair, *, bsz, s_len, rblk, q_block):
    n = proj.shape[0]
    k_block, v_block = q_block + 4, q_block + 8
    rows = s_len // GRID_W
    nrb = rows // rblk
    kern = functools.partial(_attn_b_kernel, rblk=rblk, rows=rows)
    tq = rblk * GRID_W
    return pl.pallas_call(
        kern,
        grid=(bsz, 4, nrb),
        in_specs=[pl.BlockSpec((tq, LANES), lambda b, h, i: (b * nrb + i, q_block + h)),
                  pl.BlockSpec((s_len, LANES), lambda b, h, i: (b, k_block + h)),
                  pl.BlockSpec((s_len, LANES), lambda b, h, i: (b, v_block + h)),
                  pl.BlockSpec((2, 2 * NA_ROWS - 2, GRID_W, LANES), lambda b, h, i: (h, 0, 0, 0))],
        out_specs=pl.BlockSpec((tq, LANES), lambda b, h, i: (b * nrb + i, h)),
        out_shape=jax.ShapeDtypeStruct((n, 4 * LANES), BF16),
        compiler_params=_params("parallel", "parallel", "arbitrary"),
        name="attn_b",
    )(proj, proj, proj, t_pair)


def _c_prep_kernel(x_ref, g_ref, e_ref, cos_ref, sin_ref, o_ref):
    x = x_ref[...].astype(F32)
    xx = x * x
    xx_hi = xx.astype(BF16)
    xx_lo = (xx - xx_hi.astype(F32)).astype(BF16)
    ssq = jnp.dot(jnp.concatenate([xx_hi, xx_lo], axis=1), e_ref[...], preferred_element_type=F32)
    y = x * lax.rsqrt(ssq * (1.0 / HEAD_DIM) + EPS) * g_ref[0]
    o_ref[...] = (y * cos_ref[...] + pltpu.roll(y, LANES // 2, 1) * sin_ref[...]).astype(o_ref.dtype)


def _c_prep(proj, gains, same_head, cos_t, sin_t, *, s_len, tr, first):
    n = proj.shape[0]
    nblk = gains.shape[0]
    npos = s_len // tr
    pos_spec = pl.BlockSpec((tr, LANES), lambda i, j: (i % npos, 0))
    return pl.pallas_call(
        _c_prep_kernel,
        grid=(n // tr, nblk),
        in_specs=[pl.BlockSpec((tr, LANES), lambda i, j: (i, first + j)),
                  pl.BlockSpec((1, 1, LANES), lambda i, j: (j, 0, 0)),
                  pl.BlockSpec((2 * LANES, LANES), lambda i, j: (0, 0)),
                  pos_spec, pos_spec],
        out_specs=pl.BlockSpec((tr, LANES), lambda i, j: (i, j)),
        out_shape=jax.ShapeDtypeStruct((n, nblk * LANES), BF16),
        compiler_params=_params("parallel", "arbitrary"),
        name="c_prep",
    )(proj, gains, same_head, cos_t, sin_t)


VT_ROWS = 2 * HEAD_DIM


def _c_vprep_kernel(x_ref, o_ref):
    xt = x_ref[...].astype(F32).T
    ones = jnp.ones((VT_ROWS - HEAD_DIM, xt.shape[1]), o_ref.dtype)
    for half in (0, 1):
        o_ref[half * VT_ROWS:half * VT_ROWS + HEAD_DIM, :] = (
            xt[half * HEAD_DIM:(half + 1) * HEAD_DIM].astype(o_ref.dtype))
        o_ref[half * VT_ROWS + HEAD_DIM:(half + 1) * VT_ROWS, :] = ones


def _c_vprep(proj, *, bsz, s_len, tk):
    nk = s_len // tk
    v_first = proj.shape[1] // LANES - 2
    return pl.pallas_call(
        _c_vprep_kernel,
        grid=(bsz, 2, nk),
        in_specs=[pl.BlockSpec((tk, LANES), lambda b, p, i: (b * nk + i, v_first + p))],
        out_specs=pl.BlockSpec((None, None, None, 2 * VT_ROWS, tk), lambda b, p, i: (b, p, i, 0, 0)),
        out_shape=jax.ShapeDtypeStruct((bsz, 2, nk, 2 * VT_ROWS, tk), BF16),
        compiler_params=_params("parallel", "parallel", "arbitrary"),
        name="c_vprep",
    )(proj)


C_GROUP = 256


def _attn_c_kernel(q_ref, k_ref, vt_ref, o_ref, qs_ref, acc_ref, s_ref, *, tq, tk, s_len):
    lo = _first_head_lanes()
    per_blk = tq // C_GROUP
    n_groups = 8 * per_blk
    for r in range(4):
        for j in range(per_blk):
            qb = q_ref[j * C_GROUP:(j + 1) * C_GROUP, r * LANES:(r + 1) * LANES]
            zero = jnp.zeros_like(qb)
            qs_ref[r * per_blk + j] = jnp.where(lo, qb, zero)
            qs_ref[(4 + r) * per_blk + j] = jnp.where(lo, zero, qb)
    acc_ref[...] = jnp.zeros(acc_ref.shape, F32)

    nk = s_len // tk

    def scores(t, slot, groups):
        kt = k_ref[pl.ds(pl.multiple_of(t * tk, tk), tk), :]
        for g in groups:
            s_ref[slot, g] = lax.dot_general(kt, qs_ref[g], (((1,), (1,)), ((), ())),
                                             preferred_element_type=F32)

    def step(t, slot, t_next, m):
        m_out = []
        for g in range(n_groups):
            scores(t_next, 1 - slot, (g,))
            st = s_ref[slot, g]
            m_new = jnp.maximum(m[g], jnp.max(st, axis=0, keepdims=True))
            alpha = jnp.exp2(m[g] - m_new)
            pt = jnp.exp2(st - m_new).astype(BF16)
            half = 0 if g < n_groups // 2 else 1
            vt = vt_ref[t, half * VT_ROWS:(half + 1) * VT_ROWS, :]
            acc_ref[g] = alpha * acc_ref[g] + jnp.dot(vt, pt, preferred_element_type=F32)
            m_out.append(m_new)
        return tuple(m_out)

    scores(0, 0, range(n_groups))

    def two_steps(u, m):
        t0 = 2 * u
        m = step(t0, 0, t0 + 1, m)
        return step(t0 + 1, 1, jnp.minimum(t0 + 2, nk - 1), m)

    m0 = tuple(jnp.full((1, C_GROUP), -jnp.inf, F32) for _ in range(n_groups))
    lax.fori_loop(0, nk // 2, two_steps, m0)
    for r in range(4):
        for j in range(per_blk):
            a_lo = acc_ref[r * per_blk + j]
            a_hi = acc_ref[(4 + r) * per_blk + j]
            o_lo = a_lo[:HEAD_DIM] * (1.0 / a_lo[HEAD_DIM:HEAD_DIM + 1])
            o_hi = a_hi[:HEAD_DIM] * (1.0 / a_hi[HEAD_DIM:HEAD_DIM + 1])
            o_ref[j * C_GROUP:(j + 1) * C_GROUP, r * LANES:(r + 1) * LANES] = (
                jnp.concatenate([o_lo, o_hi], axis=0).T.astype(o_ref.dtype))


def _attn_c(qk, vt, *, bsz, s_len, tq, tk):
    n = qk.shape[0]
    nq = s_len // tq
    nk = s_len // tk
    assert tq % C_GROUP == 0 and nk % 2 == 0
    n_groups = 8 * tq // C_GROUP
    k_first = qk.shape[1] // LANES - 2
    kern = functools.partial(_attn_c_kernel, tq=tq, tk=tk, s_len=s_len)
    return pl.pallas_call(
        kern,
        grid=(bsz, 2, nq),
        in_specs=[pl.BlockSpec((tq, 4 * LANES), lambda b, p, i: (b * nq + i, p)),
                  pl.BlockSpec((s_len, LANES), lambda b, p, i: (b, k_first + p)),
                  pl.BlockSpec((None, None, nk, 2 * VT_ROWS, tk), lambda b, p, i: (b, p, 0, 0, 0))],
        out_specs=pl.BlockSpec((tq, 4 * LANES), lambda b, p, i: (b * nq + i, p)),
        out_shape=jax.ShapeDtypeStruct((n, 8 * LANES), BF16),
        scratch_shapes=[pltpu.VMEM((n_groups, C_GROUP, LANES), BF16),
                        pltpu.VMEM((n_groups, VT_ROWS, C_GROUP), F32),
                        pltpu.VMEM((2, n_groups, tk, C_GROUP), F32)],
        compiler_params=_params("parallel", "parallel", "arbitrary"),
        name="attn_c",
    )(qk, qk, vt)


def _mix_out_kernel(x_ref, oa_ref, ob_ref, oc_ref, ga_ref, gb_ref, gc_ref, w_ref, o_ref):
    parts = []
    for o_r, g_r in ((oa_ref, ga_ref), (ob_ref, gb_ref), (oc_ref, gc_ref)):
        o = o_r[...].astype(F32)
        parts.append((o * _rms_scale(o) * g_r[...]).astype(BF16))
    mix = jnp.concatenate(parts, axis=-1)
    o_ref[...] = x_ref[...] + jnp.dot(mix, w_ref[...], preferred_element_type=F32)


def _mix_out(x, oa, ob, oc, ga, gb, gc, w, layer, *, tm):
    n, d = x.shape
    row = lambda width: pl.BlockSpec((tm, width), lambda i: (i, 0))
    const = lambda shape: pl.BlockSpec(shape, lambda i: (0, 0))
    return pl.pallas_call(
        _mix_out_kernel,
        grid=(n // tm,),
        in_specs=[row(d), row(oa.shape[1]), row(ob.shape[1]), row(oc.shape[1]),
                  const((1, oa.shape[1])), const((1, ob.shape[1])), const((1, oc.shape[1])),
                  pl.BlockSpec((None,) + w.shape[1:], lambda i: (layer, 0, 0))],
        out_specs=row(d),
        out_shape=jax.ShapeDtypeStruct((n, d), F32),
        compiler_params=_params("parallel"),
        name="mix_out",
    )(x, oa, ob, oc, ga.reshape(1, -1), gb.reshape(1, -1), gc.reshape(1, -1), w)


def _mlp_kernel(x_ref, g_ref, wu_ref, wd_ref, gf_ref, o_ref, h_ref, *, final_norm):
    f = pl.program_id(1)

    @pl.when(f == 0)
    def _():
        x = x_ref[...]
        h_ref[...] = (x * _rms_scale(x) * g_ref[...]).astype(BF16)
        o_ref[...] = x

    u = jnp.maximum(jnp.dot(h_ref[...], wu_ref[...], preferred_element_type=F32), 0.0)
    o_ref[...] += jnp.dot((u * u).astype(BF16), wd_ref[...], preferred_element_type=F32)

    if final_norm:
        @pl.when(f == pl.num_programs(1) - 1)
        def _():
            y = o_ref[...]
            o_ref[...] = y * _rms_scale(y) * gf_ref[...]


def _mlp(x, gain, w_up, w_down, layer, final_gain, *, tm, tf, final_norm):
    n, d = x.shape
    ff = w_up.shape[2]
    kern = functools.partial(_mlp_kernel, final_norm=final_norm)
    return pl.pallas_call(
        kern,
        grid=(n // tm, ff // tf),
        in_specs=[pl.BlockSpec((tm, d), lambda i, f: (i, 0)),
                  pl.BlockSpec((1, d), lambda i, f: (0, 0)),
                  pl.BlockSpec((None, d, tf), lambda i, f: (layer, 0, f)),
                  pl.BlockSpec((None, tf, d), lambda i, f: (layer, f, 0)),
                  pl.BlockSpec((1, d), lambda i, f: (0, 0))],
        out_specs=pl.BlockSpec((tm, d), lambda i, f: (i, 0)),
        out_shape=jax.ShapeDtypeStruct((n, d), F32),
        scratch_shapes=[pltpu.VMEM((tm, d), BF16)],
        compiler_params=_params("parallel", "arbitrary"),
        name="mlp",
    )(x, gain.reshape(1, d), w_up, w_down, final_gain.reshape(1, d))


def _t5_bucket_np(rel):
    nb = T5_BUCKETS // 2
    max_exact = nb // 2
    base = np.where(rel > 0, nb, 0)
    n = np.abs(rel)
    nf = np.maximum(n, 1).astype(np.float32)
    large = max_exact + (np.log(nf / np.float32(max_exact)) / np.float32(math.log(T5_MAX_DIST / max_exact))
                         * np.float32(nb - max_exact)).astype(np.int32)
    large = np.minimum(large, nb - 1)
    return base + np.where(n < max_exact, n, large)


def _pair_heads(a, axis):
    shape = a.shape
    halves = shape[axis] // (8 * HEAD_DIM)
    a = a.reshape(shape[:axis] + (halves, 2, 4, HEAD_DIM) + shape[axis + 1:])
    return jnp.swapaxes(a, axis + 1, axis + 2).reshape(shape)


def _split_rotary(a, axis):
    shape = a.shape
    a = a.reshape(shape[:axis] + (shape[axis] // LANES, 2, 2, 2, HEAD_DIM // 4) + shape[axis + 1:])
    perm = (tuple(range(axis + 1)) + (axis + 3, axis + 1, axis + 2, axis + 4)
            + tuple(range(axis + 5, a.ndim)))
    return jnp.transpose(a, perm).reshape(shape)


def _first_head_lanes():
    lane = lax.broadcasted_iota(jnp.int32, (1, LANES), 1)
    return (lane // (HEAD_DIM // 2)) % 2 == 0


def _rope_tables(s_len):
    axis_dim = HEAD_DIM // 2
    t = jnp.arange(s_len)
    row = (t // GRID_W).astype(F32)
    col = (t % GRID_W).astype(F32)
    freqs = ROPE_THETA ** (-jnp.arange(0, axis_dim, 2, dtype=F32) / axis_dim)
    ang = jnp.concatenate([row[:, None] * freqs[None, :], col[:, None] * freqs[None, :]], axis=-1)
    ang = jnp.tile(ang, (1, 4))
    sign = np.where(np.arange(LANES) < LANES // 2, -1.0, 1.0).astype(np.float32)
    return jnp.cos(ang), jnp.sin(ang) * sign


def kernel(x, norm_mix, w_in, a_sink, t5_table, b_rpb, c_q_gain, c_k_gain, out_gain_a, out_gain_b,
           out_gain_c, w_o, norm_mlp, w_up, w_down, norm_final):
    bsz, s_len, d_model = x.shape
    depth = w_in.shape[0]
    n = bsz * s_len
    scale = HEAD_DIM ** -0.5

    a_w = out_gain_a.shape[1]
    b_w = out_gain_b.shape[1]
    c_w = out_gain_c.shape[1]
    a_kv_w = a_w // 4
    c_kv_w = c_w // 4
    off_qb = a_w + 2 * a_kv_w
    off_qc = off_qb + 3 * b_w
    in_width = off_qc + c_w + 2 * c_kv_w
    assert in_width == w_in.shape[2] and a_w == 8 * HEAD_DIM and c_w == 16 * HEAD_DIM

    off_vc = off_qc + c_w + c_kv_w
    w_in_p = jnp.concatenate(
        [_pair_heads(w_in[..., :a_w], 2) * (scale * LOG2E), w_in[..., a_w:off_qb],
         w_in[..., off_qb:off_qb + b_w] * (scale * LOG2E), w_in[..., off_qb + b_w:off_qc],
         _split_rotary(_pair_heads(w_in[..., off_qc:off_qc + c_w], 2), 2),
         _split_rotary(w_in[..., off_qc + c_w:off_vc], 2), w_in[..., off_vc:]], axis=2).astype(BF16)
    w_o_p = jnp.concatenate(
        [_pair_heads(w_o[:, :a_w], 1), w_o[:, a_w:a_w + b_w], _pair_heads(w_o[:, a_w + b_w:], 1)],
        axis=1).astype(BF16)
    w_up_b = w_up.astype(BF16)
    w_down_b = w_down.astype(BF16)

    qi = np.arange(A_BLOCK)[:, None]
    kj = np.arange(3 * A_BLOCK)[None, :]
    bucket = _t5_bucket_np(kj - A_BLOCK - qi)
    bias_a = t5_table[bucket].astype(F32) * LOG2E
    bias_t = bias_a.reshape(A_BLOCK, 3 * A_BLOCK, 2, 4).transpose(2, 1, 3, 0)
    bias_t = bias_t.reshape(2, 3 * A_BLOCK, 4 * A_BLOCK)
    sink_t = jnp.repeat(a_sink.astype(F32) * LOG2E, A_BLOCK, axis=1).reshape(depth, 2, 1, 4 * A_BLOCK)

    cq = np.arange(GRID_W)[:, None]
    ck = np.arange(GRID_W)[None, :]
    dc = np.clip(ck - cq + NA_COLS - 1, 0, 2 * NA_COLS - 2)
    t_blocks = b_rpb[:, :, :, dc].astype(F32) * LOG2E
    t_pair = jnp.concatenate([t_blocks[:, :, :-1], t_blocks[:, :, 1:]], axis=-1)

    cos_t, sin_t = _rope_tables(s_len)
    two = lambda g: _split_rotary(jnp.concatenate([g, g], axis=-1), 0)
    n_qblk, n_kblk = c_w // LANES, c_kv_w // LANES
    head_of_lane = (np.arange(LANES) // (HEAD_DIM // 2)) % 2
    same_head = jnp.asarray(np.tile(head_of_lane[:, None] == head_of_lane[None, :], (2, 1)), BF16)

    xf = x.reshape(n, d_model)
    for l in range(depth):
        proj = _inproj(xf, norm_mix[l], w_in_p, l, tm=min(1024, n), tn=in_width // 3)
        vt_a = _vt_prep(proj, a_w // LANES + 1, 1, bsz=bsz, s_len=s_len, tr=min(1024, s_len))
        oa = _attn_a(proj, vt_a, bias_t, sink_t[l], bsz=bsz, s_len=s_len, tq=min(512, s_len),
                     k_block=a_w // LANES)
        ob = _attn_b(proj, t_pair[l], bsz=bsz, s_len=s_len, rblk=8, q_block=off_qb // LANES)
        gains = jnp.concatenate([jnp.tile(two(c_q_gain[l] * (scale * LOG2E))[None], (n_qblk, 1)),
                                 jnp.tile(two(c_k_gain[l])[None], (n_kblk, 1))], axis=0)
        qk = _c_prep(proj, gains.reshape(n_qblk + n_kblk, 1, LANES).astype(F32), same_head, cos_t, sin_t,
                     s_len=s_len, tr=min(2048, s_len), first=off_qc // LANES)
        tk_c = min(512, s_len // 2)
        vt = _c_vprep(proj, bsz=bsz, s_len=s_len, tk=tk_c)
        oc = _attn_c(qk, vt, bsz=bsz, s_len=s_len, tq=min(512, s_len), tk=tk_c)
        xf = _mix_out(xf, oa, ob, oc, _pair_heads(out_gain_a[l], 0), out_gain_b[l],
                      _pair_heads(out_gain_c[l], 0), w_o_p, l, tm=min(512, n))
        xf = _mlp(xf, norm_mlp[l], w_up_b, w_down_b, l, norm_final,
                  tm=min(512, n), tf=1024, final_norm=(l == depth - 1))
    return xf.reshape(bsz, s_len, d_model)
```

```python
import functools
import math

import numpy as np
import jax
import jax.numpy as jnp
from jax import lax
from jax.experimental import pallas as pl
from jax.experimental.pallas import tpu as pltpu

HEAD_DIM = 64
LANES = 128
WINDOW = 128
A_BLOCK = 128
T5_BUCKETS = 32
T5_MAX_DIST = 128
GRID_W = 64
NA_ROWS = 8
NA_COLS = 16
ROPE_THETA = 10000.0
EPS = 1e-6
MASK_VALUE = -1e30
LOG2E = math.log2(math.e)
VMEM_LIMIT = 56 * 1024 * 1024

F32 = jnp.float32
BF16 = jnp.bfloat16


def _params(*sem):
    return pltpu.CompilerParams(dimension_semantics=sem, vmem_limit_bytes=VMEM_LIMIT)


def _rms_scale(x):
    return lax.rsqrt(jnp.mean(x * x, axis=-1, keepdims=True) + EPS)


def _lo_lanes():
    return lax.broadcasted_iota(jnp.int32, (1, LANES), 1) < HEAD_DIM


def _inproj_kernel(x_ref, g_ref, w_ref, o_ref, h_ref):
    @pl.when(pl.program_id(1) == 0)
    def _():
        x = x_ref[...]
        h_ref[...] = (x * _rms_scale(x) * g_ref[...]).astype(BF16)

    o_ref[...] = jnp.dot(h_ref[...], w_ref[...], preferred_element_type=F32).astype(o_ref.dtype)


def _inproj(x, gain, w, layer, *, tm, tn):
    n, d = x.shape
    e = w.shape[2]
    return pl.pallas_call(
        _inproj_kernel,
        grid=(n // tm, e // tn),
        in_specs=[pl.BlockSpec((tm, d), lambda i, j: (i, 0)),
                  pl.BlockSpec((1, d), lambda i, j: (0, 0)),
                  pl.BlockSpec((None, d, tn), lambda i, j: (layer, 0, j))],
        out_specs=pl.BlockSpec((tm, tn), lambda i, j: (i, j)),
        out_shape=jax.ShapeDtypeStruct((n, e), BF16),
        scratch_shapes=[pltpu.VMEM((tm, d), BF16)],
        compiler_params=_params("parallel", "arbitrary"),
        name="inproj",
    )(x, gain.reshape(1, d), w)


def _vt_prep_kernel(x_ref, o_ref):
    for t in range(o_ref.shape[0]):
        o_ref[t] = x_ref[t * LANES:(t + 1) * LANES, :].astype(F32).T.astype(o_ref.dtype)


def _vt_prep(proj, first, count, *, bsz, s_len, tr):
    nt = s_len // tr
    return pl.pallas_call(
        _vt_prep_kernel,
        grid=(bsz, count, nt),
        in_specs=[pl.BlockSpec((tr, LANES), lambda b, c, i: (b * nt + i, first + c))],
        out_specs=pl.BlockSpec((None, None, tr // LANES, LANES, LANES), lambda b, c, i: (b, c, i, 0, 0)),
        out_shape=jax.ShapeDtypeStruct((bsz, count, s_len // LANES, LANES, LANES), BF16),
        compiler_params=_params("parallel", "parallel", "arbitrary"),
        name="vt_prep",
    )(proj)


def _attn_a_kernel(q_ref, k_ref, vt_ref, bias_ref, sink_ref, o_ref, *, tq, s_len):
    i = pl.program_id(1)
    lo = _lo_lanes()
    first_rows = lax.broadcasted_iota(jnp.int32, (LANES, 1), 0) < HEAD_DIM
    kj = lax.broadcasted_iota(jnp.int32, (3 * A_BLOCK, A_BLOCK), 0)
    qi = lax.broadcasted_iota(jnp.int32, (3 * A_BLOCK, A_BLOCK), 1)
    in_window = jnp.abs(kj - A_BLOCK - qi) <= WINDOW
    n_blocks = s_len // A_BLOCK

    def key_blocks(j):
        blk = i * (tq // A_BLOCK) + j
        return [jnp.clip(blk + o, 0, n_blocks - 1) for o in (-1, 0, 1)]

    def scores(j, half):
        keep = lo if half == 0 else jnp.logical_not(lo)
        rows = slice(j * A_BLOCK, (j + 1) * A_BLOCK)
        qs = jnp.concatenate(
            [jnp.where(keep, q_ref[rows, r * LANES:(r + 1) * LANES], jnp.zeros((), BF16)) for r in range(4)],
            axis=0)
        k3 = jnp.concatenate([k_ref[pl.ds(pl.multiple_of(b * A_BLOCK, A_BLOCK), A_BLOCK), :]
                              for b in key_blocks(j)], axis=0)
        return lax.dot_general(k3, qs, (((1,), (1,)), ((), ())), preferred_element_type=F32)

    def finish(j, half, s):
        key_pos = (i * (tq // A_BLOCK) + j - 1) * A_BLOCK + kj
        valid = in_window & (key_pos >= 0) & (key_pos < s_len)
        heads = [slice(r * A_BLOCK, (r + 1) * A_BLOCK) for r in range(4)]
        s = jnp.concatenate([jnp.where(valid, s[:, c] + bias_ref[half, :, c], MASK_VALUE) for c in heads], axis=1)
        sink = sink_ref[half]
        m = jnp.maximum(jnp.max(s, axis=0, keepdims=True), sink)
        p = jnp.exp2(s - m).astype(BF16)
        v3t = jnp.concatenate([vt_ref[b] for b in key_blocks(j)], axis=1)
        ones = jnp.ones((), BF16)
        lhs = jnp.where(first_rows, v3t, ones) if half == 0 else jnp.where(first_rows, ones, v3t)
        acc = jnp.dot(lhs, p, preferred_element_type=F32)
        num, den = (acc[:HEAD_DIM], acc[HEAD_DIM:HEAD_DIM + 1]) if half == 0 else (acc[HEAD_DIM:], acc[:1])
        return num * (1.0 / (den + jnp.exp2(sink - m)))

    units = [(j, half) for j in range(tq // A_BLOCK) for half in (0, 1)]
    ahead = 2
    pending = [scores(*u) for u in units[:ahead]]
    lo_half = None
    for n, (j, half) in enumerate(units):
        s_cur = pending.pop(0)
        if n + ahead < len(units):
            pending.append(scores(*units[n + ahead]))
        out = finish(j, half, s_cur)
        if half == 0:
            lo_half = out
        else:
            rows = slice(j * A_BLOCK, (j + 1) * A_BLOCK)
            for r in range(4):
                cols = slice(r * A_BLOCK, (r + 1) * A_BLOCK)
                o_ref[rows, r * LANES:(r + 1) * LANES] = (
                    jnp.concatenate([lo_half[:, cols], out[:, cols]], axis=0).T.astype(o_ref.dtype))


def _attn_a(proj, vt, bias_t, sink_t, *, bsz, s_len, tq, k_block):
    n = proj.shape[0]
    nq = s_len // tq
    kern = functools.partial(_attn_a_kernel, tq=tq, s_len=s_len)
    return pl.pallas_call(
        kern,
        grid=(bsz, nq),
        in_specs=[pl.BlockSpec((tq, 4 * LANES), lambda b, i: (b * nq + i, 0)),
                  pl.BlockSpec((s_len, LANES), lambda b, i: (b, k_block)),
                  pl.BlockSpec((None, None, s_len // LANES, LANES, LANES), lambda b, i: (b, 0, 0, 0, 0)),
                  pl.BlockSpec(bias_t.shape, lambda b, i: (0, 0, 0)),
                  pl.BlockSpec(sink_t.shape, lambda b, i: (0, 0, 0))],
        out_specs=pl.BlockSpec((tq, 4 * LANES), lambda b, i: (b * nq + i, 0)),
        out_shape=jax.ShapeDtypeStruct((n, 4 * LANES), BF16),
        compiler_params=_params("parallel", "arbitrary"),
        name="attn_a",
    )(proj, proj, vt, bias_t, sink_t)


def _attn_b_kernel(q_ref, k_ref, v_ref, t_ref, o_ref, *, rblk, rows):
    r0 = pl.program_id(2) * rblk
    lo = _lo_lanes()
    nkeys = NA_ROWS * GRID_W
    cq = lax.broadcasted_iota(jnp.int32, (GRID_W, nkeys), 0)
    ck = lax.broadcasted_iota(jnp.int32, (GRID_W, nkeys), 1) % GRID_W
    cs = jnp.clip(cq - NA_COLS // 2, 0, GRID_W - NA_COLS)
    col_valid = (ck >= cs) & (ck < cs + NA_COLS)

    def first_key_row(a):
        return jnp.clip(r0 + a - NA_ROWS // 2, 0, rows - NA_ROWS)

    def window(ref, a):
        return ref[pl.ds(pl.multiple_of(first_key_row(a) * GRID_W, GRID_W), nkeys), :]

    def scores(a):
        qrow = q_ref[a * GRID_W:(a + 1) * GRID_W, :]
        zero = jnp.zeros_like(qrow)
        qs = jnp.concatenate([jnp.where(lo, qrow, zero), jnp.where(lo, zero, qrow)], axis=0)
        return lax.dot_general(qs, window(k_ref, a), (((1,), (1,)), ((), ())), preferred_element_type=F32)

    def finish(a, s):
        off = first_key_row(a) - (r0 + a) + (NA_ROWS - 1)
        bias = jnp.concatenate(
            [jnp.concatenate([t_ref[half, off + 2 * j] for j in range(NA_ROWS // 2)], axis=1)
             for half in (0, 1)], axis=0)
        s = s.reshape(2, GRID_W, nkeys) + bias.reshape(2, GRID_W, nkeys)
        s = jnp.where(col_valid[None], s, MASK_VALUE)
        m = jnp.max(s, axis=-1, keepdims=True)
        e = jnp.exp2(s - m)
        den = jnp.sum(e, axis=-1, keepdims=True)
        pv = jnp.dot(e.reshape(2 * GRID_W, nkeys).astype(BF16), window(v_ref, a),
                     preferred_element_type=F32)
        pv = pv.reshape(2, GRID_W, LANES) * (1.0 / den)
        o_ref[a * GRID_W:(a + 1) * GRID_W, :] = jnp.where(lo, pv[0], pv[1]).astype(o_ref.dtype)

    ahead = min(4, rblk)
    pending = [scores(a) for a in range(ahead)]
    for a in range(rblk):
        s_cur = pending.pop(0)
        if a + ahead < rblk:
            pending.append(scores(a + ahead))
        finish(a, s_cur)


def _attn_b(proj, t_pair, *, bsz, s_len, rblk, q_block):
    n = proj.shape[0]
    k_block, v_block = q_block + 4, q_block + 8
    rows = s_len // GRID_W
    nrb = rows // rblk
    kern = functools.partial(_attn_b_kernel, rblk=rblk, rows=rows)
    tq = rblk * GRID_W
    return pl.pallas_call(
        kern,
        grid=(bsz, 4, nrb),
        in_specs=[pl.BlockSpec((tq, LANES), lambda b, h, i: (b * nrb + i, q_block + h)),
                  pl.BlockSpec((s_len, LANES), lambda b, h, i: (b, k_block + h)),
                  pl.BlockSpec((s_len, LANES), lambda b, h, i: (b, v_block + h)),
                  pl.BlockSpec((2, 2 * NA_ROWS - 2, GRID_W, LANES), lambda b, h, i: (h, 0, 0, 0))],
        out_specs=pl.BlockSpec((tq, LANES), lambda b, h, i: (b * nrb + i, h)),
        out_shape=jax.ShapeDtypeStruct((n, 4 * LANES), BF16),
        compiler_params=_params("parallel", "parallel", "arbitrary"),
        name="attn_b",
    )(proj, proj, proj, t_pair)


def _c_prep_kernel(x_ref, g_ref, e_ref, cos_ref, sin_ref, o_ref):
    x = x_ref[...].astype(F32)
    xx = x * x
    xx_hi = xx.astype(BF16)
    xx_lo = (xx - xx_hi.astype(F32)).astype(BF16)
    ssq = jnp.dot(jnp.concatenate([xx_hi, xx_lo], axis=1), e_ref[...], preferred_element_type=F32)
    y = x * lax.rsqrt(ssq * (1.0 / HEAD_DIM) + EPS) * g_ref[0]
    o_ref[...] = (y * cos_ref[...] + pltpu.roll(y, LANES // 2, 1) * sin_ref[...]).astype(o_ref.dtype)


def _c_prep(proj, gains, same_head, cos_t, sin_t, *, s_len, tr, first):
    n = proj.shape[0]
    nblk = gains.shape[0]
    npos = s_len // tr
    pos_spec = pl.BlockSpec((tr, LANES), lambda i, j: (i % npos, 0))
    return pl.pallas_call(
        _c_prep_kernel,
        grid=(n // tr, nblk),
        in_specs=[pl.BlockSpec((tr, LANES), lambda i, j: (i, first + j)),
                  pl.BlockSpec((1, 1, LANES), lambda i, j: (j, 0, 0)),
                  pl.BlockSpec((2 * LANES, LANES), lambda i, j: (0, 0)),
                  pos_spec, pos_spec],
        out_specs=pl.BlockSpec((tr, LANES), lambda i, j: (i, j)),
        out_shape=jax.ShapeDtypeStruct((n, nblk * LANES), BF16),
        compiler_params=_params("parallel", "arbitrary"),
        name="c_prep",
    )(proj, gains, same_head, cos_t, sin_t)


VT_ROWS = 2 * HEAD_DIM


def _c_vprep_kernel(x_ref, o_ref):
    xt = x_ref[...].astype(F32).T
    ones = jnp.ones((VT_ROWS - HEAD_DIM, xt.shape[1]), o_ref.dtype)
    for half in (0, 1):
        o_ref[half * VT_ROWS:half * VT_ROWS + HEAD_DIM, :] = (
            xt[half * HEAD_DIM:(half + 1) * HEAD_DIM].astype(o_ref.dtype))
        o_ref[half * VT_ROWS + HEAD_DIM:(half + 1) * VT_ROWS, :] = ones


def _c_vprep(proj, *, bsz, s_len, tk):
    nk = s_len // tk
    v_first = proj.shape[1] // LANES - 2
    return pl.pallas_call(
        _c_vprep_kernel,
        grid=(bsz, 2, nk),
        in_specs=[pl.BlockSpec((tk, LANES), lambda b, p, i: (b * nk + i, v_first + p))],
        out_specs=pl.BlockSpec((None, None, None, 2 * VT_ROWS, tk), lambda b, p, i: (b, p, i, 0, 0)),
        out_shape=jax.ShapeDtypeStruct((bsz, 2, nk, 2 * VT_ROWS, tk), BF16),
        compiler_params=_params("parallel", "parallel", "arbitrary"),
        name="c_vprep",
    )(proj)


C_GROUP = 256


def _attn_c_kernel(q_ref, k_ref, vt_ref, o_ref, qs_ref, acc_ref, s_ref, *, tq, tk, s_len):
    lo = _first_head_lanes()
    per_blk = tq // C_GROUP
    n_groups = 8 * per_blk
    for r in range(4):
        for j in range(per_blk):
            qb = q_ref[j * C_GROUP:(j + 1) * C_GROUP, r * LANES:(r + 1) * LANES]
            zero = jnp.zeros_like(qb)
            qs_ref[r * per_blk + j] = jnp.where(lo, qb, zero)
            qs_ref[(4 + r) * per_blk + j] = jnp.where(lo, zero, qb)
    acc_ref[...] = jnp.zeros(acc_ref.shape, F32)

    nk = s_len // tk

    def scores(t, groups):
        kt = k_ref[pl.ds(pl.multiple_of(t * tk, tk), tk), :]
        for g in groups:
            s_ref[g] = lax.dot_general(kt, qs_ref[g], (((1,), (1,)), ((), ())),
                                       preferred_element_type=F32)

    ahead = n_groups // 4

    def step(t, t_next, m):
        m_out = []
        for g in range(n_groups):
            if g + ahead < n_groups:
                scores(t, (g + ahead,))
            else:
                scores(t_next, (g + ahead - n_groups,))
            st = s_ref[g]
            m_new = jnp.maximum(m[g], jnp.max(st, axis=0, keepdims=True))
            alpha = jnp.exp2(m[g] - m_new)
            pt = jnp.exp2(st - m_new).astype(BF16)
            half = 0 if g < n_groups // 2 else 1
            vt = vt_ref[t, half * VT_ROWS:(half + 1) * VT_ROWS, :]
            acc_ref[g] = alpha * acc_ref[g] + jnp.dot(vt, pt, preferred_element_type=F32)
            m_out.append(m_new)
        return tuple(m_out)

    scores(0, range(ahead))

    def two_steps(u, m):
        t0 = 2 * u
        m = step(t0, t0 + 1, m)
        return step(t0 + 1, jnp.minimum(t0 + 2, nk - 1), m)

    m0 = tuple(jnp.full((1, C_GROUP), -jnp.inf, F32) for _ in range(n_groups))
    lax.fori_loop(0, nk // 2, two_steps, m0)
    for r in range(4):
        for j in range(per_blk):
            a_lo = acc_ref[r * per_blk + j]
            a_hi = acc_ref[(4 + r) * per_blk + j]
            o_lo = a_lo[:HEAD_DIM] * (1.0 / a_lo[HEAD_DIM:HEAD_DIM + 1])
            o_hi = a_hi[:HEAD_DIM] * (1.0 / a_hi[HEAD_DIM:HEAD_DIM + 1])
            o_ref[j * C_GROUP:(j + 1) * C_GROUP, r * LANES:(r + 1) * LANES] = (
                jnp.concatenate([o_lo, o_hi], axis=0).T.astype(o_ref.dtype))


def _attn_c(qk, vt, *, bsz, s_len, tq, tk):
    n = qk.shape[0]
    nq = s_len // tq
    nk = s_len // tk
    assert tq % C_GROUP == 0 and nk % 2 == 0
    n_groups = 8 * tq // C_GROUP
    k_first = qk.shape[1] // LANES - 2
    kern = functools.partial(_attn_c_kernel, tq=tq, tk=tk, s_len=s_len)
    return pl.pallas_call(
        kern,
        grid=(bsz, 2, nq),
        in_specs=[pl.BlockSpec((tq, 4 * LANES), lambda b, p, i: (b * nq + i, p)),
                  pl.BlockSpec((s_len, LANES), lambda b, p, i: (b, k_first + p)),
                  pl.BlockSpec((None, None, nk, 2 * VT_ROWS, tk), lambda b, p, i: (b, p, 0, 0, 0))],
        out_specs=pl.BlockSpec((tq, 4 * LANES), lambda b, p, i: (b * nq + i, p)),
        out_shape=jax.ShapeDtypeStruct((n, 8 * LANES), BF16),
        scratch_shapes=[pltpu.VMEM((n_groups, C_GROUP, LANES), BF16),
                        pltpu.VMEM((n_groups, VT_ROWS, C_GROUP), F32),
                        pltpu.VMEM((n_groups, tk, C_GROUP), F32)],
        compiler_params=_params("parallel", "parallel", "arbitrary"),
        name="attn_c",
    )(qk, qk, vt)


def _mix_out_kernel(x_ref, oa_ref, ob_ref, oc_ref, ga_ref, gb_ref, gc_ref, w_ref, o_ref):
    parts = []
    for o_r, g_r in ((oa_ref, ga_ref), (ob_ref, gb_ref), (oc_ref, gc_ref)):
        o = o_r[...].astype(F32)
        parts.append((o * _rms_scale(o) * g_r[...]).astype(BF16))
    mix = jnp.concatenate(parts, axis=-1)
    o_ref[...] = x_ref[...] + jnp.dot(mix, w_ref[...], preferred_element_type=F32)


def _mix_out(x, oa, ob, oc, ga, gb, gc, w, layer, *, tm):
    n, d = x.shape
    row = lambda width: pl.BlockSpec((tm, width), lambda i: (i, 0))
    const = lambda shape: pl.BlockSpec(shape, lambda i: (0, 0))
    return pl.pallas_call(
        _mix_out_kernel,
        grid=(n // tm,),
        in_specs=[row(d), row(oa.shape[1]), row(ob.shape[1]), row(oc.shape[1]),
                  const((1, oa.shape[1])), const((1, ob.shape[1])), const((1, oc.shape[1])),
                  pl.BlockSpec((None,) + w.shape[1:], lambda i: (layer, 0, 0))],
        out_specs=row(d),
        out_shape=jax.ShapeDtypeStruct((n, d), F32),
        compiler_params=_params("parallel"),
        name="mix_out",
    )(x, oa, ob, oc, ga.reshape(1, -1), gb.reshape(1, -1), gc.reshape(1, -1), w)


def _mlp_kernel(x_ref, g_ref, wu_ref, wd_ref, gf_ref, o_ref, h_ref, *, final_norm):
    f = pl.program_id(1)

    @pl.when(f == 0)
    def _():
        x = x_ref[...]
        h_ref[...] = (x * _rms_scale(x) * g_ref[...]).astype(BF16)
        o_ref[...] = x

    u = jnp.maximum(jnp.dot(h_ref[...], wu_ref[...], preferred_element_type=F32), 0.0)
    o_ref[...] += jnp.dot((u * u).astype(BF16), wd_ref[...], preferred_element_type=F32)

    if final_norm:
        @pl.when(f == pl.num_programs(1) - 1)
        def _():
            y = o_ref[...]
            o_ref[...] = y * _rms_scale(y) * gf_ref[...]


def _mlp(x, gain, w_up, w_down, layer, final_gain, *, tm, tf, final_norm):
    n, d = x.shape
    ff = w_up.shape[2]
    kern = functools.partial(_mlp_kernel, final_norm=final_norm)
    return pl.pallas_call(
        kern,
        grid=(n // tm, ff // tf),
        in_specs=[pl.BlockSpec((tm, d), lambda i, f: (i, 0)),
                  pl.BlockSpec((1, d), lambda i, f: (0, 0)),
                  pl.BlockSpec((None, d, tf), lambda i, f: (layer, 0, f)),
                  pl.BlockSpec((None, tf, d), lambda i, f: (layer, f, 0)),
                  pl.BlockSpec((1, d), lambda i, f: (0, 0))],
        out_specs=pl.BlockSpec((tm, d), lambda i, f: (i, 0)),
        out_shape=jax.ShapeDtypeStruct((n, d), F32),
        scratch_shapes=[pltpu.VMEM((tm, d), BF16)],
        compiler_params=_params("parallel", "arbitrary"),
        name="mlp",
    )(x, gain.reshape(1, d), w_up, w_down, final_gain.reshape(1, d))


def _t5_bucket_np(rel):
    nb = T5_BUCKETS // 2
    max_exact = nb // 2
    base = np.where(rel > 0, nb, 0)
    n = np.abs(rel)
    nf = np.maximum(n, 1).astype(np.float32)
    large = max_exact + (np.log(nf / np.float32(max_exact)) / np.float32(math.log(T5_MAX_DIST / max_exact))
                         * np.float32(nb - max_exact)).astype(np.int32)
    large = np.minimum(large, nb - 1)
    return base + np.where(n < max_exact, n, large)


def _pair_heads(a, axis):
    shape = a.shape
    halves = shape[axis] // (8 * HEAD_DIM)
    a = a.reshape(shape[:axis] + (halves, 2, 4, HEAD_DIM) + shape[axis + 1:])
    return jnp.swapaxes(a, axis + 1, axis + 2).reshape(shape)


def _split_rotary(a, axis):
    shape = a.shape
    a = a.reshape(shape[:axis] + (shape[axis] // LANES, 2, 2, 2, HEAD_DIM // 4) + shape[axis + 1:])
    perm = (tuple(range(axis + 1)) + (axis + 3, axis + 1, axis + 2, axis + 4)
            + tuple(range(axis + 5, a.ndim)))
    return jnp.transpose(a, perm).reshape(shape)


def _first_head_lanes():
    lane = lax.broadcasted_iota(jnp.int32, (1, LANES), 1)
    return (lane // (HEAD_DIM // 2)) % 2 == 0


def _rope_tables(s_len):
    axis_dim = HEAD_DIM // 2
    t = jnp.arange(s_len)
    row = (t // GRID_W).astype(F32)
    col = (t % GRID_W).astype(F32)
    freqs = ROPE_THETA ** (-jnp.arange(0, axis_dim, 2, dtype=F32) / axis_dim)
    ang = jnp.concatenate([row[:, None] * freqs[None, :], col[:, None] * freqs[None, :]], axis=-1)
    ang = jnp.tile(ang, (1, 4))
    sign = np.where(np.arange(LANES) < LANES // 2, -1.0, 1.0).astype(np.float32)
    return jnp.cos(ang), jnp.sin(ang) * sign


def kernel(x, norm_mix, w_in, a_sink, t5_table, b_rpb, c_q_gain, c_k_gain, out_gain_a, out_gain_b,
           out_gain_c, w_o, norm_mlp, w_up, w_down, norm_final):
    bsz, s_len, d_model = x.shape
    depth = w_in.shape[0]
    n = bsz * s_len
    scale = HEAD_DIM ** -0.5

    a_w = out_gain_a.shape[1]
    b_w = out_gain_b.shape[1]
    c_w = out_gain_c.shape[1]
    a_kv_w = a_w // 4
    c_kv_w = c_w // 4
    off_qb = a_w + 2 * a_kv_w
    off_qc = off_qb + 3 * b_w
    in_width = off_qc + c_w + 2 * c_kv_w
    assert in_width == w_in.shape[2] and a_w == 8 * HEAD_DIM and c_w == 16 * HEAD_DIM

    off_vc = off_qc + c_w + c_kv_w
    w_in_p = jnp.concatenate(
        [_pair_heads(w_in[..., :a_w], 2) * (scale * LOG2E), w_in[..., a_w:off_qb],
         w_in[..., off_qb:off_qb + b_w] * (scale * LOG2E), w_in[..., off_qb + b_w:off_qc],
         _split_rotary(_pair_heads(w_in[..., off_qc:off_qc + c_w], 2), 2),
         _split_rotary(w_in[..., off_qc + c_w:off_vc], 2), w_in[..., off_vc:]], axis=2).astype(BF16)
    w_o_p = jnp.concatenate(
        [_pair_heads(w_o[:, :a_w], 1), w_o[:, a_w:a_w + b_w], _pair_heads(w_o[:, a_w + b_w:], 1)],
        axis=1).astype(BF16)
    w_up_b = w_up.astype(BF16)
    w_down_b = w_down.astype(BF16)

    qi = np.arange(A_BLOCK)[:, None]
    kj = np.arange(3 * A_BLOCK)[None, :]
    bucket = _t5_bucket_np(kj - A_BLOCK - qi)
    bias_a = t5_table[bucket].astype(F32) * LOG2E
    bias_t = bias_a.reshape(A_BLOCK, 3 * A_BLOCK, 2, 4).transpose(2, 1, 3, 0)
    bias_t = bias_t.reshape(2, 3 * A_BLOCK, 4 * A_BLOCK)
    sink_t = jnp.repeat(a_sink.astype(F32) * LOG2E, A_BLOCK, axis=1).reshape(depth, 2, 1, 4 * A_BLOCK)

    cq = np.arange(GRID_W)[:, None]
    ck = np.arange(GRID_W)[None, :]
    dc = np.clip(ck - cq + NA_COLS - 1, 0, 2 * NA_COLS - 2)
    t_blocks = b_rpb[:, :, :, dc].astype(F32) * LOG2E
    t_pair = jnp.concatenate([t_blocks[:, :, :-1], t_blocks[:, :, 1:]], axis=-1)

    cos_t, sin_t = _rope_tables(s_len)
    two = lambda g: _split_rotary(jnp.concatenate([g, g], axis=-1), 0)
    n_qblk, n_kblk = c_w // LANES, c_kv_w // LANES
    head_of_lane = (np.arange(LANES) // (HEAD_DIM // 2)) % 2
    same_head = jnp.asarray(np.tile(head_of_lane[:, None] == head_of_lane[None, :], (2, 1)), BF16)

    xf = x.reshape(n, d_model)
    for l in range(depth):
        proj = _inproj(xf, norm_mix[l], w_in_p, l, tm=min(1024, n), tn=in_width // 3)
        vt_a = _vt_prep(proj, a_w // LANES + 1, 1, bsz=bsz, s_len=s_len, tr=min(1024, s_len))
        oa = _attn_a(proj, vt_a, bias_t, sink_t[l], bsz=bsz, s_len=s_len, tq=min(512, s_len),
                     k_block=a_w // LANES)
        ob = _attn_b(proj, t_pair[l], bsz=bsz, s_len=s_len, rblk=8, q_block=off_qb // LANES)
        gains = jnp.concatenate([jnp.tile(two(c_q_gain[l] * (scale * LOG2E))[None], (n_qblk, 1)),
                                 jnp.tile(two(c_k_gain[l])[None], (n_kblk, 1))], axis=0)
        qk = _c_prep(proj, gains.reshape(n_qblk + n_kblk, 1, LANES).astype(F32), same_head, cos_t, sin_t,
                     s_len=s_len, tr=min(2048, s_len), first=off_qc // LANES)
        tk_c = min(512, s_len // 2)
        vt = _c_vprep(proj, bsz=bsz, s_len=s_len, tk=tk_c)
        oc = _attn_c(qk, vt, bsz=bsz, s_len=s_len, tq=min(1024, s_len), tk=tk_c)
        xf = _mix_out(xf, oa, ob, oc, _pair_heads(out_gain_a[l], 0), out_gain_b[l],
                      _pair_heads(out_gain_c[l], 0), w_o_p, l, tm=min(512, n))
        xf = _mlp(xf, norm_mlp[l], w_up_b, w_down_b, l, norm_final,
                  tm=min(512, n), tf=1024, final_norm=(l == depth - 1))
    return xf.reshape(bsz, s_len, d_model)
```

```python
import functools
import math

import numpy as np
import jax
import jax.numpy as jnp
from jax import lax
from jax.experimental import pallas as pl
from jax.experimental.pallas import tpu as pltpu

HEAD_DIM = 64
LANES = 128
WINDOW = 128
A_BLOCK = 128
T5_BUCKETS = 32
T5_MAX_DIST = 128
GRID_W = 64
NA_ROWS = 8
NA_COLS = 16
ROPE_THETA = 10000.0
EPS = 1e-6
MASK_VALUE = -1e30
LOG2E = math.log2(math.e)
VMEM_LIMIT = 56 * 1024 * 1024

F32 = jnp.float32
BF16 = jnp.bfloat16


def _params(*sem):
    return pltpu.CompilerParams(dimension_semantics=sem, vmem_limit_bytes=VMEM_LIMIT)


def _rms_scale(x):
    return lax.rsqrt(jnp.mean(x * x, axis=-1, keepdims=True) + EPS)


def _lo_lanes():
    return lax.broadcasted_iota(jnp.int32, (1, LANES), 1) < HEAD_DIM


def _inproj_kernel(x_ref, g_ref, w_ref, o_ref, h_ref):
    @pl.when(pl.program_id(1) == 0)
    def _():
        x = x_ref[...]
        h_ref[...] = (x * _rms_scale(x) * g_ref[...]).astype(BF16)

    o_ref[...] = jnp.dot(h_ref[...], w_ref[...], preferred_element_type=F32).astype(o_ref.dtype)


def _inproj(x, gain, w, layer, *, tm, tn):
    n, d = x.shape
    e = w.shape[2]
    return pl.pallas_call(
        _inproj_kernel,
        grid=(n // tm, e // tn),
        in_specs=[pl.BlockSpec((tm, d), lambda i, j: (i, 0)),
                  pl.BlockSpec((1, d), lambda i, j: (0, 0)),
                  pl.BlockSpec((None, d, tn), lambda i, j: (layer, 0, j))],
        out_specs=pl.BlockSpec((tm, tn), lambda i, j: (i, j)),
        out_shape=jax.ShapeDtypeStruct((n, e), BF16),
        scratch_shapes=[pltpu.VMEM((tm, d), BF16)],
        compiler_params=_params("parallel", "arbitrary"),
        name="inproj",
    )(x, gain.reshape(1, d), w)


def _vt_prep_kernel(x_ref, o_ref):
    for t in range(o_ref.shape[0]):
        o_ref[t] = x_ref[t * LANES:(t + 1) * LANES, :].astype(F32).T.astype(o_ref.dtype)


def _vt_prep(proj, first, count, *, bsz, s_len, tr):
    nt = s_len // tr
    return pl.pallas_call(
        _vt_prep_kernel,
        grid=(bsz, count, nt),
        in_specs=[pl.BlockSpec((tr, LANES), lambda b, c, i: (b * nt + i, first + c))],
        out_specs=pl.BlockSpec((None, None, tr // LANES, LANES, LANES), lambda b, c, i: (b, c, i, 0, 0)),
        out_shape=jax.ShapeDtypeStruct((bsz, count, s_len // LANES, LANES, LANES), BF16),
        compiler_params=_params("parallel", "parallel", "arbitrary"),
        name="vt_prep",
    )(proj)


def _attn_a_kernel(q_ref, k_ref, vt_ref, bias_ref, sink_ref, o_ref, *, tq, s_len):
    i = pl.program_id(1)
    lo = _lo_lanes()
    first_rows = lax.broadcasted_iota(jnp.int32, (LANES, 1), 0) < HEAD_DIM
    kj = lax.broadcasted_iota(jnp.int32, (3 * A_BLOCK, A_BLOCK), 0)
    qi = lax.broadcasted_iota(jnp.int32, (3 * A_BLOCK, A_BLOCK), 1)
    in_window = jnp.abs(kj - A_BLOCK - qi) <= WINDOW
    n_blocks = s_len // A_BLOCK

    def key_blocks(j):
        blk = i * (tq // A_BLOCK) + j
        return [jnp.clip(blk + o, 0, n_blocks - 1) for o in (-1, 0, 1)]

    def scores(j, half):
        keep = lo if half == 0 else jnp.logical_not(lo)
        rows = slice(j * A_BLOCK, (j + 1) * A_BLOCK)
        qs = jnp.concatenate(
            [jnp.where(keep, q_ref[rows, r * LANES:(r + 1) * LANES], jnp.zeros((), BF16)) for r in range(4)],
            axis=0)
        k3 = jnp.concatenate([k_ref[pl.ds(pl.multiple_of(b * A_BLOCK, A_BLOCK), A_BLOCK), :]
                              for b in key_blocks(j)], axis=0)
        return lax.dot_general(k3, qs, (((1,), (1,)), ((), ())), preferred_element_type=F32)

    def finish(j, half, s):
        key_pos = (i * (tq // A_BLOCK) + j - 1) * A_BLOCK + kj
        valid = in_window & (key_pos >= 0) & (key_pos < s_len)
        heads = [slice(r * A_BLOCK, (r + 1) * A_BLOCK) for r in range(4)]
        s = jnp.concatenate([jnp.where(valid, s[:, c] + bias_ref[half, :, c], MASK_VALUE) for c in heads], axis=1)
        sink = sink_ref[half]
        m = jnp.maximum(jnp.max(s, axis=0, keepdims=True), sink)
        p = jnp.exp2(s - m).astype(BF16)
        v3t = jnp.concatenate([vt_ref[b] for b in key_blocks(j)], axis=1)
        ones = jnp.ones((), BF16)
        lhs = jnp.where(first_rows, v3t, ones) if half == 0 else jnp.where(first_rows, ones, v3t)
        acc = jnp.dot(lhs, p, preferred_element_type=F32)
        num, den = (acc[:HEAD_DIM], acc[HEAD_DIM:HEAD_DIM + 1]) if half == 0 else (acc[HEAD_DIM:], acc[:1])
        return num * (1.0 / (den + jnp.exp2(sink - m)))

    units = [(j, half) for j in range(tq // A_BLOCK) for half in (0, 1)]
    ahead = 2
    pending = [scores(*u) for u in units[:ahead]]
    lo_half = None
    for n, (j, half) in enumerate(units):
        s_cur = pending.pop(0)
        if n + ahead < len(units):
            pending.append(scores(*units[n + ahead]))
        out = finish(j, half, s_cur)
        if half == 0:
            lo_half = out
        else:
            rows = slice(j * A_BLOCK, (j + 1) * A_BLOCK)
            for r in range(4):
                cols = slice(r * A_BLOCK, (r + 1) * A_BLOCK)
                o_ref[rows, r * LANES:(r + 1) * LANES] = (
                    jnp.concatenate([lo_half[:, cols], out[:, cols]], axis=0).T.astype(o_ref.dtype))


def _attn_a(proj, vt, bias_t, sink_t, *, bsz, s_len, tq, k_block):
    n = proj.shape[0]
    nq = s_len // tq
    kern = functools.partial(_attn_a_kernel, tq=tq, s_len=s_len)
    return pl.pallas_call(
        kern,
        grid=(bsz, nq),
        in_specs=[pl.BlockSpec((tq, 4 * LANES), lambda b, i: (b * nq + i, 0)),
                  pl.BlockSpec((s_len, LANES), lambda b, i: (b, k_block)),
                  pl.BlockSpec((None, None, s_len // LANES, LANES, LANES), lambda b, i: (b, 0, 0, 0, 0)),
                  pl.BlockSpec(bias_t.shape, lambda b, i: (0, 0, 0)),
                  pl.BlockSpec(sink_t.shape, lambda b, i: (0, 0, 0))],
        out_specs=pl.BlockSpec((tq, 4 * LANES), lambda b, i: (b * nq + i, 0)),
        out_shape=jax.ShapeDtypeStruct((n, 4 * LANES), BF16),
        compiler_params=_params("parallel", "arbitrary"),
        name="attn_a",
    )(proj, proj, vt, bias_t, sink_t)


def _attn_b_kernel(q_ref, k_ref, v_ref, t_ref, o_ref, *, rblk, rows):
    r0 = pl.program_id(2) * rblk
    lo = _lo_lanes()
    nkeys = NA_ROWS * GRID_W
    cq = lax.broadcasted_iota(jnp.int32, (GRID_W, nkeys), 0)
    ck = lax.broadcasted_iota(jnp.int32, (GRID_W, nkeys), 1) % GRID_W
    cs = jnp.clip(cq - NA_COLS // 2, 0, GRID_W - NA_COLS)
    col_valid = (ck >= cs) & (ck < cs + NA_COLS)

    def first_key_row(a):
        return jnp.clip(r0 + a - NA_ROWS // 2, 0, rows - NA_ROWS)

    def window(ref, a):
        return ref[pl.ds(pl.multiple_of(first_key_row(a) * GRID_W, GRID_W), nkeys), :]

    def scores(a):
        qrow = q_ref[a * GRID_W:(a + 1) * GRID_W, :]
        zero = jnp.zeros_like(qrow)
        qs = jnp.concatenate([jnp.where(lo, qrow, zero), jnp.where(lo, zero, qrow)], axis=0)
        return lax.dot_general(qs, window(k_ref, a), (((1,), (1,)), ((), ())), preferred_element_type=F32)

    def finish(a, s):
        off = first_key_row(a) - (r0 + a) + (NA_ROWS - 1)
        bias = jnp.concatenate(
            [jnp.concatenate([t_ref[half, off + 2 * j] for j in range(NA_ROWS // 2)], axis=1)
             for half in (0, 1)], axis=0)
        s = s.reshape(2, GRID_W, nkeys) + bias.reshape(2, GRID_W, nkeys)
        s = jnp.where(col_valid[None], s, MASK_VALUE)
        m = jnp.max(s, axis=-1, keepdims=True)
        e = jnp.exp2(s - m)
        den = jnp.sum(e, axis=-1, keepdims=True)
        pv = jnp.dot(e.reshape(2 * GRID_W, nkeys).astype(BF16), window(v_ref, a),
                     preferred_element_type=F32)
        pv = pv.reshape(2, GRID_W, LANES) * (1.0 / den)
        o_ref[a * GRID_W:(a + 1) * GRID_W, :] = jnp.where(lo, pv[0], pv[1]).astype(o_ref.dtype)

    ahead = min(4, rblk)
    pending = [scores(a) for a in range(ahead)]
    for a in range(rblk):
        s_cur = pending.pop(0)
        if a + ahead < rblk:
            pending.append(scores(a + ahead))
        finish(a, s_cur)


def _attn_b(proj, t_pair, *, bsz, s_len, rblk, q_block):
    n = proj.shape[0]
    k_block, v_block = q_block + 4, q_block + 8
    rows = s_len // GRID_W
    nrb = rows // rblk
    kern = functools.partial(_attn_b_kernel, rblk=rblk, rows=rows)
    tq = rblk * GRID_W
    return pl.pallas_call(
        kern,
        grid=(bsz, 4, nrb),
        in_specs=[pl.BlockSpec((tq, LANES), lambda b, h, i: (b * nrb + i, q_block + h)),
                  pl.BlockSpec((s_len, LANES), lambda b, h, i: (b, k_block + h)),
                  pl.BlockSpec((s_len, LANES), lambda b, h, i: (b, v_block + h)),
                  pl.BlockSpec((2, 2 * NA_ROWS - 2, GRID_W, LANES), lambda b, h, i: (h, 0, 0, 0))],
        out_specs=pl.BlockSpec((tq, LANES), lambda b, h, i: (b * nrb + i, h)),
        out_shape=jax.ShapeDtypeStruct((n, 4 * LANES), BF16),
        compiler_params=_params("parallel", "parallel", "arbitrary"),
        name="attn_b",
    )(proj, proj, proj, t_pair)


def _c_prep_kernel(x_ref, g_ref, e_ref, cos_ref, sin_ref, o_ref):
    x = x_ref[...].astype(F32)
    xx = x * x
    xx_hi = xx.astype(BF16)
    xx_lo = (xx - xx_hi.astype(F32)).astype(BF16)
    ssq = jnp.dot(jnp.concatenate([xx_hi, xx_lo], axis=1), e_ref[...], preferred_element_type=F32)
    y = x * lax.rsqrt(ssq * (1.0 / HEAD_DIM) + EPS) * g_ref[0]
    o_ref[...] = (y * cos_ref[...] + pltpu.roll(y, LANES // 2, 1) * sin_ref[...]).astype(o_ref.dtype)


def _c_prep(proj, gains, same_head, cos_t, sin_t, *, s_len, tr, first):
    n = proj.shape[0]
    nblk = gains.shape[0]
    npos = s_len // tr
    pos_spec = pl.BlockSpec((tr, LANES), lambda i, j: (i % npos, 0))
    return pl.pallas_call(
        _c_prep_kernel,
        grid=(n // tr, nblk),
        in_specs=[pl.BlockSpec((tr, LANES), lambda i, j: (i, first + j)),
                  pl.BlockSpec((1, 1, LANES), lambda i, j: (j, 0, 0)),
                  pl.BlockSpec((2 * LANES, LANES), lambda i, j: (0, 0)),
                  pos_spec, pos_spec],
        out_specs=pl.BlockSpec((tr, LANES), lambda i, j: (i, j)),
        out_shape=jax.ShapeDtypeStruct((n, nblk * LANES), BF16),
        compiler_params=_params("parallel", "arbitrary"),
        name="c_prep",
    )(proj, gains, same_head, cos_t, sin_t)


VT_ROWS = 2 * HEAD_DIM


def _c_vprep_kernel(x_ref, o_ref):
    xt = x_ref[...].astype(F32).T
    ones = jnp.ones((VT_ROWS - HEAD_DIM, xt.shape[1]), o_ref.dtype)
    for half in (0, 1):
        o_ref[half * VT_ROWS:half * VT_ROWS + HEAD_DIM, :] = (
            xt[half * HEAD_DIM:(half + 1) * HEAD_DIM].astype(o_ref.dtype))
        o_ref[half * VT_ROWS + HEAD_DIM:(half + 1) * VT_ROWS, :] = ones


def _c_vprep(proj, *, bsz, s_len, tk):
    nk = s_len // tk
    v_first = proj.shape[1] // LANES - 2
    return pl.pallas_call(
        _c_vprep_kernel,
        grid=(bsz, 2, nk),
        in_specs=[pl.BlockSpec((tk, LANES), lambda b, p, i: (b * nk + i, v_first + p))],
        out_specs=pl.BlockSpec((None, None, None, 2 * VT_ROWS, tk), lambda b, p, i: (b, p, i, 0, 0)),
        out_shape=jax.ShapeDtypeStruct((bsz, 2, nk, 2 * VT_ROWS, tk), BF16),
        compiler_params=_params("parallel", "parallel", "arbitrary"),
        name="c_vprep",
    )(proj)


C_GROUP = 256


def _attn_c_kernel(q_ref, k_ref, vt_ref, o_ref, qs_ref, acc_ref, s_ref, *, tq, tk, s_len):
    lo = _first_head_lanes()
    per_blk = tq // C_GROUP
    n_groups = 8 * per_blk
    for r in range(4):
        for j in range(per_blk):
            qb = q_ref[j * C_GROUP:(j + 1) * C_GROUP, r * LANES:(r + 1) * LANES]
            zero = jnp.zeros_like(qb)
            qs_ref[r * per_blk + j] = jnp.where(lo, qb, zero)
            qs_ref[(4 + r) * per_blk + j] = jnp.where(lo, zero, qb)
    acc_ref[...] = jnp.zeros(acc_ref.shape, F32)

    nk = s_len // tk

    def scores(t, groups):
        kt = k_ref[pl.ds(pl.multiple_of(t * tk, tk), tk), :]
        for g in groups:
            s_ref[g] = lax.dot_general(kt, qs_ref[g], (((1,), (1,)), ((), ())),
                                       preferred_element_type=F32)

    ahead = n_groups // 4

    def step(t, t_next, m):
        m_out = []
        for g in range(n_groups):
            if g + ahead < n_groups:
                scores(t, (g + ahead,))
            else:
                scores(t_next, (g + ahead - n_groups,))
            st = s_ref[g]
            m_new = jnp.maximum(m[g], jnp.max(st, axis=0, keepdims=True))
            alpha = jnp.exp2(m[g] - m_new)
            pt = jnp.exp2(st - m_new).astype(BF16)
            half = 0 if g < n_groups // 2 else 1
            vt = vt_ref[t, half * VT_ROWS:(half + 1) * VT_ROWS, :]
            acc_ref[g] = alpha * acc_ref[g] + jnp.dot(vt, pt, preferred_element_type=F32)
            m_out.append(m_new)
        return tuple(m_out)

    scores(0, range(ahead))

    def two_steps(u, m):
        t0 = 2 * u
        m = step(t0, t0 + 1, m)
        return step(t0 + 1, jnp.minimum(t0 + 2, nk - 1), m)

    m0 = tuple(jnp.full((1, C_GROUP), -jnp.inf, F32) for _ in range(n_groups))
    lax.fori_loop(0, nk // 2, two_steps, m0)
    for r in range(4):
        for j in range(per_blk):
            a_lo = acc_ref[r * per_blk + j]
            a_hi = acc_ref[(4 + r) * per_blk + j]
            o_lo = a_lo[:HEAD_DIM] * (1.0 / a_lo[HEAD_DIM:HEAD_DIM + 1])
            o_hi = a_hi[:HEAD_DIM] * (1.0 / a_hi[HEAD_DIM:HEAD_DIM + 1])
            o_ref[j * C_GROUP:(j + 1) * C_GROUP, r * LANES:(r + 1) * LANES] = (
                jnp.concatenate([o_lo, o_hi], axis=0).T.astype(o_ref.dtype))


def _attn_c(qk, vt, *, bsz, s_len, tq, tk):
    n = qk.shape[0]
    nq = s_len // tq
    nk = s_len // tk
    assert tq % C_GROUP == 0 and nk % 2 == 0
    n_groups = 8 * tq // C_GROUP
    k_first = qk.shape[1] // LANES - 2
    kern = functools.partial(_attn_c_kernel, tq=tq, tk=tk, s_len=s_len)
    return pl.pallas_call(
        kern,
        grid=(bsz, 2, nq),
        in_specs=[pl.BlockSpec((tq, 4 * LANES), lambda b, p, i: (b * nq + i, p)),
                  pl.BlockSpec((s_len, LANES), lambda b, p, i: (b, k_first + p)),
                  pl.BlockSpec((None, None, nk, 2 * VT_ROWS, tk), lambda b, p, i: (b, p, 0, 0, 0))],
        out_specs=pl.BlockSpec((tq, 4 * LANES), lambda b, p, i: (b * nq + i, p)),
        out_shape=jax.ShapeDtypeStruct((n, 8 * LANES), BF16),
        scratch_shapes=[pltpu.VMEM((n_groups, C_GROUP, LANES), BF16),
                        pltpu.VMEM((n_groups, VT_ROWS, C_GROUP), F32),
                        pltpu.VMEM((n_groups, tk, C_GROUP), F32)],
        compiler_params=_params("parallel", "parallel", "arbitrary"),
        name="attn_c",
    )(qk, qk, vt)


def _mix_out_kernel(x_ref, oa_ref, ob_ref, oc_ref, ga_ref, gb_ref, gc_ref, w_ref, o_ref):
    parts = []
    for o_r, g_r in ((oa_ref, ga_ref), (ob_ref, gb_ref), (oc_ref, gc_ref)):
        o = o_r[...].astype(F32)
        parts.append((o * _rms_scale(o) * g_r[...]).astype(BF16))
    mix = jnp.concatenate(parts, axis=-1)
    o_ref[...] = x_ref[...] + jnp.dot(mix, w_ref[...], preferred_element_type=F32)


def _mix_out(x, oa, ob, oc, ga, gb, gc, w, layer, *, tm):
    n, d = x.shape
    row = lambda width: pl.BlockSpec((tm, width), lambda i: (i, 0))
    const = lambda shape: pl.BlockSpec(shape, lambda i: (0, 0))
    return pl.pallas_call(
        _mix_out_kernel,
        grid=(n // tm,),
        in_specs=[row(d), row(oa.shape[1]), row(ob.shape[1]), row(oc.shape[1]),
                  const((1, oa.shape[1])), const((1, ob.shape[1])), const((1, oc.shape[1])),
                  pl.BlockSpec((None,) + w.shape[1:], lambda i: (layer, 0, 0))],
        out_specs=row(d),
        out_shape=jax.ShapeDtypeStruct((n, d), F32),
        compiler_params=_params("parallel"),
        name="mix_out",
    )(x, oa, ob, oc, ga.reshape(1, -1), gb.reshape(1, -1), gc.reshape(1, -1), w)


def _mlp_kernel(x_ref, g_ref, wu_ref, wd_ref, gf_ref, o_ref, h_ref, *, final_norm):
    f = pl.program_id(1)

    @pl.when(f == 0)
    def _():
        x = x_ref[...]
        h_ref[...] = (x * _rms_scale(x) * g_ref[...]).astype(BF16)
        o_ref[...] = x

    u = jnp.maximum(jnp.dot(h_ref[...], wu_ref[...], preferred_element_type=F32), 0.0)
    o_ref[...] += jnp.dot((u * u).astype(BF16), wd_ref[...], preferred_element_type=F32)

    if final_norm:
        @pl.when(f == pl.num_programs(1) - 1)
        def _():
            y = o_ref[...]
            o_ref[...] = y * _rms_scale(y) * gf_ref[...]


def _mlp(x, gain, w_up, w_down, layer, final_gain, *, tm, tf, final_norm):
    n, d = x.shape
    ff = w_up.shape[2]
    kern = functools.partial(_mlp_kernel, final_norm=final_norm)
    return pl.pallas_call(
        kern,
        grid=(n // tm, ff // tf),
        in_specs=[pl.BlockSpec((tm, d), lambda i, f: (i, 0)),
                  pl.BlockSpec((1, d), lambda i, f: (0, 0)),
                  pl.BlockSpec((None, d, tf), lambda i, f: (layer, 0, f)),
                  pl.BlockSpec((None, tf, d), lambda i, f: (layer, f, 0)),
                  pl.BlockSpec((1, d), lambda i, f: (0, 0))],
        out_specs=pl.BlockSpec((tm, d), lambda i, f: (i, 0)),
        out_shape=jax.ShapeDtypeStruct((n, d), F32),
        scratch_shapes=[pltpu.VMEM((tm, d), BF16)],
        compiler_params=_params("parallel", "arbitrary"),
        name="mlp",
    )(x, gain.reshape(1, d), w_up, w_down, final_gain.reshape(1, d))


def _t5_bucket_np(rel):
    nb = T5_BUCKETS // 2
    max_exact = nb // 2
    base = np.where(rel > 0, nb, 0)
    n = np.abs(rel)
    nf = np.maximum(n, 1).astype(np.float32)
    large = max_exact + (np.log(nf / np.float32(max_exact)) / np.float32(math.log(T5_MAX_DIST / max_exact))
                         * np.float32(nb - max_exact)).astype(np.int32)
    large = np.minimum(large, nb - 1)
    return base + np.where(n < max_exact, n, large)


def _pair_heads(a, axis):
    shape = a.shape
    halves = shape[axis] // (8 * HEAD_DIM)
    a = a.reshape(shape[:axis] + (halves, 2, 4, HEAD_DIM) + shape[axis + 1:])
    return jnp.swapaxes(a, axis + 1, axis + 2).reshape(shape)


def _split_rotary(a, axis):
    shape = a.shape
    a = a.reshape(shape[:axis] + (shape[axis] // LANES, 2, 2, 2, HEAD_DIM // 4) + shape[axis + 1:])
    perm = (tuple(range(axis + 1)) + (axis + 3, axis + 1, axis + 2, axis + 4)
            + tuple(range(axis + 5, a.ndim)))
    return jnp.transpose(a, perm).reshape(shape)


def _first_head_lanes():
    lane = lax.broadcasted_iota(jnp.int32, (1, LANES), 1)
    return (lane // (HEAD_DIM // 2)) % 2 == 0


def _rope_tables(s_len):
    axis_dim = HEAD_DIM // 2
    t = jnp.arange(s_len)
    row = (t // GRID_W).astype(F32)
    col = (t % GRID_W).astype(F32)
    freqs = ROPE_THETA ** (-jnp.arange(0, axis_dim, 2, dtype=F32) / axis_dim)
    ang = jnp.concatenate([row[:, None] * freqs[None, :], col[:, None] * freqs[None, :]], axis=-1)
    sign = np.where(np.arange(LANES) < LANES // 2, -1.0, 1.0).astype(np.float32)
    return jnp.tile(jnp.cos(ang), (1, 4)), jnp.tile(jnp.sin(ang), (1, 4)) * sign


def kernel(x, norm_mix, w_in, a_sink, t5_table, b_rpb, c_q_gain, c_k_gain, out_gain_a, out_gain_b,
           out_gain_c, w_o, norm_mlp, w_up, w_down, norm_final):
    bsz, s_len, d_model = x.shape
    depth = w_in.shape[0]
    n = bsz * s_len
    scale = HEAD_DIM ** -0.5

    a_w = out_gain_a.shape[1]
    b_w = out_gain_b.shape[1]
    c_w = out_gain_c.shape[1]
    a_kv_w = a_w // 4
    c_kv_w = c_w // 4
    off_qb = a_w + 2 * a_kv_w
    off_qc = off_qb + 3 * b_w
    in_width = off_qc + c_w + 2 * c_kv_w
    assert in_width == w_in.shape[2] and a_w == 8 * HEAD_DIM and c_w == 16 * HEAD_DIM

    off_vc = off_qc + c_w + c_kv_w
    w_in_p = jnp.concatenate(
        [_pair_heads(w_in[..., :a_w], 2) * (scale * LOG2E), w_in[..., a_w:off_qb],
         w_in[..., off_qb:off_qb + b_w] * (scale * LOG2E), w_in[..., off_qb + b_w:off_qc],
         _split_rotary(_pair_heads(w_in[..., off_qc:off_qc + c_w], 2), 2),
         _split_rotary(w_in[..., off_qc + c_w:off_vc], 2), w_in[..., off_vc:]], axis=2).astype(BF16)
    w_o_p = jnp.concatenate(
        [_pair_heads(w_o[:, :a_w], 1), w_o[:, a_w:a_w + b_w], _pair_heads(w_o[:, a_w + b_w:], 1)],
        axis=1).astype(BF16)
    w_up_b = w_up.astype(BF16)
    w_down_b = w_down.astype(BF16)

    qi = np.arange(A_BLOCK)[:, None]
    kj = np.arange(3 * A_BLOCK)[None, :]
    bucket = _t5_bucket_np(kj - A_BLOCK - qi)
    bias_a = t5_table[bucket].astype(F32) * LOG2E
    bias_t = bias_a.reshape(A_BLOCK, 3 * A_BLOCK, 2, 4).transpose(2, 1, 3, 0)
    bias_t = bias_t.reshape(2, 3 * A_BLOCK, 4 * A_BLOCK)
    sink_t = jnp.repeat(a_sink.astype(F32) * LOG2E, A_BLOCK, axis=1).reshape(depth, 2, 1, 4 * A_BLOCK)

    cq = np.arange(GRID_W)[:, None]
    ck = np.arange(GRID_W)[None, :]
    dc = np.clip(ck - cq + NA_COLS - 1, 0, 2 * NA_COLS - 2)
    t_blocks = b_rpb[:, :, :, dc].astype(F32) * LOG2E
    t_pair = jnp.concatenate([t_blocks[:, :, :-1], t_blocks[:, :, 1:]], axis=-1)

    cos_t, sin_t = _rope_tables(s_len)
    two = lambda g: _split_rotary(jnp.concatenate([g, g], axis=-1), 0)
    n_qblk, n_kblk = c_w // LANES, c_kv_w // LANES
    head_of_lane = (np.arange(LANES) // (HEAD_DIM // 2)) % 2
    same_head = jnp.asarray(np.tile(head_of_lane[:, None] == head_of_lane[None, :], (2, 1)), BF16)

    xf = x.reshape(n, d_model)
    for l in range(depth):
        proj = _inproj(xf, norm_mix[l], w_in_p, l, tm=min(1024, n), tn=in_width // 3)
        vt_a = _vt_prep(proj, a_w // LANES + 1, 1, bsz=bsz, s_len=s_len, tr=min(1024, s_len))
        oa = _attn_a(proj, vt_a, bias_t, sink_t[l], bsz=bsz, s_len=s_len, tq=min(1024, s_len),
                     k_block=a_w // LANES)
        ob = _attn_b(proj, t_pair[l], bsz=bsz, s_len=s_len, rblk=32, q_block=off_qb // LANES)
        gains = jnp.concatenate([jnp.tile(two(c_q_gain[l] * (scale * LOG2E))[None], (n_qblk, 1)),
                                 jnp.tile(two(c_k_gain[l])[None], (n_kblk, 1))], axis=0)
        qk = _c_prep(proj, gains.reshape(n_qblk + n_kblk, 1, LANES).astype(F32), same_head, cos_t, sin_t,
                     s_len=s_len, tr=min(4096, s_len), first=off_qc // LANES)
        tk_c = min(512, s_len // 2)
        vt = _c_vprep(proj, bsz=bsz, s_len=s_len, tk=tk_c)
        oc = _attn_c(qk, vt, bsz=bsz, s_len=s_len, tq=min(1024, s_len), tk=tk_c)
        xf = _mix_out(xf, oa, ob, oc, _pair_heads(out_gain_a[l], 0), out_gain_b[l],
                      _pair_heads(out_gain_c[l], 0), w_o_p, l, tm=min(512, n))
        xf = _mlp(xf, norm_mlp[l], w_up_b, w_down_b, l, norm_final,
                  tm=min(512, n), tf=1024, final_norm=(l == depth - 1))
    return xf.reshape(bsz, s_len, d_model)
```

```python
import functools
import math

import numpy as np
import jax
import jax.numpy as jnp
from jax import lax
from jax.experimental import pallas as pl
from jax.experimental.pallas import tpu as pltpu

HEAD_DIM = 64
LANES = 128
WINDOW = 128
A_BLOCK = 128
T5_BUCKETS = 32
T5_MAX_DIST = 128
GRID_W = 64
NA_ROWS = 8
NA_COLS = 16
ROPE_THETA = 10000.0
EPS = 1e-6
MASK_VALUE = -1e30
LOG2E = math.log2(math.e)
V7X_VMEM_BYTES = 64 * 1024 * 1024
VMEM_LIMIT = V7X_VMEM_BYTES * 7 // 8

F32 = jnp.float32
BF16 = jnp.bfloat16


def _params(*sem):
    return pltpu.CompilerParams(dimension_semantics=sem, vmem_limit_bytes=VMEM_LIMIT)


def _rms_scale(x):
    return lax.rsqrt(jnp.mean(x * x, axis=-1, keepdims=True) + EPS)


def _lo_lanes():
    return lax.broadcasted_iota(jnp.int32, (1, LANES), 1) < HEAD_DIM


def _inproj_kernel(x_ref, g_ref, w_ref, o_ref, h_ref):
    @pl.when(pl.program_id(1) == 0)
    def _():
        x = x_ref[...]
        h_ref[...] = (x * _rms_scale(x) * g_ref[...]).astype(BF16)

    o_ref[...] = jnp.dot(h_ref[...], w_ref[...], preferred_element_type=F32).astype(o_ref.dtype)


def _inproj(x, gain, w, layer, *, tm, tn):
    n, d = x.shape
    e = w.shape[2]
    assert n % tm == 0 and e % tn == 0
    return pl.pallas_call(
        _inproj_kernel,
        grid=(n // tm, e // tn),
        in_specs=[pl.BlockSpec((tm, d), lambda i, j: (i, 0)),
                  pl.BlockSpec((1, d), lambda i, j: (0, 0)),
                  pl.BlockSpec((None, d, tn), lambda i, j: (layer, 0, j))],
        out_specs=pl.BlockSpec((tm, tn), lambda i, j: (i, j)),
        out_shape=jax.ShapeDtypeStruct((n, e), BF16),
        scratch_shapes=[pltpu.VMEM((tm, d), BF16)],
        compiler_params=_params("parallel", "arbitrary"),
        name="inproj",
    )(x, gain.reshape(1, d), w)


def _vt_prep_kernel(x_ref, o_ref):
    for t in range(o_ref.shape[0]):
        o_ref[t] = x_ref[t * LANES:(t + 1) * LANES, :].astype(F32).T.astype(o_ref.dtype)


def _vt_prep(proj, first, count, *, bsz, s_len, tr):
    nt = s_len // tr
    assert s_len % tr == 0 and tr % LANES == 0
    return pl.pallas_call(
        _vt_prep_kernel,
        grid=(bsz, count, nt),
        in_specs=[pl.BlockSpec((tr, LANES), lambda b, c, i: (b * nt + i, first + c))],
        out_specs=pl.BlockSpec((None, None, tr // LANES, LANES, LANES), lambda b, c, i: (b, c, i, 0, 0)),
        out_shape=jax.ShapeDtypeStruct((bsz, count, s_len // LANES, LANES, LANES), BF16),
        compiler_params=_params("parallel", "parallel", "arbitrary"),
        name="vt_prep",
    )(proj)


def _attn_a_kernel(q_ref, k_ref, vt_ref, bias_ref, sink_ref, o_ref, *, tq, s_len):
    i = pl.program_id(1)
    lo = _lo_lanes()
    first_rows = lax.broadcasted_iota(jnp.int32, (LANES, 1), 0) < HEAD_DIM
    kj = lax.broadcasted_iota(jnp.int32, (3 * A_BLOCK, A_BLOCK), 0)
    qi = lax.broadcasted_iota(jnp.int32, (3 * A_BLOCK, A_BLOCK), 1)
    in_window = jnp.abs(kj - A_BLOCK - qi) <= WINDOW
    n_blocks = s_len // A_BLOCK

    def key_blocks(j):
        blk = i * (tq // A_BLOCK) + j
        return [jnp.clip(blk + o, 0, n_blocks - 1) for o in (-1, 0, 1)]

    def scores(j, half):
        keep = lo if half == 0 else jnp.logical_not(lo)
        rows = slice(j * A_BLOCK, (j + 1) * A_BLOCK)
        qs = jnp.concatenate(
            [jnp.where(keep, q_ref[rows, r * LANES:(r + 1) * LANES], jnp.zeros((), BF16)) for r in range(4)],
            axis=0)
        k3 = jnp.concatenate([k_ref[pl.ds(pl.multiple_of(b * A_BLOCK, A_BLOCK), A_BLOCK), :]
                              for b in key_blocks(j)], axis=0)
        return lax.dot_general(k3, qs, (((1,), (1,)), ((), ())), preferred_element_type=F32)

    def finish(j, half, s):
        key_pos = (i * (tq // A_BLOCK) + j - 1) * A_BLOCK + kj
        valid = in_window & (key_pos >= 0) & (key_pos < s_len)
        heads = [slice(r * A_BLOCK, (r + 1) * A_BLOCK) for r in range(4)]
        s = jnp.concatenate([jnp.where(valid, s[:, c] + bias_ref[half, :, c], MASK_VALUE) for c in heads], axis=1)
        sink = sink_ref[half]
        m = jnp.maximum(jnp.max(s, axis=0, keepdims=True), sink)
        p = jnp.exp2(s - m).astype(BF16)
        v3t = jnp.concatenate([vt_ref[b] for b in key_blocks(j)], axis=1)
        ones = jnp.ones((), BF16)
        lhs = jnp.where(first_rows, v3t, ones) if half == 0 else jnp.where(first_rows, ones, v3t)
        acc = jnp.dot(lhs, p, preferred_element_type=F32)
        num, den = (acc[:HEAD_DIM], acc[HEAD_DIM:HEAD_DIM + 1]) if half == 0 else (acc[HEAD_DIM:], acc[:1])
        return num * (1.0 / (den + jnp.exp2(sink - m)))

    units = [(j, half) for j in range(tq // A_BLOCK) for half in (0, 1)]
    ahead = 2
    pending = [scores(*u) for u in units[:ahead]]
    lo_half = None
    for n, (j, half) in enumerate(units):
        s_cur = pending.pop(0)
        if n + ahead < len(units):
            pending.append(scores(*units[n + ahead]))
        out = finish(j, half, s_cur)
        if half == 0:
            lo_half = out
        else:
            rows = slice(j * A_BLOCK, (j + 1) * A_BLOCK)
            for r in range(4):
                cols = slice(r * A_BLOCK, (r + 1) * A_BLOCK)
                o_ref[rows, r * LANES:(r + 1) * LANES] = (
                    jnp.concatenate([lo_half[:, cols], out[:, cols]], axis=0).T.astype(o_ref.dtype))


def _attn_a(proj, vt, bias_t, sink_t, *, bsz, s_len, tq, k_block):
    n = proj.shape[0]
    nq = s_len // tq
    assert s_len % tq == 0 and tq % A_BLOCK == 0
    kern = functools.partial(_attn_a_kernel, tq=tq, s_len=s_len)
    return pl.pallas_call(
        kern,
        grid=(bsz, nq),
        in_specs=[pl.BlockSpec((tq, 4 * LANES), lambda b, i: (b * nq + i, 0)),
                  pl.BlockSpec((s_len, LANES), lambda b, i: (b, k_block)),
                  pl.BlockSpec((None, None, s_len // LANES, LANES, LANES), lambda b, i: (b, 0, 0, 0, 0)),
                  pl.BlockSpec(bias_t.shape, lambda b, i: (0, 0, 0)),
                  pl.BlockSpec(sink_t.shape, lambda b, i: (0, 0, 0))],
        out_specs=pl.BlockSpec((tq, 4 * LANES), lambda b, i: (b * nq + i, 0)),
        out_shape=jax.ShapeDtypeStruct((n, 4 * LANES), BF16),
        compiler_params=_params("parallel", "arbitrary"),
        name="attn_a",
    )(proj, proj, vt, bias_t, sink_t)


def _attn_b_kernel(q_ref, k_ref, v_ref, t_ref, o_ref, *, rblk, rows):
    r0 = pl.program_id(2) * rblk
    lo = _lo_lanes()
    nkeys = NA_ROWS * GRID_W
    cq = lax.broadcasted_iota(jnp.int32, (GRID_W, nkeys), 0)
    ck = lax.broadcasted_iota(jnp.int32, (GRID_W, nkeys), 1) % GRID_W
    cs = jnp.clip(cq - NA_COLS // 2, 0, GRID_W - NA_COLS)
    col_valid = (ck >= cs) & (ck < cs + NA_COLS)

    def first_key_row(a):
        return jnp.clip(r0 + a - NA_ROWS // 2, 0, rows - NA_ROWS)

    def window(ref, a):
        return ref[pl.ds(pl.multiple_of(first_key_row(a) * GRID_W, GRID_W), nkeys), :]

    def scores(a):
        qrow = q_ref[a * GRID_W:(a + 1) * GRID_W, :]
        zero = jnp.zeros_like(qrow)
        qs = jnp.concatenate([jnp.where(lo, qrow, zero), jnp.where(lo, zero, qrow)], axis=0)
        return lax.dot_general(qs, window(k_ref, a), (((1,), (1,)), ((), ())), preferred_element_type=F32)

    def finish(a, s):
        off = first_key_row(a) - (r0 + a) + (NA_ROWS - 1)
        bias = jnp.concatenate(
            [jnp.concatenate([t_ref[half, off + 2 * j] for j in range(NA_ROWS // 2)], axis=1)
             for half in (0, 1)], axis=0)
        s = s.reshape(2, GRID_W, nkeys) + bias.reshape(2, GRID_W, nkeys)
        s = jnp.where(col_valid[None], s, MASK_VALUE)
        m = jnp.max(s, axis=-1, keepdims=True)
        e = jnp.exp2(s - m)
        den = jnp.sum(e, axis=-1, keepdims=True)
        pv = jnp.dot(e.reshape(2 * GRID_W, nkeys).astype(BF16), window(v_ref, a),
                     preferred_element_type=F32)
        pv = pv.reshape(2, GRID_W, LANES) * (1.0 / den)
        o_ref[a * GRID_W:(a + 1) * GRID_W, :] = jnp.where(lo, pv[0], pv[1]).astype(o_ref.dtype)

    ahead = min(4, rblk)
    pending = [scores(a) for a in range(ahead)]
    for a in range(rblk):
        s_cur = pending.pop(0)
        if a + ahead < rblk:
            pending.append(scores(a + ahead))
        finish(a, s_cur)


def _attn_b(proj, t_pair, *, bsz, s_len, rblk, q_block):
    n = proj.shape[0]
    k_block, v_block = q_block + 4, q_block + 8
    rows = s_len // GRID_W
    nrb = rows // rblk
    assert s_len % GRID_W == 0 and rows % rblk == 0 and rows >= NA_ROWS
    kern = functools.partial(_attn_b_kernel, rblk=rblk, rows=rows)
    tq = rblk * GRID_W
    return pl.pallas_call(
        kern,
        grid=(bsz, 4, nrb),
        in_specs=[pl.BlockSpec((tq, LANES), lambda b, h, i: (b * nrb + i, q_block + h)),
                  pl.BlockSpec((s_len, LANES), lambda b, h, i: (b, k_block + h)),
                  pl.BlockSpec((s_len, LANES), lambda b, h, i: (b, v_block + h)),
                  pl.BlockSpec((2, 2 * NA_ROWS - 2, GRID_W, LANES), lambda b, h, i: (h, 0, 0, 0))],
        out_specs=pl.BlockSpec((tq, LANES), lambda b, h, i: (b * nrb + i, h)),
        out_shape=jax.ShapeDtypeStruct((n, 4 * LANES), BF16),
        compiler_params=_params("parallel", "parallel", "arbitrary"),
        name="attn_b",
    )(proj, proj, proj, t_pair)


def _c_prep_kernel(x_ref, g_ref, e_ref, cos_ref, sin_ref, o_ref):
    x = x_ref[...].astype(F32)
    xx = x * x
    xx_hi = xx.astype(BF16)
    xx_lo = (xx - xx_hi.astype(F32)).astype(BF16)
    ssq = jnp.dot(jnp.concatenate([xx_hi, xx_lo], axis=1), e_ref[...], preferred_element_type=F32)
    y = x * lax.rsqrt(ssq * (1.0 / HEAD_DIM) + EPS) * g_ref[0]
    o_ref[...] = (y * cos_ref[...] + pltpu.roll(y, LANES // 2, 1) * sin_ref[...]).astype(o_ref.dtype)


def _c_prep(proj, gains, same_head, cos_t, sin_t, *, s_len, tr, first):
    n = proj.shape[0]
    nblk = gains.shape[0]
    npos = s_len // tr
    assert s_len % tr == 0
    pos_spec = pl.BlockSpec((tr, LANES), lambda i, j: (i % npos, 0))
    return pl.pallas_call(
        _c_prep_kernel,
        grid=(n // tr, nblk),
        in_specs=[pl.BlockSpec((tr, LANES), lambda i, j: (i, first + j)),
                  pl.BlockSpec((1, 1, LANES), lambda i, j: (j, 0, 0)),
                  pl.BlockSpec((2 * LANES, LANES), lambda i, j: (0, 0)),
                  pos_spec, pos_spec],
        out_specs=pl.BlockSpec((tr, LANES), lambda i, j: (i, j)),
        out_shape=jax.ShapeDtypeStruct((n, nblk * LANES), BF16),
        compiler_params=_params("parallel", "arbitrary"),
        name="c_prep",
    )(proj, gains, same_head, cos_t, sin_t)


VT_ROWS = 2 * HEAD_DIM


def _c_vprep_kernel(x_ref, o_ref):
    tiles, _, tk = o_ref.shape
    ones = jnp.ones((VT_ROWS - HEAD_DIM, tk), o_ref.dtype)
    for t in range(tiles):
        xt = x_ref[t * tk:(t + 1) * tk, :].astype(F32).T
        for half in (0, 1):
            o_ref[t, half * VT_ROWS:half * VT_ROWS + HEAD_DIM, :] = (
                xt[half * HEAD_DIM:(half + 1) * HEAD_DIM].astype(o_ref.dtype))
            o_ref[t, half * VT_ROWS + HEAD_DIM:(half + 1) * VT_ROWS, :] = ones


def _c_vprep(proj, *, bsz, s_len, tk):
    nk = s_len // tk
    tiles = min(4, nk)
    steps = nk // tiles
    assert s_len % tk == 0 and nk % tiles == 0
    v_first = proj.shape[1] // LANES - 2
    return pl.pallas_call(
        _c_vprep_kernel,
        grid=(bsz, 2, steps),
        in_specs=[pl.BlockSpec((tiles * tk, LANES), lambda b, p, i: (b * steps + i, v_first + p))],
        out_specs=pl.BlockSpec((None, None, tiles, 2 * VT_ROWS, tk), lambda b, p, i: (b, p, i, 0, 0)),
        out_shape=jax.ShapeDtypeStruct((bsz, 2, nk, 2 * VT_ROWS, tk), BF16),
        compiler_params=_params("parallel", "parallel", "arbitrary"),
        name="c_vprep",
    )(proj)


C_GROUP = 256


def _attn_c_kernel(q_ref, k_ref, vt_ref, o_ref, qs_ref, acc_ref, s_ref, *, tq, tk, s_len):
    lo = _first_head_lanes()
    per_blk = tq // C_GROUP
    n_groups = 8 * per_blk
    for r in range(4):
        for j in range(per_blk):
            qb = q_ref[j * C_GROUP:(j + 1) * C_GROUP, r * LANES:(r + 1) * LANES]
            zero = jnp.zeros_like(qb)
            qs_ref[r * per_blk + j] = jnp.where(lo, qb, zero)
            qs_ref[(4 + r) * per_blk + j] = jnp.where(lo, zero, qb)
    acc_ref[...] = jnp.zeros(acc_ref.shape, F32)

    nk = s_len // tk

    def scores(t, groups):
        kt = k_ref[pl.ds(pl.multiple_of(t * tk, tk), tk), :]
        for g in groups:
            s_ref[g] = lax.dot_general(kt, qs_ref[g], (((1,), (1,)), ((), ())),
                                       preferred_element_type=F32)

    ahead = min(4, n_groups // 2)

    def step(t, t_next, m):
        m_out = []
        for g in range(n_groups):
            if g + ahead < n_groups:
                scores(t, (g + ahead,))
            else:
                scores(t_next, (g + ahead - n_groups,))
            st = s_ref[g]
            m_new = jnp.maximum(m[g], jnp.max(st, axis=0, keepdims=True))
            alpha = jnp.exp2(m[g] - m_new)
            pt = jnp.exp2(st - m_new).astype(BF16)
            half = 0 if g < n_groups // 2 else 1
            vt = vt_ref[t, half * VT_ROWS:(half + 1) * VT_ROWS, :]
            acc_ref[g] = alpha * acc_ref[g] + jnp.dot(vt, pt, preferred_element_type=F32)
            m_out.append(m_new)
        return tuple(m_out)

    scores(0, range(ahead))

    def two_steps(u, m):
        t0 = 2 * u
        m = step(t0, t0 + 1, m)
        return step(t0 + 1, jnp.minimum(t0 + 2, nk - 1), m)

    m0 = tuple(jnp.full((1, C_GROUP), -jnp.inf, F32) for _ in range(n_groups))
    lax.fori_loop(0, nk // 2, two_steps, m0)
    for r in range(4):
        for j in range(per_blk):
            a_lo = acc_ref[r * per_blk + j]
            a_hi = acc_ref[(4 + r) * per_blk + j]
            o_lo = a_lo[:HEAD_DIM] * (1.0 / a_lo[HEAD_DIM:HEAD_DIM + 1])
            o_hi = a_hi[:HEAD_DIM] * (1.0 / a_hi[HEAD_DIM:HEAD_DIM + 1])
            o_ref[j * C_GROUP:(j + 1) * C_GROUP, r * LANES:(r + 1) * LANES] = (
                jnp.concatenate([o_lo, o_hi], axis=0).T.astype(o_ref.dtype))


def _attn_c(qk, vt, *, bsz, s_len, tq, tk):
    n = qk.shape[0]
    nq = s_len // tq
    nk = s_len // tk
    assert s_len % tq == 0 and tq % C_GROUP == 0 and s_len % tk == 0 and nk % 2 == 0
    n_groups = 8 * tq // C_GROUP
    k_first = qk.shape[1] // LANES - 2
    kern = functools.partial(_attn_c_kernel, tq=tq, tk=tk, s_len=s_len)
    return pl.pallas_call(
        kern,
        grid=(bsz, 2, nq),
        in_specs=[pl.BlockSpec((tq, 4 * LANES), lambda b, p, i: (b * nq + i, p)),
                  pl.BlockSpec((s_len, LANES), lambda b, p, i: (b, k_first + p)),
                  pl.BlockSpec((None, None, nk, 2 * VT_ROWS, tk), lambda b, p, i: (b, p, 0, 0, 0))],
        out_specs=pl.BlockSpec((tq, 4 * LANES), lambda b, p, i: (b * nq + i, p)),
        out_shape=jax.ShapeDtypeStruct((n, 8 * LANES), BF16),
        scratch_shapes=[pltpu.VMEM((n_groups, C_GROUP, LANES), BF16),
                        pltpu.VMEM((n_groups, VT_ROWS, C_GROUP), F32),
                        pltpu.VMEM((n_groups, tk, C_GROUP), F32)],
        compiler_params=_params("parallel", "parallel", "arbitrary"),
        name="attn_c",
    )(qk, qk, vt)


def _mix_out_kernel(x_ref, oa_ref, ob_ref, oc_ref, ga_ref, gb_ref, gc_ref, w_ref, o_ref):
    parts = []
    for o_r, g_r in ((oa_ref, ga_ref), (ob_ref, gb_ref), (oc_ref, gc_ref)):
        o = o_r[...].astype(F32)
        parts.append((o * _rms_scale(o) * g_r[...]).astype(BF16))
    mix = jnp.concatenate(parts, axis=-1)
    o_ref[...] = x_ref[...] + jnp.dot(mix, w_ref[...], preferred_element_type=F32)


def _mix_out(x, oa, ob, oc, ga, gb, gc, w, layer, *, tm):
    n, d = x.shape
    assert n % tm == 0
    row = lambda width: pl.BlockSpec((tm, width), lambda i: (i, 0))
    const = lambda shape: pl.BlockSpec(shape, lambda i: (0, 0))
    return pl.pallas_call(
        _mix_out_kernel,
        grid=(n // tm,),
        in_specs=[row(d), row(oa.shape[1]), row(ob.shape[1]), row(oc.shape[1]),
                  const((1, oa.shape[1])), const((1, ob.shape[1])), const((1, oc.shape[1])),
                  pl.BlockSpec((None,) + w.shape[1:], lambda i: (layer, 0, 0))],
        out_specs=row(d),
        out_shape=jax.ShapeDtypeStruct((n, d), F32),
        compiler_params=_params("parallel"),
        name="mix_out",
    )(x, oa, ob, oc, ga.reshape(1, -1), gb.reshape(1, -1), gc.reshape(1, -1), w)


def _mlp_kernel(x_ref, g_ref, wu_ref, wd_ref, gf_ref, o_ref, h_ref, *, final_norm):
    f = pl.program_id(1)

    @pl.when(f == 0)
    def _():
        x = x_ref[...]
        h_ref[...] = (x * _rms_scale(x) * g_ref[...]).astype(BF16)
        o_ref[...] = x

    u = jnp.maximum(jnp.dot(h_ref[...], wu_ref[...], preferred_element_type=F32), 0.0)
    o_ref[...] += jnp.dot((u * u).astype(BF16), wd_ref[...], preferred_element_type=F32)

    if final_norm:
        @pl.when(f == pl.num_programs(1) - 1)
        def _():
            y = o_ref[...]
            o_ref[...] = y * _rms_scale(y) * gf_ref[...]


def _mlp(x, gain, w_up, w_down, layer, final_gain, *, tm, tf, final_norm):
    n, d = x.shape
    ff = w_up.shape[2]
    assert n % tm == 0 and ff % tf == 0
    kern = functools.partial(_mlp_kernel, final_norm=final_norm)
    return pl.pallas_call(
        kern,
        grid=(n // tm, ff // tf),
        in_specs=[pl.BlockSpec((tm, d), lambda i, f: (i, 0)),
                  pl.BlockSpec((1, d), lambda i, f: (0, 0)),
                  pl.BlockSpec((None, d, tf), lambda i, f: (layer, 0, f)),
                  pl.BlockSpec((None, tf, d), lambda i, f: (layer, f, 0)),
                  pl.BlockSpec((1, d), lambda i, f: (0, 0))],
        out_specs=pl.BlockSpec((tm, d), lambda i, f: (i, 0)),
        out_shape=jax.ShapeDtypeStruct((n, d), F32),
        scratch_shapes=[pltpu.VMEM((tm, d), BF16)],
        compiler_params=_params("parallel", "arbitrary"),
        name="mlp",
    )(x, gain.reshape(1, d), w_up, w_down, final_gain.reshape(1, d))


def _t5_bucket_np(rel):
    nb = T5_BUCKETS // 2
    max_exact = nb // 2
    base = np.where(rel > 0, nb, 0)
    n = np.abs(rel)
    nf = np.maximum(n, 1).astype(np.float32)
    large = max_exact + (np.log(nf / np.float32(max_exact)) / np.float32(math.log(T5_MAX_DIST / max_exact))
                         * np.float32(nb - max_exact)).astype(np.int32)
    large = np.minimum(large, nb - 1)
    return base + np.where(n < max_exact, n, large)


def _pair_heads(a, axis):
    shape = a.shape
    halves = shape[axis] // (8 * HEAD_DIM)
    a = a.reshape(shape[:axis] + (halves, 2, 4, HEAD_DIM) + shape[axis + 1:])
    return jnp.swapaxes(a, axis + 1, axis + 2).reshape(shape)


def _split_rotary(a, axis):
    shape = a.shape
    a = a.reshape(shape[:axis] + (shape[axis] // LANES, 2, 2, 2, HEAD_DIM // 4) + shape[axis + 1:])
    perm = (tuple(range(axis + 1)) + (axis + 3, axis + 1, axis + 2, axis + 4)
            + tuple(range(axis + 5, a.ndim)))
    return jnp.transpose(a, perm).reshape(shape)


def _first_head_lanes():
    lane = lax.broadcasted_iota(jnp.int32, (1, LANES), 1)
    return (lane // (HEAD_DIM // 2)) % 2 == 0


def _rope_tables(s_len):
    axis_dim = HEAD_DIM // 2
    t = jnp.arange(s_len)
    row = (t // GRID_W).astype(F32)
    col = (t % GRID_W).astype(F32)
    freqs = ROPE_THETA ** (-jnp.arange(0, axis_dim, 2, dtype=F32) / axis_dim)
    ang = jnp.concatenate([row[:, None] * freqs[None, :], col[:, None] * freqs[None, :]], axis=-1)
    sign = np.where(np.arange(LANES) < LANES // 2, -1.0, 1.0).astype(np.float32)
    return jnp.tile(jnp.cos(ang), (1, 4)), jnp.tile(jnp.sin(ang), (1, 4)) * sign


def kernel(x, norm_mix, w_in, a_sink, t5_table, b_rpb, c_q_gain, c_k_gain, out_gain_a, out_gain_b,
           out_gain_c, w_o, norm_mlp, w_up, w_down, norm_final):
    bsz, s_len, d_model = x.shape
    depth = w_in.shape[0]
    n = bsz * s_len
    scale = HEAD_DIM ** -0.5

    a_w = out_gain_a.shape[1]
    b_w = out_gain_b.shape[1]
    c_w = out_gain_c.shape[1]
    a_kv_w = a_w // 4
    c_kv_w = c_w // 4
    off_qb = a_w + 2 * a_kv_w
    off_qc = off_qb + 3 * b_w
    in_width = off_qc + c_w + 2 * c_kv_w
    assert in_width == w_in.shape[2] and a_w == 8 * HEAD_DIM and c_w == 16 * HEAD_DIM

    off_vc = off_qc + c_w + c_kv_w
    w_in_p = jnp.concatenate(
        [_pair_heads(w_in[..., :a_w], 2) * (scale * LOG2E), w_in[..., a_w:off_qb],
         w_in[..., off_qb:off_qb + b_w] * (scale * LOG2E), w_in[..., off_qb + b_w:off_qc],
         _split_rotary(_pair_heads(w_in[..., off_qc:off_qc + c_w], 2), 2),
         _split_rotary(w_in[..., off_qc + c_w:off_vc], 2), w_in[..., off_vc:]], axis=2).astype(BF16)
    w_o_p = jnp.concatenate(
        [_pair_heads(w_o[:, :a_w], 1), w_o[:, a_w:a_w + b_w], _pair_heads(w_o[:, a_w + b_w:], 1)],
        axis=1).astype(BF16)
    w_up_b = w_up.astype(BF16)
    w_down_b = w_down.astype(BF16)

    qi = np.arange(A_BLOCK)[:, None]
    kj = np.arange(3 * A_BLOCK)[None, :]
    bucket = _t5_bucket_np(kj - A_BLOCK - qi)
    bias_a = t5_table[bucket].astype(F32) * LOG2E
    bias_t = bias_a.reshape(A_BLOCK, 3 * A_BLOCK, 2, 4).transpose(2, 1, 3, 0)
    bias_t = bias_t.reshape(2, 3 * A_BLOCK, 4 * A_BLOCK)
    sink_t = jnp.repeat(a_sink.astype(F32) * LOG2E, A_BLOCK, axis=1).reshape(depth, 2, 1, 4 * A_BLOCK)

    cq = np.arange(GRID_W)[:, None]
    ck = np.arange(GRID_W)[None, :]
    dc = np.clip(ck - cq + NA_COLS - 1, 0, 2 * NA_COLS - 2)
    t_blocks = b_rpb[:, :, :, dc].astype(F32) * LOG2E
    t_pair = jnp.concatenate([t_blocks[:, :, :-1], t_blocks[:, :, 1:]], axis=-1)

    cos_t, sin_t = _rope_tables(s_len)
    two = lambda g: _split_rotary(jnp.concatenate([g, g], axis=-1), 0)
    n_qblk, n_kblk = c_w // LANES, c_kv_w // LANES
    head_of_lane = (np.arange(LANES) // (HEAD_DIM // 2)) % 2
    same_head = jnp.asarray(np.tile(head_of_lane[:, None] == head_of_lane[None, :], (2, 1)), BF16)

    xf = x.reshape(n, d_model)
    for l in range(depth):
        proj = _inproj(xf, norm_mix[l], w_in_p, l, tm=min(1024, n), tn=in_width // 3)
        vt_a = _vt_prep(proj, a_w // LANES + 1, 1, bsz=bsz, s_len=s_len, tr=min(1024, s_len))
        oa = _attn_a(proj, vt_a, bias_t, sink_t[l], bsz=bsz, s_len=s_len, tq=min(1024, s_len),
                     k_block=a_w // LANES)
        ob = _attn_b(proj, t_pair[l], bsz=bsz, s_len=s_len, rblk=min(32, s_len // GRID_W),
                     q_block=off_qb // LANES)
        gains = jnp.concatenate([jnp.tile(two(c_q_gain[l] * (scale * LOG2E))[None], (n_qblk, 1)),
                                 jnp.tile(two(c_k_gain[l])[None], (n_kblk, 1))], axis=0)
        qk = _c_prep(proj, gains.reshape(n_qblk + n_kblk, 1, LANES).astype(F32), same_head, cos_t, sin_t,
                     s_len=s_len, tr=min(4096, s_len), first=off_qc // LANES)
        tk_c = min(512, s_len // 2)
        vt = _c_vprep(proj, bsz=bsz, s_len=s_len, tk=tk_c)
        oc = _attn_c(qk, vt, bsz=bsz, s_len=s_len, tq=min(1024, s_len), tk=tk_c)
        xf = _mix_out(xf, oa, ob, oc, _pair_heads(out_gain_a[l], 0), out_gain_b[l],
                      _pair_heads(out_gain_c[l], 0), w_o_p, l, tm=min(512, n))
        xf = _mlp(xf, norm_mlp[l], w_up_b, w_down_b, l, norm_final,
                  tm=min(512, n), tf=1024, final_norm=(l == depth - 1))
    return xf.reshape(bsz, s_len, d_model)
```

```python
import functools
import math

import numpy as np
import jax
import jax.numpy as jnp
from jax import lax
from jax.experimental import pallas as pl
from jax.experimental.pallas import tpu as pltpu

HEAD_DIM = 64
LANES = 128
WINDOW = 128
A_BLOCK = 128
T5_BUCKETS = 32
T5_MAX_DIST = 128
GRID_W = 64
NA_ROWS = 8
NA_COLS = 16
ROPE_THETA = 10000.0
EPS = 1e-6
MASK_VALUE = -1e30
LOG2E = math.log2(math.e)
V7X_VMEM_BYTES = 64 * 1024 * 1024
VMEM_LIMIT = V7X_VMEM_BYTES * 7 // 8

F32 = jnp.float32
BF16 = jnp.bfloat16


def _params(*sem):
    return pltpu.CompilerParams(dimension_semantics=sem, vmem_limit_bytes=VMEM_LIMIT)


def _rms_scale(x):
    return lax.rsqrt(jnp.mean(x * x, axis=-1, keepdims=True) + EPS)


def _lo_lanes():
    return lax.broadcasted_iota(jnp.int32, (1, LANES), 1) < HEAD_DIM


def _inproj_kernel(x_ref, g_ref, w_ref, o_ref, h_ref):
    @pl.when(pl.program_id(1) == 0)
    def _():
        x = x_ref[...]
        h_ref[...] = (x * _rms_scale(x) * g_ref[...]).astype(BF16)

    o_ref[...] = jnp.dot(h_ref[...], w_ref[...], preferred_element_type=F32).astype(o_ref.dtype)


def _inproj(x, gain, w, layer, *, tm, tn):
    n, d = x.shape
    e = w.shape[2]
    assert n % tm == 0 and e % tn == 0
    return pl.pallas_call(
        _inproj_kernel,
        grid=(n // tm, e // tn),
        in_specs=[pl.BlockSpec((tm, d), lambda i, j: (i, 0)),
                  pl.BlockSpec((1, d), lambda i, j: (0, 0)),
                  pl.BlockSpec((None, d, tn), lambda i, j: (layer, 0, j))],
        out_specs=pl.BlockSpec((tm, tn), lambda i, j: (i, j)),
        out_shape=jax.ShapeDtypeStruct((n, e), BF16),
        scratch_shapes=[pltpu.VMEM((tm, d), BF16)],
        compiler_params=_params("parallel", "arbitrary"),
        name="inproj",
    )(x, gain.reshape(1, d), w)


def _vt_prep_kernel(x_ref, o_ref):
    for t in range(o_ref.shape[0]):
        o_ref[t] = x_ref[t * LANES:(t + 1) * LANES, :].astype(F32).T.astype(o_ref.dtype)


def _vt_prep(proj, first, count, *, bsz, s_len, tr):
    nt = s_len // tr
    assert s_len % tr == 0 and tr % LANES == 0
    return pl.pallas_call(
        _vt_prep_kernel,
        grid=(bsz, count, nt),
        in_specs=[pl.BlockSpec((tr, LANES), lambda b, c, i: (b * nt + i, first + c))],
        out_specs=pl.BlockSpec((None, None, tr // LANES, LANES, LANES), lambda b, c, i: (b, c, i, 0, 0)),
        out_shape=jax.ShapeDtypeStruct((bsz, count, s_len // LANES, LANES, LANES), BF16),
        compiler_params=_params("parallel", "parallel", "arbitrary"),
        name="vt_prep",
    )(proj)


def _attn_a_kernel(q_ref, k_ref, vt_ref, bias_ref, sink_ref, o_ref, *, tq, s_len):
    i = pl.program_id(1)
    lo = _lo_lanes()
    first_rows = lax.broadcasted_iota(jnp.int32, (LANES, 1), 0) < HEAD_DIM
    kj = lax.broadcasted_iota(jnp.int32, (3 * A_BLOCK, A_BLOCK), 0)
    qi = lax.broadcasted_iota(jnp.int32, (3 * A_BLOCK, A_BLOCK), 1)
    in_window = jnp.abs(kj - A_BLOCK - qi) <= WINDOW
    n_blocks = s_len // A_BLOCK

    def key_blocks(j):
        blk = i * (tq // A_BLOCK) + j
        return [jnp.clip(blk + o, 0, n_blocks - 1) for o in (-1, 0, 1)]

    def scores(j, half):
        keep = lo if half == 0 else jnp.logical_not(lo)
        rows = slice(j * A_BLOCK, (j + 1) * A_BLOCK)
        qs = jnp.concatenate(
            [jnp.where(keep, q_ref[rows, r * LANES:(r + 1) * LANES], jnp.zeros((), BF16)) for r in range(4)],
            axis=0)
        k3 = jnp.concatenate([k_ref[pl.ds(pl.multiple_of(b * A_BLOCK, A_BLOCK), A_BLOCK), :]
                              for b in key_blocks(j)], axis=0)
        return lax.dot_general(k3, qs, (((1,), (1,)), ((), ())), preferred_element_type=F32)

    def finish(j, half, s):
        key_pos = (i * (tq // A_BLOCK) + j - 1) * A_BLOCK + kj
        valid = in_window & (key_pos >= 0) & (key_pos < s_len)
        heads = [slice(r * A_BLOCK, (r + 1) * A_BLOCK) for r in range(4)]
        s = jnp.concatenate([jnp.where(valid, s[:, c] + bias_ref[half, :, c], MASK_VALUE) for c in heads], axis=1)
        sink = sink_ref[half]
        m = jnp.maximum(jnp.max(s, axis=0, keepdims=True), sink)
        p = jnp.exp2(s - m).astype(BF16)
        v3t = jnp.concatenate([vt_ref[b] for b in key_blocks(j)], axis=1)
        ones = jnp.ones((), BF16)
        lhs = jnp.where(first_rows, v3t, ones) if half == 0 else jnp.where(first_rows, ones, v3t)
        acc = jnp.dot(lhs, p, preferred_element_type=F32)
        num, den = (acc[:HEAD_DIM], acc[HEAD_DIM:HEAD_DIM + 1]) if half == 0 else (acc[HEAD_DIM:], acc[:1])
        return num * (1.0 / (den + jnp.exp2(sink - m)))

    units = [(j, half) for j in range(tq // A_BLOCK) for half in (0, 1)]
    ahead = 2
    pending = [scores(*u) for u in units[:ahead]]
    lo_half = None
    for n, (j, half) in enumerate(units):
        s_cur = pending.pop(0)
        if n + ahead < len(units):
            pending.append(scores(*units[n + ahead]))
        out = finish(j, half, s_cur)
        if half == 0:
            lo_half = out
        else:
            rows = slice(j * A_BLOCK, (j + 1) * A_BLOCK)
            for r in range(4):
                cols = slice(r * A_BLOCK, (r + 1) * A_BLOCK)
                o_ref[rows, r * LANES:(r + 1) * LANES] = (
                    jnp.concatenate([lo_half[:, cols], out[:, cols]], axis=0).T.astype(o_ref.dtype))


def _attn_a(proj, vt, bias_t, sink_t, *, bsz, s_len, tq, k_block):
    n = proj.shape[0]
    nq = s_len // tq
    assert s_len % tq == 0 and tq % A_BLOCK == 0
    kern = functools.partial(_attn_a_kernel, tq=tq, s_len=s_len)
    return pl.pallas_call(
        kern,
        grid=(bsz, nq),
        in_specs=[pl.BlockSpec((tq, 4 * LANES), lambda b, i: (b * nq + i, 0)),
                  pl.BlockSpec((s_len, LANES), lambda b, i: (b, k_block)),
                  pl.BlockSpec((None, None, s_len // LANES, LANES, LANES), lambda b, i: (b, 0, 0, 0, 0)),
                  pl.BlockSpec(bias_t.shape, lambda b, i: (0, 0, 0)),
                  pl.BlockSpec(sink_t.shape, lambda b, i: (0, 0, 0))],
        out_specs=pl.BlockSpec((tq, 4 * LANES), lambda b, i: (b * nq + i, 0)),
        out_shape=jax.ShapeDtypeStruct((n, 4 * LANES), BF16),
        compiler_params=_params("parallel", "arbitrary"),
        name="attn_a",
    )(proj, proj, vt, bias_t, sink_t)


def _attn_b_kernel(q_ref, k_ref, v_ref, t_ref, o_ref, *, rblk, rows):
    r0 = pl.program_id(2) * rblk
    lo = _lo_lanes()
    nkeys = NA_ROWS * GRID_W
    cq = lax.broadcasted_iota(jnp.int32, (GRID_W, nkeys), 0)
    ck = lax.broadcasted_iota(jnp.int32, (GRID_W, nkeys), 1) % GRID_W
    cs = jnp.clip(cq - NA_COLS // 2, 0, GRID_W - NA_COLS)
    col_valid = (ck >= cs) & (ck < cs + NA_COLS)

    def first_key_row(a):
        return jnp.clip(r0 + a - NA_ROWS // 2, 0, rows - NA_ROWS)

    def window(ref, a):
        return ref[pl.ds(pl.multiple_of(first_key_row(a) * GRID_W, GRID_W), nkeys), :]

    def scores(a):
        qrow = q_ref[a * GRID_W:(a + 1) * GRID_W, :]
        zero = jnp.zeros_like(qrow)
        qs = jnp.concatenate([jnp.where(lo, qrow, zero), jnp.where(lo, zero, qrow)], axis=0)
        return lax.dot_general(qs, window(k_ref, a), (((1,), (1,)), ((), ())), preferred_element_type=F32)

    def finish(a, s):
        off = first_key_row(a) - (r0 + a) + (NA_ROWS - 1)
        bias = jnp.concatenate(
            [jnp.concatenate([t_ref[half, off + 2 * j] for j in range(NA_ROWS // 2)], axis=1)
             for half in (0, 1)], axis=0)
        s = s.reshape(2, GRID_W, nkeys) + bias.reshape(2, GRID_W, nkeys)
        s = jnp.where(col_valid[None], s, MASK_VALUE)
        m = jnp.max(s, axis=-1, keepdims=True)
        e = jnp.exp2(s - m)
        den = jnp.sum(e, axis=-1, keepdims=True)
        pv = jnp.dot(e.reshape(2 * GRID_W, nkeys).astype(BF16), window(v_ref, a),
                     preferred_element_type=F32)
        pv = pv.reshape(2, GRID_W, LANES) * (1.0 / den)
        o_ref[a * GRID_W:(a + 1) * GRID_W, :] = jnp.where(lo, pv[0], pv[1]).astype(o_ref.dtype)

    ahead = min(4, rblk)
    pending = [scores(a) for a in range(ahead)]
    for a in range(rblk):
        s_cur = pending.pop(0)
        if a + ahead < rblk:
            pending.append(scores(a + ahead))
        finish(a, s_cur)


def _attn_b(proj, t_pair, *, bsz, s_len, rblk, q_block):
    n = proj.shape[0]
    k_block, v_block = q_block + 4, q_block + 8
    rows = s_len // GRID_W
    nrb = rows // rblk
    assert s_len % GRID_W == 0 and rows % rblk == 0 and rows >= NA_ROWS
    kern = functools.partial(_attn_b_kernel, rblk=rblk, rows=rows)
    tq = rblk * GRID_W
    return pl.pallas_call(
        kern,
        grid=(bsz, 4, nrb),
        in_specs=[pl.BlockSpec((tq, LANES), lambda b, h, i: (b * nrb + i, q_block + h)),
                  pl.BlockSpec((s_len, LANES), lambda b, h, i: (b, k_block + h)),
                  pl.BlockSpec((s_len, LANES), lambda b, h, i: (b, v_block + h)),
                  pl.BlockSpec((2, 2 * NA_ROWS - 2, GRID_W, LANES), lambda b, h, i: (h, 0, 0, 0))],
        out_specs=pl.BlockSpec((tq, LANES), lambda b, h, i: (b * nrb + i, h)),
        out_shape=jax.ShapeDtypeStruct((n, 4 * LANES), BF16),
        compiler_params=_params("parallel", "parallel", "arbitrary"),
        name="attn_b",
    )(proj, proj, proj, t_pair)


def _c_prep_kernel(x_ref, g_ref, e_ref, cos_ref, sin_ref, o_ref):
    x = x_ref[...].astype(F32)
    xx = x * x
    xx_hi = xx.astype(BF16)
    xx_lo = (xx - xx_hi.astype(F32)).astype(BF16)
    ssq = jnp.dot(jnp.concatenate([xx_hi, xx_lo], axis=1), e_ref[...], preferred_element_type=F32)
    y = x * lax.rsqrt(ssq * (1.0 / HEAD_DIM) + EPS) * g_ref[0]
    o_ref[...] = (y * cos_ref[...] + pltpu.roll(y, LANES // 2, 1) * sin_ref[...]).astype(o_ref.dtype)


def _c_prep(proj, gains, same_head, cos_t, sin_t, *, s_len, tr, first):
    n = proj.shape[0]
    nblk = gains.shape[0]
    npos = s_len // tr
    assert s_len % tr == 0
    pos_spec = pl.BlockSpec((tr, LANES), lambda i, j: (i % npos, 0))
    return pl.pallas_call(
        _c_prep_kernel,
        grid=(n // tr, nblk),
        in_specs=[pl.BlockSpec((tr, LANES), lambda i, j: (i, first + j)),
                  pl.BlockSpec((1, 1, LANES), lambda i, j: (j, 0, 0)),
                  pl.BlockSpec((2 * LANES, LANES), lambda i, j: (0, 0)),
                  pos_spec, pos_spec],
        out_specs=pl.BlockSpec((tr, LANES), lambda i, j: (i, j)),
        out_shape=jax.ShapeDtypeStruct((n, nblk * LANES), BF16),
        compiler_params=_params("parallel", "arbitrary"),
        name="c_prep",
    )(proj, gains, same_head, cos_t, sin_t)


VT_ROWS = 2 * HEAD_DIM


def _c_vprep_kernel(x_ref, o_ref):
    tiles, _, tk = o_ref.shape
    ones = jnp.ones((VT_ROWS - HEAD_DIM, tk), o_ref.dtype)
    for t in range(tiles):
        xt = x_ref[t * tk:(t + 1) * tk, :].astype(F32).T
        for half in (0, 1):
            o_ref[t, half * VT_ROWS:half * VT_ROWS + HEAD_DIM, :] = (
                xt[half * HEAD_DIM:(half + 1) * HEAD_DIM].astype(o_ref.dtype))
            o_ref[t, half * VT_ROWS + HEAD_DIM:(half + 1) * VT_ROWS, :] = ones


def _c_vprep(proj, *, bsz, s_len, tk):
    nk = s_len // tk
    tiles = min(4, nk)
    steps = nk // tiles
    assert s_len % tk == 0 and nk % tiles == 0
    v_first = proj.shape[1] // LANES - 2
    return pl.pallas_call(
        _c_vprep_kernel,
        grid=(bsz, 2, steps),
        in_specs=[pl.BlockSpec((tiles * tk, LANES), lambda b, p, i: (b * steps + i, v_first + p))],
        out_specs=pl.BlockSpec((None, None, tiles, 2 * VT_ROWS, tk), lambda b, p, i: (b, p, i, 0, 0)),
        out_shape=jax.ShapeDtypeStruct((bsz, 2, nk, 2 * VT_ROWS, tk), BF16),
        compiler_params=_params("parallel", "parallel", "arbitrary"),
        name="c_vprep",
    )(proj)


C_GROUP = 256


def _attn_c_kernel(q_ref, k_ref, vt_ref, o_ref, qs_ref, acc_ref, s_ref, *, tq, tk, s_len):
    lo = _first_head_lanes()
    per_blk = tq // C_GROUP
    n_groups = 8 * per_blk
    for r in range(4):
        for j in range(per_blk):
            qb = q_ref[j * C_GROUP:(j + 1) * C_GROUP, r * LANES:(r + 1) * LANES]
            zero = jnp.zeros_like(qb)
            qs_ref[r * per_blk + j] = jnp.where(lo, qb, zero).astype(F32).T.astype(BF16)
            qs_ref[(4 + r) * per_blk + j] = jnp.where(lo, zero, qb).astype(F32).T.astype(BF16)
    acc_ref[...] = jnp.zeros(acc_ref.shape, F32)

    nk = s_len // tk

    def scores(t, groups):
        kt = k_ref[pl.ds(pl.multiple_of(t * tk, tk), tk), :]
        for g in groups:
            s_ref[g] = jnp.dot(kt, qs_ref[g], preferred_element_type=F32)

    ahead = min(4, n_groups // 2)

    def step(t, t_next, m):
        m_out = []
        for g in range(n_groups):
            if g + ahead < n_groups:
                scores(t, (g + ahead,))
            else:
                scores(t_next, (g + ahead - n_groups,))
            st = s_ref[g]
            m_new = jnp.maximum(m[g], jnp.max(st, axis=0, keepdims=True))
            alpha = jnp.exp2(m[g] - m_new)
            pt = jnp.exp2(st - m_new).astype(BF16)
            half = 0 if g < n_groups // 2 else 1
            vt = vt_ref[t, half * VT_ROWS:(half + 1) * VT_ROWS, :]
            acc_ref[g] = alpha * acc_ref[g] + jnp.dot(vt, pt, preferred_element_type=F32)
            m_out.append(m_new)
        return tuple(m_out)

    scores(0, range(ahead))

    def two_steps(u, m):
        t0 = 2 * u
        m = step(t0, t0 + 1, m)
        return step(t0 + 1, jnp.minimum(t0 + 2, nk - 1), m)

    m0 = tuple(jnp.full((1, C_GROUP), -jnp.inf, F32) for _ in range(n_groups))
    lax.fori_loop(0, nk // 2, two_steps, m0)
    for r in range(4):
        for j in range(per_blk):
            a_lo = acc_ref[r * per_blk + j]
            a_hi = acc_ref[(4 + r) * per_blk + j]
            o_lo = a_lo[:HEAD_DIM] * (1.0 / a_lo[HEAD_DIM:HEAD_DIM + 1])
            o_hi = a_hi[:HEAD_DIM] * (1.0 / a_hi[HEAD_DIM:HEAD_DIM + 1])
            o_ref[j * C_GROUP:(j + 1) * C_GROUP, r * LANES:(r + 1) * LANES] = (
                jnp.concatenate([o_lo, o_hi], axis=0).T.astype(o_ref.dtype))


def _attn_c(qk, vt, *, bsz, s_len, tq, tk):
    n = qk.shape[0]
    nq = s_len // tq
    nk = s_len // tk
    assert s_len % tq == 0 and tq % C_GROUP == 0 and s_len % tk == 0 and nk % 2 == 0
    n_groups = 8 * tq // C_GROUP
    k_first = qk.shape[1] // LANES - 2
    kern = functools.partial(_attn_c_kernel, tq=tq, tk=tk, s_len=s_len)
    return pl.pallas_call(
        kern,
        grid=(bsz, 2, nq),
        in_specs=[pl.BlockSpec((tq, 4 * LANES), lambda b, p, i: (b * nq + i, p)),
                  pl.BlockSpec((s_len, LANES), lambda b, p, i: (b, k_first + p)),
                  pl.BlockSpec((None, None, nk, 2 * VT_ROWS, tk), lambda b, p, i: (b, p, 0, 0, 0))],
        out_specs=pl.BlockSpec((tq, 4 * LANES), lambda b, p, i: (b * nq + i, p)),
        out_shape=jax.ShapeDtypeStruct((n, 8 * LANES), BF16),
        scratch_shapes=[pltpu.VMEM((n_groups, LANES, C_GROUP), BF16),
                        pltpu.VMEM((n_groups, VT_ROWS, C_GROUP), F32),
                        pltpu.VMEM((n_groups, tk, C_GROUP), F32)],
        compiler_params=_params("parallel", "parallel", "arbitrary"),
        name="attn_c",
    )(qk, qk, vt)


def _mix_out_kernel(x_ref, oa_ref, ob_ref, oc_ref, ga_ref, gb_ref, gc_ref, w_ref, o_ref):
    parts = []
    for o_r, g_r in ((oa_ref, ga_ref), (ob_ref, gb_ref), (oc_ref, gc_ref)):
        o = o_r[...].astype(F32)
        parts.append((o * _rms_scale(o) * g_r[...]).astype(BF16))
    mix = jnp.concatenate(parts, axis=-1)
    o_ref[...] = x_ref[...] + jnp.dot(mix, w_ref[...], preferred_element_type=F32)


def _mix_out(x, oa, ob, oc, ga, gb, gc, w, layer, *, tm):
    n, d = x.shape
    assert n % tm == 0
    row = lambda width: pl.BlockSpec((tm, width), lambda i: (i, 0))
    const = lambda shape: pl.BlockSpec(shape, lambda i: (0, 0))
    return pl.pallas_call(
        _mix_out_kernel,
        grid=(n // tm,),
        in_specs=[row(d), row(oa.shape[1]), row(ob.shape[1]), row(oc.shape[1]),
                  const((1, oa.shape[1])), const((1, ob.shape[1])), const((1, oc.shape[1])),
                  pl.BlockSpec((None,) + w.shape[1:], lambda i: (layer, 0, 0))],
        out_specs=row(d),
        out_shape=jax.ShapeDtypeStruct((n, d), F32),
        compiler_params=_params("parallel"),
        name="mix_out",
    )(x, oa, ob, oc, ga.reshape(1, -1), gb.reshape(1, -1), gc.reshape(1, -1), w)


def _mlp_kernel(x_ref, g_ref, wu_ref, wd_ref, gf_ref, o_ref, h_ref, *, final_norm):
    f = pl.program_id(1)

    @pl.when(f == 0)
    def _():
        x = x_ref[...]
        h_ref[...] = (x * _rms_scale(x) * g_ref[...]).astype(BF16)
        o_ref[...] = x

    u = jnp.maximum(jnp.dot(h_ref[...], wu_ref[...], preferred_element_type=F32), 0.0)
    o_ref[...] += jnp.dot((u * u).astype(BF16), wd_ref[...], preferred_element_type=F32)

    if final_norm:
        @pl.when(f == pl.num_programs(1) - 1)
        def _():
            y = o_ref[...]
            o_ref[...] = y * _rms_scale(y) * gf_ref[...]


def _mlp(x, gain, w_up, w_down, layer, final_gain, *, tm, tf, final_norm):
    n, d = x.shape
    ff = w_up.shape[2]
    assert n % tm == 0 and ff % tf == 0
    kern = functools.partial(_mlp_kernel, final_norm=final_norm)
    return pl.pallas_call(
        kern,
        grid=(n // tm, ff // tf),
        in_specs=[pl.BlockSpec((tm, d), lambda i, f: (i, 0)),
                  pl.BlockSpec((1, d), lambda i, f: (0, 0)),
                  pl.BlockSpec((None, d, tf), lambda i, f: (layer, 0, f)),
                  pl.BlockSpec((None, tf, d), lambda i, f: (layer, f, 0)),
                  pl.BlockSpec((1, d), lambda i, f: (0, 0))],
        out_specs=pl.BlockSpec((tm, d), lambda i, f: (i, 0)),
        out_shape=jax.ShapeDtypeStruct((n, d), F32),
        scratch_shapes=[pltpu.VMEM((tm, d), BF16)],
        compiler_params=_params("parallel", "arbitrary"),
        name="mlp",
    )(x, gain.reshape(1, d), w_up, w_down, final_gain.reshape(1, d))


def _t5_bucket_np(rel):
    nb = T5_BUCKETS // 2
    max_exact = nb // 2
    base = np.where(rel > 0, nb, 0)
    n = np.abs(rel)
    nf = np.maximum(n, 1).astype(np.float32)
    large = max_exact + (np.log(nf / np.float32(max_exact)) / np.float32(math.log(T5_MAX_DIST / max_exact))
                         * np.float32(nb - max_exact)).astype(np.int32)
    large = np.minimum(large, nb - 1)
    return base + np.where(n < max_exact, n, large)


def _pair_heads(a, axis):
    shape = a.shape
    halves = shape[axis] // (8 * HEAD_DIM)
    a = a.reshape(shape[:axis] + (halves, 2, 4, HEAD_DIM) + shape[axis + 1:])
    return jnp.swapaxes(a, axis + 1, axis + 2).reshape(shape)


def _split_rotary(a, axis):
    shape = a.shape
    a = a.reshape(shape[:axis] + (shape[axis] // LANES, 2, 2, 2, HEAD_DIM // 4) + shape[axis + 1:])
    perm = (tuple(range(axis + 1)) + (axis + 3, axis + 1, axis + 2, axis + 4)
            + tuple(range(axis + 5, a.ndim)))
    return jnp.transpose(a, perm).reshape(shape)


def _first_head_lanes():
    lane = lax.broadcasted_iota(jnp.int32, (1, LANES), 1)
    return (lane // (HEAD_DIM // 2)) % 2 == 0


def _rope_tables(s_len):
    axis_dim = HEAD_DIM // 2
    t = jnp.arange(s_len)
    row = (t // GRID_W).astype(F32)
    col = (t % GRID_W).astype(F32)
    freqs = ROPE_THETA ** (-jnp.arange(0, axis_dim, 2, dtype=F32) / axis_dim)
    ang = jnp.concatenate([row[:, None] * freqs[None, :], col[:, None] * freqs[None, :]], axis=-1)
    sign = np.where(np.arange(LANES) < LANES // 2, -1.0, 1.0).astype(np.float32)
    return jnp.tile(jnp.cos(ang), (1, 4)), jnp.tile(jnp.sin(ang), (1, 4)) * sign


def kernel(x, norm_mix, w_in, a_sink, t5_table, b_rpb, c_q_gain, c_k_gain, out_gain_a, out_gain_b,
           out_gain_c, w_o, norm_mlp, w_up, w_down, norm_final):
    bsz, s_len, d_model = x.shape
    depth = w_in.shape[0]
    n = bsz * s_len
    scale = HEAD_DIM ** -0.5

    a_w = out_gain_a.shape[1]
    b_w = out_gain_b.shape[1]
    c_w = out_gain_c.shape[1]
    a_kv_w = a_w // 4
    c_kv_w = c_w // 4
    off_qb = a_w + 2 * a_kv_w
    off_qc = off_qb + 3 * b_w
    in_width = off_qc + c_w + 2 * c_kv_w
    assert in_width == w_in.shape[2] and a_w == 8 * HEAD_DIM and c_w == 16 * HEAD_DIM

    off_vc = off_qc + c_w + c_kv_w
    w_in_p = jnp.concatenate(
        [_pair_heads(w_in[..., :a_w], 2) * (scale * LOG2E), w_in[..., a_w:off_qb],
         w_in[..., off_qb:off_qb + b_w] * (scale * LOG2E), w_in[..., off_qb + b_w:off_qc],
         _split_rotary(_pair_heads(w_in[..., off_qc:off_qc + c_w], 2), 2),
         _split_rotary(w_in[..., off_qc + c_w:off_vc], 2), w_in[..., off_vc:]], axis=2).astype(BF16)
    w_o_p = jnp.concatenate(
        [_pair_heads(w_o[:, :a_w], 1), w_o[:, a_w:a_w + b_w], _pair_heads(w_o[:, a_w + b_w:], 1)],
        axis=1).astype(BF16)
    w_up_b = w_up.astype(BF16)
    w_down_b = w_down.astype(BF16)

    qi = np.arange(A_BLOCK)[:, None]
    kj = np.arange(3 * A_BLOCK)[None, :]
    bucket = _t5_bucket_np(kj - A_BLOCK - qi)
    bias_a = t5_table[bucket].astype(F32) * LOG2E
    bias_t = bias_a.reshape(A_BLOCK, 3 * A_BLOCK, 2, 4).transpose(2, 1, 3, 0)
    bias_t = bias_t.reshape(2, 3 * A_BLOCK, 4 * A_BLOCK)
    sink_t = jnp.repeat(a_sink.astype(F32) * LOG2E, A_BLOCK, axis=1).reshape(depth, 2, 1, 4 * A_BLOCK)

    cq = np.arange(GRID_W)[:, None]
    ck = np.arange(GRID_W)[None, :]
    dc = np.clip(ck - cq + NA_COLS - 1, 0, 2 * NA_COLS - 2)
    t_blocks = b_rpb[:, :, :, dc].astype(F32) * LOG2E
    t_pair = jnp.concatenate([t_blocks[:, :, :-1], t_blocks[:, :, 1:]], axis=-1)

    cos_t, sin_t = _rope_tables(s_len)
    two = lambda g: _split_rotary(jnp.concatenate([g, g], axis=-1), 0)
    n_qblk, n_kblk = c_w // LANES, c_kv_w // LANES
    head_of_lane = (np.arange(LANES) // (HEAD_DIM // 2)) % 2
    same_head = jnp.asarray(np.tile(head_of_lane[:, None] == head_of_lane[None, :], (2, 1)), BF16)

    xf = x.reshape(n, d_model)
    for l in range(depth):
        proj = _inproj(xf, norm_mix[l], w_in_p, l, tm=min(1024, n), tn=in_width // 3)
        vt_a = _vt_prep(proj, a_w // LANES + 1, 1, bsz=bsz, s_len=s_len, tr=min(1024, s_len))
        oa = _attn_a(proj, vt_a, bias_t, sink_t[l], bsz=bsz, s_len=s_len, tq=min(1024, s_len),
                     k_block=a_w // LANES)
        ob = _attn_b(proj, t_pair[l], bsz=bsz, s_len=s_len, rblk=min(32, s_len // GRID_W),
                     q_block=off_qb // LANES)
        gains = jnp.concatenate([jnp.tile(two(c_q_gain[l] * (scale * LOG2E))[None], (n_qblk, 1)),
                                 jnp.tile(two(c_k_gain[l])[None], (n_kblk, 1))], axis=0)
        qk = _c_prep(proj, gains.reshape(n_qblk + n_kblk, 1, LANES).astype(F32), same_head, cos_t, sin_t,
                     s_len=s_len, tr=min(4096, s_len), first=off_qc // LANES)
        tk_c = min(512, s_len // 2)
        vt = _c_vprep(proj, bsz=bsz, s_len=s_len, tk=tk_c)
        oc = _attn_c(qk, vt, bsz=bsz, s_len=s_len, tq=min(1024, s_len), tk=tk_c)
        xf = _mix_out(xf, oa, ob, oc, _pair_heads(out_gain_a[l], 0), out_gain_b[l],
                      _pair_heads(out_gain_c[l], 0), w_o_p, l, tm=min(512, n))
        xf = _mlp(xf, norm_mlp[l], w_up_b, w_down_b, l, norm_final,
                  tm=min(512, n), tf=1024, final_norm=(l == depth - 1))
    return xf.reshape(bsz, s_len, d_model)
```

```python
import functools
import math

import numpy as np
import jax
import jax.numpy as jnp
from jax import lax
from jax.experimental import pallas as pl
from jax.experimental.pallas import tpu as pltpu

HEAD_DIM = 64
LANES = 128
WINDOW = 128
A_BLOCK = 128
T5_BUCKETS = 32
T5_MAX_DIST = 128
GRID_W = 64
NA_ROWS = 8
NA_COLS = 16
ROPE_THETA = 10000.0
EPS = 1e-6
MASK_VALUE = -1e30
LOG2E = math.log2(math.e)
V7X_VMEM_BYTES = 64 * 1024 * 1024
VMEM_LIMIT = V7X_VMEM_BYTES * 7 // 8

F32 = jnp.float32
BF16 = jnp.bfloat16


def _params(*sem):
    return pltpu.CompilerParams(dimension_semantics=sem, vmem_limit_bytes=VMEM_LIMIT)


def _rms_scale(x):
    return lax.rsqrt(jnp.mean(x * x, axis=-1, keepdims=True) + EPS)


def _lo_lanes():
    return lax.broadcasted_iota(jnp.int32, (1, LANES), 1) < HEAD_DIM


def _inproj_kernel(x_ref, g_ref, w_ref, o_ref, h_ref):
    @pl.when(pl.program_id(1) == 0)
    def _():
        x = x_ref[...]
        h_ref[...] = (x * _rms_scale(x) * g_ref[...]).astype(BF16)

    o_ref[...] = jnp.dot(h_ref[...], w_ref[...], preferred_element_type=F32).astype(o_ref.dtype)


def _inproj(x, gain, w, layer, *, tm, tn):
    n, d = x.shape
    e = w.shape[2]
    assert n % tm == 0 and e % tn == 0
    return pl.pallas_call(
        _inproj_kernel,
        grid=(n // tm, e // tn),
        in_specs=[pl.BlockSpec((tm, d), lambda i, j: (i, 0)),
                  pl.BlockSpec((1, d), lambda i, j: (0, 0)),
                  pl.BlockSpec((None, d, tn), lambda i, j: (layer, 0, j))],
        out_specs=pl.BlockSpec((tm, tn), lambda i, j: (i, j)),
        out_shape=jax.ShapeDtypeStruct((n, e), BF16),
        scratch_shapes=[pltpu.VMEM((tm, d), BF16)],
        compiler_params=_params("parallel", "arbitrary"),
        name="inproj",
    )(x, gain.reshape(1, d), w)


def _vt_prep_kernel(x_ref, o_ref):
    for t in range(o_ref.shape[0]):
        o_ref[t] = x_ref[t * LANES:(t + 1) * LANES, :].astype(F32).T.astype(o_ref.dtype)


def _vt_prep(proj, first, count, *, bsz, s_len, tr):
    nt = s_len // tr
    assert s_len % tr == 0 and tr % LANES == 0
    return pl.pallas_call(
        _vt_prep_kernel,
        grid=(bsz, count, nt),
        in_specs=[pl.BlockSpec((tr, LANES), lambda b, c, i: (b * nt + i, first + c))],
        out_specs=pl.BlockSpec((None, None, tr // LANES, LANES, LANES), lambda b, c, i: (b, c, i, 0, 0)),
        out_shape=jax.ShapeDtypeStruct((bsz, count, s_len // LANES, LANES, LANES), BF16),
        compiler_params=_params("parallel", "parallel", "arbitrary"),
        name="vt_prep",
    )(proj)


def _attn_a_kernel(q_ref, k_ref, vt_ref, bias_ref, sink_ref, o_ref, *, tq, s_len):
    i = pl.program_id(1)
    lo = _lo_lanes()
    first_rows = lax.broadcasted_iota(jnp.int32, (LANES, 1), 0) < HEAD_DIM
    kj = lax.broadcasted_iota(jnp.int32, (3 * A_BLOCK, A_BLOCK), 0)
    qi = lax.broadcasted_iota(jnp.int32, (3 * A_BLOCK, A_BLOCK), 1)
    in_window = jnp.abs(kj - A_BLOCK - qi) <= WINDOW
    n_blocks = s_len // A_BLOCK

    def key_blocks(j):
        blk = i * (tq // A_BLOCK) + j
        return [jnp.clip(blk + o, 0, n_blocks - 1) for o in (-1, 0, 1)]

    def scores(j, half):
        keep = lo if half == 0 else jnp.logical_not(lo)
        rows = slice(j * A_BLOCK, (j + 1) * A_BLOCK)
        qs = jnp.concatenate(
            [jnp.where(keep, q_ref[rows, r * LANES:(r + 1) * LANES], jnp.zeros((), BF16)) for r in range(4)],
            axis=0)
        k3 = jnp.concatenate([k_ref[pl.ds(pl.multiple_of(b * A_BLOCK, A_BLOCK), A_BLOCK), :]
                              for b in key_blocks(j)], axis=0)
        return lax.dot_general(k3, qs, (((1,), (1,)), ((), ())), preferred_element_type=F32)

    def finish(j, half, s):
        key_pos = (i * (tq // A_BLOCK) + j - 1) * A_BLOCK + kj
        valid = in_window & (key_pos >= 0) & (key_pos < s_len)
        heads = [slice(r * A_BLOCK, (r + 1) * A_BLOCK) for r in range(4)]
        s = jnp.concatenate([jnp.where(valid, s[:, c] + bias_ref[half, :, c], MASK_VALUE) for c in heads], axis=1)
        sink = sink_ref[half]
        m = jnp.maximum(jnp.max(s, axis=0, keepdims=True), sink)
        p = jnp.exp2(s - m).astype(BF16)
        v3t = jnp.concatenate([vt_ref[b] for b in key_blocks(j)], axis=1)
        ones = jnp.ones((), BF16)
        lhs = jnp.where(first_rows, v3t, ones) if half == 0 else jnp.where(first_rows, ones, v3t)
        acc = jnp.dot(lhs, p, preferred_element_type=F32)
        num, den = (acc[:HEAD_DIM], acc[HEAD_DIM:HEAD_DIM + 1]) if half == 0 else (acc[HEAD_DIM:], acc[:1])
        return num * (1.0 / (den + jnp.exp2(sink - m)))

    units = [(j, half) for j in range(tq // A_BLOCK) for half in (0, 1)]
    ahead = 2
    pending = [scores(*u) for u in units[:ahead]]
    lo_half = None
    for n, (j, half) in enumerate(units):
        s_cur = pending.pop(0)
        if n + ahead < len(units):
            pending.append(scores(*units[n + ahead]))
        out = finish(j, half, s_cur)
        if half == 0:
            lo_half = out
        else:
            rows = slice(j * A_BLOCK, (j + 1) * A_BLOCK)
            for r in range(4):
                cols = slice(r * A_BLOCK, (r + 1) * A_BLOCK)
                o_ref[rows, r * LANES:(r + 1) * LANES] = (
                    jnp.concatenate([lo_half[:, cols], out[:, cols]], axis=0).T.astype(o_ref.dtype))


def _attn_a(proj, vt, bias_t, sink_t, *, bsz, s_len, tq, k_block):
    n = proj.shape[0]
    nq = s_len // tq
    assert s_len % tq == 0 and tq % A_BLOCK == 0
    kern = functools.partial(_attn_a_kernel, tq=tq, s_len=s_len)
    return pl.pallas_call(
        kern,
        grid=(bsz, nq),
        in_specs=[pl.BlockSpec((tq, 4 * LANES), lambda b, i: (b * nq + i, 0)),
                  pl.BlockSpec((s_len, LANES), lambda b, i: (b, k_block)),
                  pl.BlockSpec((None, None, s_len // LANES, LANES, LANES), lambda b, i: (b, 0, 0, 0, 0)),
                  pl.BlockSpec(bias_t.shape, lambda b, i: (0, 0, 0)),
                  pl.BlockSpec(sink_t.shape, lambda b, i: (0, 0, 0))],
        out_specs=pl.BlockSpec((tq, 4 * LANES), lambda b, i: (b * nq + i, 0)),
        out_shape=jax.ShapeDtypeStruct((n, 4 * LANES), BF16),
        compiler_params=_params("parallel", "arbitrary"),
        name="attn_a",
    )(proj, proj, vt, bias_t, sink_t)


def _attn_b_kernel(q_ref, k_ref, v_ref, t_ref, o_ref, *, rblk, rows):
    r0 = pl.program_id(2) * rblk
    lo = _lo_lanes()
    nkeys = NA_ROWS * GRID_W
    cq = lax.broadcasted_iota(jnp.int32, (GRID_W, nkeys), 0)
    ck = lax.broadcasted_iota(jnp.int32, (GRID_W, nkeys), 1) % GRID_W
    cs = jnp.clip(cq - NA_COLS // 2, 0, GRID_W - NA_COLS)
    col_valid = (ck >= cs) & (ck < cs + NA_COLS)

    def first_key_row(a):
        return jnp.clip(r0 + a - NA_ROWS // 2, 0, rows - NA_ROWS)

    def window(ref, a):
        return ref[pl.ds(pl.multiple_of(first_key_row(a) * GRID_W, GRID_W), nkeys), :]

    def scores(a):
        qrow = q_ref[a * GRID_W:(a + 1) * GRID_W, :]
        zero = jnp.zeros_like(qrow)
        qs = jnp.concatenate([jnp.where(lo, qrow, zero), jnp.where(lo, zero, qrow)], axis=0)
        return lax.dot_general(qs, window(k_ref, a), (((1,), (1,)), ((), ())), preferred_element_type=F32)

    def finish(a, s):
        off = first_key_row(a) - (r0 + a) + (NA_ROWS - 1)
        bias = jnp.concatenate(
            [jnp.concatenate([t_ref[half, off + 2 * j] for j in range(NA_ROWS // 2)], axis=1)
             for half in (0, 1)], axis=0)
        s = s.reshape(2, GRID_W, nkeys) + bias.reshape(2, GRID_W, nkeys)
        s = jnp.where(col_valid[None], s, MASK_VALUE)
        m = jnp.max(s, axis=-1, keepdims=True)
        e = jnp.exp2(s - m)
        den = jnp.sum(e, axis=-1, keepdims=True)
        pv = jnp.dot(e.reshape(2 * GRID_W, nkeys).astype(BF16), window(v_ref, a),
                     preferred_element_type=F32)
        pv = pv.reshape(2, GRID_W, LANES) * (1.0 / den)
        o_ref[a * GRID_W:(a + 1) * GRID_W, :] = jnp.where(lo, pv[0], pv[1]).astype(o_ref.dtype)

    ahead = min(4, rblk)
    pending = [scores(a) for a in range(ahead)]
    for a in range(rblk):
        s_cur = pending.pop(0)
        if a + ahead < rblk:
            pending.append(scores(a + ahead))
        finish(a, s_cur)


def _attn_b(proj, t_pair, *, bsz, s_len, rblk, q_block):
    n = proj.shape[0]
    k_block, v_block = q_block + 4, q_block + 8
    rows = s_len // GRID_W
    nrb = rows // rblk
    assert s_len % GRID_W == 0 and rows % rblk == 0 and rows >= NA_ROWS
    kern = functools.partial(_attn_b_kernel, rblk=rblk, rows=rows)
    tq = rblk * GRID_W
    return pl.pallas_call(
        kern,
        grid=(bsz, 4, nrb),
        in_specs=[pl.BlockSpec((tq, LANES), lambda b, h, i: (b * nrb + i, q_block + h)),
                  pl.BlockSpec((s_len, LANES), lambda b, h, i: (b, k_block + h)),
                  pl.BlockSpec((s_len, LANES), lambda b, h, i: (b, v_block + h)),
                  pl.BlockSpec((2, 2 * NA_ROWS - 2, GRID_W, LANES), lambda b, h, i: (h, 0, 0, 0))],
        out_specs=pl.BlockSpec((tq, LANES), lambda b, h, i: (b * nrb + i, h)),
        out_shape=jax.ShapeDtypeStruct((n, 4 * LANES), BF16),
        compiler_params=_params("parallel", "parallel", "arbitrary"),
        name="attn_b",
    )(proj, proj, proj, t_pair)


def _c_prep_kernel(x_ref, g_ref, e_ref, cos_ref, sin_ref, o_ref):
    x = x_ref[...].astype(F32)
    xx = x * x
    xx_hi = xx.astype(BF16)
    xx_lo = (xx - xx_hi.astype(F32)).astype(BF16)
    ssq = jnp.dot(jnp.concatenate([xx_hi, xx_lo], axis=1), e_ref[...], preferred_element_type=F32)
    y = x * lax.rsqrt(ssq * (1.0 / HEAD_DIM) + EPS) * g_ref[0]
    o_ref[...] = (y * cos_ref[...] + pltpu.roll(y, LANES // 2, 1) * sin_ref[...]).astype(o_ref.dtype)


def _c_prep(proj, gains, same_head, cos_t, sin_t, *, s_len, tr, first):
    n = proj.shape[0]
    nblk = gains.shape[0]
    npos = s_len // tr
    assert s_len % tr == 0
    pos_spec = pl.BlockSpec((tr, LANES), lambda i, j: (i % npos, 0))
    return pl.pallas_call(
        _c_prep_kernel,
        grid=(n // tr, nblk),
        in_specs=[pl.BlockSpec((tr, LANES), lambda i, j: (i, first + j)),
                  pl.BlockSpec((1, 1, LANES), lambda i, j: (j, 0, 0)),
                  pl.BlockSpec((2 * LANES, LANES), lambda i, j: (0, 0)),
                  pos_spec, pos_spec],
        out_specs=pl.BlockSpec((tr, LANES), lambda i, j: (i, j)),
        out_shape=jax.ShapeDtypeStruct((n, nblk * LANES), BF16),
        compiler_params=_params("parallel", "arbitrary"),
        name="c_prep",
    )(proj, gains, same_head, cos_t, sin_t)


VT_ROWS = HEAD_DIM + 16


def _c_vprep_kernel(x_ref, o_ref):
    tiles, _, tk = o_ref.shape
    ones = jnp.ones((VT_ROWS - HEAD_DIM, tk), o_ref.dtype)
    for t in range(tiles):
        xt = x_ref[t * tk:(t + 1) * tk, :].astype(F32).T
        for half in (0, 1):
            o_ref[t, half * VT_ROWS:half * VT_ROWS + HEAD_DIM, :] = (
                xt[half * HEAD_DIM:(half + 1) * HEAD_DIM].astype(o_ref.dtype))
            o_ref[t, half * VT_ROWS + HEAD_DIM:(half + 1) * VT_ROWS, :] = ones


def _c_vprep(proj, *, bsz, s_len, tk):
    nk = s_len // tk
    tiles = min(4, nk)
    steps = nk // tiles
    assert s_len % tk == 0 and nk % tiles == 0
    v_first = proj.shape[1] // LANES - 2
    return pl.pallas_call(
        _c_vprep_kernel,
        grid=(bsz, 2, steps),
        in_specs=[pl.BlockSpec((tiles * tk, LANES), lambda b, p, i: (b * steps + i, v_first + p))],
        out_specs=pl.BlockSpec((None, None, tiles, 2 * VT_ROWS, tk), lambda b, p, i: (b, p, i, 0, 0)),
        out_shape=jax.ShapeDtypeStruct((bsz, 2, nk, 2 * VT_ROWS, tk), BF16),
        compiler_params=_params("parallel", "parallel", "arbitrary"),
        name="c_vprep",
    )(proj)


C_GROUP = 256


def _attn_c_kernel(q_ref, k_ref, vt_ref, o_ref, qs_ref, acc_ref, s_ref, *, tq, tk, s_len):
    lo = _first_head_lanes()
    per_blk = tq // C_GROUP
    n_groups = 8 * per_blk
    for r in range(4):
        for j in range(per_blk):
            qb = q_ref[j * C_GROUP:(j + 1) * C_GROUP, r * LANES:(r + 1) * LANES]
            zero = jnp.zeros_like(qb)
            qs_ref[r * per_blk + j] = jnp.where(lo, qb, zero).astype(F32).T.astype(BF16)
            qs_ref[(4 + r) * per_blk + j] = jnp.where(lo, zero, qb).astype(F32).T.astype(BF16)
    acc_ref[...] = jnp.zeros(acc_ref.shape, F32)

    nk = s_len // tk

    def scores(t, groups):
        kt = k_ref[pl.ds(pl.multiple_of(t * tk, tk), tk), :]
        for g in groups:
            s_ref[g] = jnp.dot(kt, qs_ref[g], preferred_element_type=F32)

    ahead = min(4, n_groups // 2)

    def step(t, t_next, m):
        m_out = []
        for g in range(n_groups):
            if g + ahead < n_groups:
                scores(t, (g + ahead,))
            else:
                scores(t_next, (g + ahead - n_groups,))
            st = s_ref[g]
            m_new = jnp.maximum(m[g], jnp.max(st, axis=0, keepdims=True))
            alpha = jnp.exp2(m[g] - m_new)
            pt = jnp.exp2(st - m_new).astype(BF16)
            half = 0 if g < n_groups // 2 else 1
            vt = vt_ref[t, half * VT_ROWS:(half + 1) * VT_ROWS, :]
            acc_ref[g] = alpha * acc_ref[g] + jnp.dot(vt, pt, preferred_element_type=F32)
            m_out.append(m_new)
        return tuple(m_out)

    scores(0, range(ahead))

    def two_steps(u, m):
        t0 = 2 * u
        m = step(t0, t0 + 1, m)
        return step(t0 + 1, jnp.minimum(t0 + 2, nk - 1), m)

    m0 = tuple(jnp.full((1, C_GROUP), -jnp.inf, F32) for _ in range(n_groups))
    lax.fori_loop(0, nk // 2, two_steps, m0)
    for r in range(4):
        for j in range(per_blk):
            a_lo = acc_ref[r * per_blk + j]
            a_hi = acc_ref[(4 + r) * per_blk + j]
            o_lo = a_lo[:HEAD_DIM] * (1.0 / a_lo[HEAD_DIM:HEAD_DIM + 1])
            o_hi = a_hi[:HEAD_DIM] * (1.0 / a_hi[HEAD_DIM:HEAD_DIM + 1])
            o_ref[j * C_GROUP:(j + 1) * C_GROUP, r * LANES:(r + 1) * LANES] = (
                jnp.concatenate([o_lo, o_hi], axis=0).T.astype(o_ref.dtype))


def _attn_c(qk, vt, *, bsz, s_len, tq, tk):
    n = qk.shape[0]
    nq = s_len // tq
    nk = s_len // tk
    assert s_len % tq == 0 and tq % C_GROUP == 0 and s_len % tk == 0 and nk % 2 == 0
    n_groups = 8 * tq // C_GROUP
    k_first = qk.shape[1] // LANES - 2
    kern = functools.partial(_attn_c_kernel, tq=tq, tk=tk, s_len=s_len)
    return pl.pallas_call(
        kern,
        grid=(bsz, 2, nq),
        in_specs=[pl.BlockSpec((tq, 4 * LANES), lambda b, p, i: (b * nq + i, p)),
                  pl.BlockSpec((s_len, LANES), lambda b, p, i: (b, k_first + p)),
                  pl.BlockSpec((None, None, nk, 2 * VT_ROWS, tk), lambda b, p, i: (b, p, 0, 0, 0))],
        out_specs=pl.BlockSpec((tq, 4 * LANES), lambda b, p, i: (b * nq + i, p)),
        out_shape=jax.ShapeDtypeStruct((n, 8 * LANES), BF16),
        scratch_shapes=[pltpu.VMEM((n_groups, LANES, C_GROUP), BF16),
                        pltpu.VMEM((n_groups, VT_ROWS, C_GROUP), F32),
                        pltpu.VMEM((n_groups, tk, C_GROUP), F32)],
        compiler_params=_params("parallel", "parallel", "arbitrary"),
        name="attn_c",
    )(qk, qk, vt)


def _mix_out_kernel(x_ref, oa_ref, ob_ref, oc_ref, ga_ref, gb_ref, gc_ref, w_ref, o_ref):
    parts = []
    for o_r, g_r in ((oa_ref, ga_ref), (ob_ref, gb_ref), (oc_ref, gc_ref)):
        o = o_r[...].astype(F32)
        parts.append((o * _rms_scale(o) * g_r[...]).astype(BF16))
    mix = jnp.concatenate(parts, axis=-1)
    o_ref[...] = x_ref[...] + jnp.dot(mix, w_ref[...], preferred_element_type=F32)


def _mix_out(x, oa, ob, oc, ga, gb, gc, w, layer, *, tm):
    n, d = x.shape
    assert n % tm == 0
    row = lambda width: pl.BlockSpec((tm, width), lambda i: (i, 0))
    const = lambda shape: pl.BlockSpec(shape, lambda i: (0, 0))
    return pl.pallas_call(
        _mix_out_kernel,
        grid=(n // tm,),
        in_specs=[row(d), row(oa.shape[1]), row(ob.shape[1]), row(oc.shape[1]),
                  const((1, oa.shape[1])), const((1, ob.shape[1])), const((1, oc.shape[1])),
                  pl.BlockSpec((None,) + w.shape[1:], lambda i: (layer, 0, 0))],
        out_specs=row(d),
        out_shape=jax.ShapeDtypeStruct((n, d), F32),
        compiler_params=_params("parallel"),
        name="mix_out",
    )(x, oa, ob, oc, ga.reshape(1, -1), gb.reshape(1, -1), gc.reshape(1, -1), w)


def _mlp_kernel(x_ref, g_ref, wu_ref, wd_ref, gf_ref, o_ref, h_ref, *, final_norm):
    f = pl.program_id(1)

    @pl.when(f == 0)
    def _():
        x = x_ref[...]
        h_ref[...] = (x * _rms_scale(x) * g_ref[...]).astype(BF16)
        o_ref[...] = x

    u = jnp.maximum(jnp.dot(h_ref[...], wu_ref[...], preferred_element_type=F32), 0.0)
    o_ref[...] += jnp.dot((u * u).astype(BF16), wd_ref[...], preferred_element_type=F32)

    if final_norm:
        @pl.when(f == pl.num_programs(1) - 1)
        def _():
            y = o_ref[...]
            o_ref[...] = y * _rms_scale(y) * gf_ref[...]


def _mlp(x, gain, w_up, w_down, layer, final_gain, *, tm, tf, final_norm):
    n, d = x.shape
    ff = w_up.shape[2]
    assert n % tm == 0 and ff % tf == 0
    kern = functools.partial(_mlp_kernel, final_norm=final_norm)
    return pl.pallas_call(
        kern,
        grid=(n // tm, ff // tf),
        in_specs=[pl.BlockSpec((tm, d), lambda i, f: (i, 0)),
                  pl.BlockSpec((1, d), lambda i, f: (0, 0)),
                  pl.BlockSpec((None, d, tf), lambda i, f: (layer, 0, f)),
                  pl.BlockSpec((None, tf, d), lambda i, f: (layer, f, 0)),
                  pl.BlockSpec((1, d), lambda i, f: (0, 0))],
        out_specs=pl.BlockSpec((tm, d), lambda i, f: (i, 0)),
        out_shape=jax.ShapeDtypeStruct((n, d), F32),
        scratch_shapes=[pltpu.VMEM((tm, d), BF16)],
        compiler_params=_params("parallel", "arbitrary"),
        name="mlp",
    )(x, gain.reshape(1, d), w_up, w_down, final_gain.reshape(1, d))


def _t5_bucket_np(rel):
    nb = T5_BUCKETS // 2
    max_exact = nb // 2
    base = np.where(rel > 0, nb, 0)
    n = np.abs(rel)
    nf = np.maximum(n, 1).astype(np.float32)
    large = max_exact + (np.log(nf / np.float32(max_exact)) / np.float32(math.log(T5_MAX_DIST / max_exact))
                         * np.float32(nb - max_exact)).astype(np.int32)
    large = np.minimum(large, nb - 1)
    return base + np.where(n < max_exact, n, large)


def _pair_heads(a, axis):
    shape = a.shape
    halves = shape[axis] // (8 * HEAD_DIM)
    a = a.reshape(shape[:axis] + (halves, 2, 4, HEAD_DIM) + shape[axis + 1:])
    return jnp.swapaxes(a, axis + 1, axis + 2).reshape(shape)


def _split_rotary(a, axis):
    shape = a.shape
    a = a.reshape(shape[:axis] + (shape[axis] // LANES, 2, 2, 2, HEAD_DIM // 4) + shape[axis + 1:])
    perm = (tuple(range(axis + 1)) + (axis + 3, axis + 1, axis + 2, axis + 4)
            + tuple(range(axis + 5, a.ndim)))
    return jnp.transpose(a, perm).reshape(shape)


def _first_head_lanes():
    lane = lax.broadcasted_iota(jnp.int32, (1, LANES), 1)
    return (lane // (HEAD_DIM // 2)) % 2 == 0


def _rope_tables(s_len):
    axis_dim = HEAD_DIM // 2
    t = jnp.arange(s_len)
    row = (t // GRID_W).astype(F32)
    col = (t % GRID_W).astype(F32)
    freqs = ROPE_THETA ** (-jnp.arange(0, axis_dim, 2, dtype=F32) / axis_dim)
    ang = jnp.concatenate([row[:, None] * freqs[None, :], col[:, None] * freqs[None, :]], axis=-1)
    sign = np.where(np.arange(LANES) < LANES // 2, -1.0, 1.0).astype(np.float32)
    return jnp.tile(jnp.cos(ang), (1, 4)), jnp.tile(jnp.sin(ang), (1, 4)) * sign


def kernel(x, norm_mix, w_in, a_sink, t5_table, b_rpb, c_q_gain, c_k_gain, out_gain_a, out_gain_b,
           out_gain_c, w_o, norm_mlp, w_up, w_down, norm_final):
    bsz, s_len, d_model = x.shape
    depth = w_in.shape[0]
    n = bsz * s_len
    scale = HEAD_DIM ** -0.5

    a_w = out_gain_a.shape[1]
    b_w = out_gain_b.shape[1]
    c_w = out_gain_c.shape[1]
    a_kv_w = a_w // 4
    c_kv_w = c_w // 4
    off_qb = a_w + 2 * a_kv_w
    off_qc = off_qb + 3 * b_w
    in_width = off_qc + c_w + 2 * c_kv_w
    assert in_width == w_in.shape[2] and a_w == 8 * HEAD_DIM and c_w == 16 * HEAD_DIM

    off_vc = off_qc + c_w + c_kv_w
    w_in_p = jnp.concatenate(
        [_pair_heads(w_in[..., :a_w], 2) * (scale * LOG2E), w_in[..., a_w:off_qb],
         w_in[..., off_qb:off_qb + b_w] * (scale * LOG2E), w_in[..., off_qb + b_w:off_qc],
         _split_rotary(_pair_heads(w_in[..., off_qc:off_qc + c_w], 2), 2),
         _split_rotary(w_in[..., off_qc + c_w:off_vc], 2), w_in[..., off_vc:]], axis=2).astype(BF16)
    w_o_p = jnp.concatenate(
        [_pair_heads(w_o[:, :a_w], 1), w_o[:, a_w:a_w + b_w], _pair_heads(w_o[:, a_w + b_w:], 1)],
        axis=1).astype(BF16)
    w_up_b = w_up.astype(BF16)
    w_down_b = w_down.astype(BF16)

    qi = np.arange(A_BLOCK)[:, None]
    kj = np.arange(3 * A_BLOCK)[None, :]
    bucket = _t5_bucket_np(kj - A_BLOCK - qi)
    bias_a = t5_table[bucket].astype(F32) * LOG2E
    bias_t = bias_a.reshape(A_BLOCK, 3 * A_BLOCK, 2, 4).transpose(2, 1, 3, 0)
    bias_t = bias_t.reshape(2, 3 * A_BLOCK, 4 * A_BLOCK)
    sink_t = jnp.repeat(a_sink.astype(F32) * LOG2E, A_BLOCK, axis=1).reshape(depth, 2, 1, 4 * A_BLOCK)

    cq = np.arange(GRID_W)[:, None]
    ck = np.arange(GRID_W)[None, :]
    dc = np.clip(ck - cq + NA_COLS - 1, 0, 2 * NA_COLS - 2)
    t_blocks = b_rpb[:, :, :, dc].astype(F32) * LOG2E
    t_pair = jnp.concatenate([t_blocks[:, :, :-1], t_blocks[:, :, 1:]], axis=-1)

    cos_t, sin_t = _rope_tables(s_len)
    two = lambda g: _split_rotary(jnp.concatenate([g, g], axis=-1), 0)
    n_qblk, n_kblk = c_w // LANES, c_kv_w // LANES
    head_of_lane = (np.arange(LANES) // (HEAD_DIM // 2)) % 2
    same_head = jnp.asarray(np.tile(head_of_lane[:, None] == head_of_lane[None, :], (2, 1)), BF16)

    xf = x.reshape(n, d_model)
    for l in range(depth):
        proj = _inproj(xf, norm_mix[l], w_in_p, l, tm=min(1024, n), tn=in_width // 3)
        vt_a = _vt_prep(proj, a_w // LANES + 1, 1, bsz=bsz, s_len=s_len, tr=min(1024, s_len))
        oa = _attn_a(proj, vt_a, bias_t, sink_t[l], bsz=bsz, s_len=s_len, tq=min(1024, s_len),
                     k_block=a_w // LANES)
        ob = _attn_b(proj, t_pair[l], bsz=bsz, s_len=s_len, rblk=min(32, s_len // GRID_W),
                     q_block=off_qb // LANES)
        gains = jnp.concatenate([jnp.tile(two(c_q_gain[l] * (scale * LOG2E))[None], (n_qblk, 1)),
                                 jnp.tile(two(c_k_gain[l])[None], (n_kblk, 1))], axis=0)
        qk = _c_prep(proj, gains.reshape(n_qblk + n_kblk, 1, LANES).astype(F32), same_head, cos_t, sin_t,
                     s_len=s_len, tr=min(4096, s_len), first=off_qc // LANES)
        tk_c = min(512, s_len // 2)
        vt = _c_vprep(proj, bsz=bsz, s_len=s_len, tk=tk_c)
        oc = _attn_c(qk, vt, bsz=bsz, s_len=s_len, tq=min(1024, s_len), tk=tk_c)
        xf = _mix_out(xf, oa, ob, oc, _pair_heads(out_gain_a[l], 0), out_gain_b[l],
                      _pair_heads(out_gain_c[l], 0), w_o_p, l, tm=min(512, n))
        xf = _mlp(xf, norm_mlp[l], w_up_b, w_down_b, l, norm_final,
                  tm=min(512, n), tf=1024, final_norm=(l == depth - 1))
    return xf.reshape(bsz, s_len, d_model)
```

```python
import functools
import math

import numpy as np
import jax
import jax.numpy as jnp
from jax import lax
from jax.experimental import pallas as pl
from jax.experimental.pallas import tpu as pltpu

HEAD_DIM = 64
LANES = 128
WINDOW = 128
A_BLOCK = 128
T5_BUCKETS = 32
T5_MAX_DIST = 128
GRID_W = 64
NA_ROWS = 8
NA_COLS = 16
ROPE_THETA = 10000.0
EPS = 1e-6
MASK_VALUE = -1e30
LOG2E = math.log2(math.e)
V7X_VMEM_BYTES = 64 * 1024 * 1024
VMEM_LIMIT = V7X_VMEM_BYTES * 7 // 8

F32 = jnp.float32
BF16 = jnp.bfloat16


def _params(*sem):
    return pltpu.CompilerParams(dimension_semantics=sem, vmem_limit_bytes=VMEM_LIMIT)


def _rms_scale(x):
    return lax.rsqrt(jnp.mean(x * x, axis=-1, keepdims=True) + EPS)


def _lo_lanes():
    return lax.broadcasted_iota(jnp.int32, (1, LANES), 1) < HEAD_DIM


def _inproj_kernel(x_ref, g_ref, w_ref, o_ref, h_ref):
    @pl.when(pl.program_id(1) == 0)
    def _():
        x = x_ref[...]
        h_ref[...] = (x * _rms_scale(x) * g_ref[...]).astype(BF16)

    o_ref[...] = jnp.dot(h_ref[...], w_ref[...], preferred_element_type=F32).astype(o_ref.dtype)


def _inproj(x, gain, w, layer, *, tm, tn):
    n, d = x.shape
    e = w.shape[2]
    assert n % tm == 0 and e % tn == 0
    return pl.pallas_call(
        _inproj_kernel,
        grid=(n // tm, e // tn),
        in_specs=[pl.BlockSpec((tm, d), lambda i, j: (i, 0)),
                  pl.BlockSpec((1, d), lambda i, j: (0, 0)),
                  pl.BlockSpec((None, d, tn), lambda i, j: (layer, 0, j))],
        out_specs=pl.BlockSpec((tm, tn), lambda i, j: (i, j)),
        out_shape=jax.ShapeDtypeStruct((n, e), BF16),
        scratch_shapes=[pltpu.VMEM((tm, d), BF16)],
        compiler_params=_params("parallel", "arbitrary"),
        name="inproj",
    )(x, gain.reshape(1, d), w)


def _attn_a_kernel(q_ref, k_ref, v_ref, bias_ref, sink_ref, o_ref, *, tq, s_len):
    i = pl.program_id(1)
    lo = _lo_lanes()
    kj = lax.broadcasted_iota(jnp.int32, (3 * A_BLOCK, A_BLOCK), 0)
    qi = lax.broadcasted_iota(jnp.int32, (3 * A_BLOCK, A_BLOCK), 1)
    in_window = jnp.abs(kj - A_BLOCK - qi) <= WINDOW
    n_blocks = s_len // A_BLOCK

    def key_blocks(j):
        blk = i * (tq // A_BLOCK) + j
        return [jnp.clip(blk + o, 0, n_blocks - 1) for o in (-1, 0, 1)]

    def scores(j, half):
        keep = lo if half == 0 else jnp.logical_not(lo)
        rows = slice(j * A_BLOCK, (j + 1) * A_BLOCK)
        qs = jnp.concatenate(
            [jnp.where(keep, q_ref[rows, r * LANES:(r + 1) * LANES], jnp.zeros((), BF16)) for r in range(4)],
            axis=0)
        k3 = jnp.concatenate([k_ref[pl.ds(pl.multiple_of(b * A_BLOCK, A_BLOCK), A_BLOCK), :]
                              for b in key_blocks(j)], axis=0)
        return lax.dot_general(k3, qs, (((1,), (1,)), ((), ())), preferred_element_type=F32)

    def finish(j, half, s):
        key_pos = (i * (tq // A_BLOCK) + j - 1) * A_BLOCK + kj
        valid = in_window & (key_pos >= 0) & (key_pos < s_len)
        heads = [slice(r * A_BLOCK, (r + 1) * A_BLOCK) for r in range(4)]
        s = jnp.concatenate([jnp.where(valid, s[:, c] + bias_ref[half, :, c], MASK_VALUE) for c in heads], axis=1)
        sink = sink_ref[half]
        m = jnp.maximum(jnp.max(s, axis=0, keepdims=True), sink)
        p = jnp.exp2(s - m).astype(BF16)
        v3 = jnp.concatenate([v_ref[pl.ds(pl.multiple_of(b * A_BLOCK, A_BLOCK), A_BLOCK), :]
                              for b in key_blocks(j)], axis=0)
        ones = jnp.ones((), BF16)
        v_ones = jnp.where(lo, v3, ones) if half == 0 else jnp.where(lo, ones, v3)
        acc = lax.dot_general(v_ones, p, (((0,), (0,)), ((), ())),
                              preferred_element_type=F32)
        num, den = (acc[:HEAD_DIM], acc[HEAD_DIM:HEAD_DIM + 1]) if half == 0 else (acc[HEAD_DIM:], acc[:1])
        return num * (1.0 / (den + jnp.exp2(sink - m)))

    units = [(j, half) for j in range(tq // A_BLOCK) for half in (0, 1)]
    ahead = 2
    pending = [scores(*u) for u in units[:ahead]]
    lo_half = None
    for n, (j, half) in enumerate(units):
        s_cur = pending.pop(0)
        if n + ahead < len(units):
            pending.append(scores(*units[n + ahead]))
        out = finish(j, half, s_cur)
        if half == 0:
            lo_half = out
        else:
            rows = slice(j * A_BLOCK, (j + 1) * A_BLOCK)
            for r in range(4):
                cols = slice(r * A_BLOCK, (r + 1) * A_BLOCK)
                o_ref[rows, r * LANES:(r + 1) * LANES] = (
                    jnp.concatenate([lo_half[:, cols], out[:, cols]], axis=0).T.astype(o_ref.dtype))


def _attn_a(proj, bias_t, sink_t, *, bsz, s_len, tq, k_block):
    n = proj.shape[0]
    nq = s_len // tq
    assert s_len % tq == 0 and tq % A_BLOCK == 0
    kern = functools.partial(_attn_a_kernel, tq=tq, s_len=s_len)
    return pl.pallas_call(
        kern,
        grid=(bsz, nq),
        in_specs=[pl.BlockSpec((tq, 4 * LANES), lambda b, i: (b * nq + i, 0)),
                  pl.BlockSpec((s_len, LANES), lambda b, i: (b, k_block)),
                  pl.BlockSpec((s_len, LANES), lambda b, i: (b, k_block + 1)),
                  pl.BlockSpec(bias_t.shape, lambda b, i: (0, 0, 0)),
                  pl.BlockSpec(sink_t.shape, lambda b, i: (0, 0, 0))],
        out_specs=pl.BlockSpec((tq, 4 * LANES), lambda b, i: (b * nq + i, 0)),
        out_shape=jax.ShapeDtypeStruct((n, 4 * LANES), BF16),
        compiler_params=_params("parallel", "arbitrary"),
        name="attn_a",
    )(proj, proj, proj, bias_t, sink_t)


def _attn_b_kernel(q_ref, k_ref, v_ref, t_ref, o_ref, *, rblk, rows):
    r0 = pl.program_id(2) * rblk
    lo = _lo_lanes()
    nkeys = NA_ROWS * GRID_W
    cq = lax.broadcasted_iota(jnp.int32, (GRID_W, nkeys), 0)
    ck = lax.broadcasted_iota(jnp.int32, (GRID_W, nkeys), 1) % GRID_W
    cs = jnp.clip(cq - NA_COLS // 2, 0, GRID_W - NA_COLS)
    col_valid = (ck >= cs) & (ck < cs + NA_COLS)

    def first_key_row(a):
        return jnp.clip(r0 + a - NA_ROWS // 2, 0, rows - NA_ROWS)

    def window(ref, a):
        return ref[pl.ds(pl.multiple_of(first_key_row(a) * GRID_W, GRID_W), nkeys), :]

    def scores(a):
        qrow = q_ref[a * GRID_W:(a + 1) * GRID_W, :]
        zero = jnp.zeros_like(qrow)
        qs = jnp.concatenate([jnp.where(lo, qrow, zero), jnp.where(lo, zero, qrow)], axis=0)
        return lax.dot_general(qs, window(k_ref, a), (((1,), (1,)), ((), ())), preferred_element_type=F32)

    def finish(a, s):
        off = first_key_row(a) - (r0 + a) + (NA_ROWS - 1)
        bias = jnp.concatenate(
            [jnp.concatenate([t_ref[half, off + 2 * j] for j in range(NA_ROWS // 2)], axis=1)
             for half in (0, 1)], axis=0)
        s = s.reshape(2, GRID_W, nkeys) + bias.reshape(2, GRID_W, nkeys)
        s = jnp.where(col_valid[None], s, MASK_VALUE)
        m = jnp.max(s, axis=-1, keepdims=True)
        e = jnp.exp2(s - m)
        den = jnp.sum(e, axis=-1, keepdims=True)
        pv = jnp.dot(e.reshape(2 * GRID_W, nkeys).astype(BF16), window(v_ref, a),
                     preferred_element_type=F32)
        pv = pv.reshape(2, GRID_W, LANES) * (1.0 / den)
        o_ref[a * GRID_W:(a + 1) * GRID_W, :] = jnp.where(lo, pv[0], pv[1]).astype(o_ref.dtype)

    ahead = min(4, rblk)
    pending = [scores(a) for a in range(ahead)]
    for a in range(rblk):
        s_cur = pending.pop(0)
        if a + ahead < rblk:
            pending.append(scores(a + ahead))
        finish(a, s_cur)


def _attn_b(proj, t_pair, *, bsz, s_len, rblk, q_block):
    n = proj.shape[0]
    k_block, v_block = q_block + 4, q_block + 8
    rows = s_len // GRID_W
    nrb = rows // rblk
    assert s_len % GRID_W == 0 and rows % rblk == 0 and rows >= NA_ROWS
    kern = functools.partial(_attn_b_kernel, rblk=rblk, rows=rows)
    tq = rblk * GRID_W
    return pl.pallas_call(
        kern,
        grid=(bsz, 4, nrb),
        in_specs=[pl.BlockSpec((tq, LANES), lambda b, h, i: (b * nrb + i, q_block + h)),
                  pl.BlockSpec((s_len, LANES), lambda b, h, i: (b, k_block + h)),
                  pl.BlockSpec((s_len, LANES), lambda b, h, i: (b, v_block + h)),
                  pl.BlockSpec((2, 2 * NA_ROWS - 2, GRID_W, LANES), lambda b, h, i: (h, 0, 0, 0))],
        out_specs=pl.BlockSpec((tq, LANES), lambda b, h, i: (b * nrb + i, h)),
        out_shape=jax.ShapeDtypeStruct((n, 4 * LANES), BF16),
        compiler_params=_params("parallel", "parallel", "arbitrary"),
        name="attn_b",
    )(proj, proj, proj, t_pair)


def _c_prep_kernel(x_ref, g_ref, e_ref, cos_ref, sin_ref, o_ref):
    x = x_ref[...].astype(F32)
    xx = x * x
    xx_hi = xx.astype(BF16)
    xx_lo = (xx - xx_hi.astype(F32)).astype(BF16)
    ssq = jnp.dot(jnp.concatenate([xx_hi, xx_lo], axis=1), e_ref[...], preferred_element_type=F32)
    y = x * lax.rsqrt(ssq * (1.0 / HEAD_DIM) + EPS) * g_ref[0]
    o_ref[...] = (y * cos_ref[...] + pltpu.roll(y, LANES // 2, 1) * sin_ref[...]).astype(o_ref.dtype)


def _c_prep(proj, gains, same_head, cos_t, sin_t, *, s_len, tr, first):
    n = proj.shape[0]
    nblk = gains.shape[0]
    npos = s_len // tr
    assert s_len % tr == 0
    pos_spec = pl.BlockSpec((tr, LANES), lambda i, j: (i % npos, 0))
    return pl.pallas_call(
        _c_prep_kernel,
        grid=(n // tr, nblk),
        in_specs=[pl.BlockSpec((tr, LANES), lambda i, j: (i, first + j)),
                  pl.BlockSpec((1, 1, LANES), lambda i, j: (j, 0, 0)),
                  pl.BlockSpec((2 * LANES, LANES), lambda i, j: (0, 0)),
                  pos_spec, pos_spec],
        out_specs=pl.BlockSpec((tr, LANES), lambda i, j: (i, j)),
        out_shape=jax.ShapeDtypeStruct((n, nblk * LANES), BF16),
        compiler_params=_params("parallel", "arbitrary"),
        name="c_prep",
    )(proj, gains, same_head, cos_t, sin_t)


VT_ROWS = HEAD_DIM + 16


def _c_vprep_kernel(x_ref, o_ref):
    tiles, _, tk = o_ref.shape
    ones = jnp.ones((VT_ROWS - HEAD_DIM, tk), o_ref.dtype)
    for t in range(tiles):
        xt = x_ref[t * tk:(t + 1) * tk, :].astype(F32).T
        for half in (0, 1):
            o_ref[t, half * VT_ROWS:half * VT_ROWS + HEAD_DIM, :] = (
                xt[half * HEAD_DIM:(half + 1) * HEAD_DIM].astype(o_ref.dtype))
            o_ref[t, half * VT_ROWS + HEAD_DIM:(half + 1) * VT_ROWS, :] = ones


def _c_vprep(proj, *, bsz, s_len, tk):
    nk = s_len // tk
    tiles = min(4, nk)
    steps = nk // tiles
    assert s_len % tk == 0 and nk % tiles == 0
    v_first = proj.shape[1] // LANES - 2
    return pl.pallas_call(
        _c_vprep_kernel,
        grid=(bsz, 2, steps),
        in_specs=[pl.BlockSpec((tiles * tk, LANES), lambda b, p, i: (b * steps + i, v_first + p))],
        out_specs=pl.BlockSpec((None, None, tiles, 2 * VT_ROWS, tk), lambda b, p, i: (b, p, i, 0, 0)),
        out_shape=jax.ShapeDtypeStruct((bsz, 2, nk, 2 * VT_ROWS, tk), BF16),
        compiler_params=_params("parallel", "parallel", "arbitrary"),
        name="c_vprep",
    )(proj)


C_GROUP = 256


def _attn_c_kernel(q_ref, k_ref, vt_ref, o_ref, qs_ref, acc_ref, s_ref, *, tq, tk, s_len):
    lo = _first_head_lanes()
    per_blk = tq // C_GROUP
    n_groups = 8 * per_blk
    for r in range(4):
        for j in range(per_blk):
            qb = q_ref[j * C_GROUP:(j + 1) * C_GROUP, r * LANES:(r + 1) * LANES]
            zero = jnp.zeros_like(qb)
            qs_ref[r * per_blk + j] = jnp.where(lo, qb, zero).astype(F32).T.astype(BF16)
            qs_ref[(4 + r) * per_blk + j] = jnp.where(lo, zero, qb).astype(F32).T.astype(BF16)
    acc_ref[...] = jnp.zeros(acc_ref.shape, F32)

    nk = s_len // tk

    def scores(t, groups):
        kt = k_ref[pl.ds(pl.multiple_of(t * tk, tk), tk), :]
        for g in groups:
            s_ref[g] = jnp.dot(kt, qs_ref[g], preferred_element_type=F32)

    ahead = min(4, n_groups // 2)

    def step(t, t_next, m):
        m_out = []
        for g in range(n_groups):
            if g + ahead < n_groups:
                scores(t, (g + ahead,))
            else:
                scores(t_next, (g + ahead - n_groups,))
            st = s_ref[g]
            m_new = jnp.maximum(m[g], jnp.max(st, axis=0, keepdims=True))
            alpha = jnp.exp2(m[g] - m_new)
            pt = jnp.exp2(st - m_new).astype(BF16)
            half = 0 if g < n_groups // 2 else 1
            vt = vt_ref[t, half * VT_ROWS:(half + 1) * VT_ROWS, :]
            acc_ref[g] = alpha * acc_ref[g] + jnp.dot(vt, pt, preferred_element_type=F32)
            m_out.append(m_new)
        return tuple(m_out)

    scores(0, range(ahead))

    def two_steps(u, m):
        t0 = 2 * u
        m = step(t0, t0 + 1, m)
        return step(t0 + 1, jnp.minimum(t0 + 2, nk - 1), m)

    m0 = tuple(jnp.full((1, C_GROUP), -jnp.inf, F32) for _ in range(n_groups))
    lax.fori_loop(0, nk // 2, two_steps, m0)
    for r in range(4):
        for j in range(per_blk):
            a_lo = acc_ref[r * per_blk + j]
            a_hi = acc_ref[(4 + r) * per_blk + j]
            o_lo = a_lo[:HEAD_DIM] * (1.0 / a_lo[HEAD_DIM:HEAD_DIM + 1])
            o_hi = a_hi[:HEAD_DIM] * (1.0 / a_hi[HEAD_DIM:HEAD_DIM + 1])
            o_ref[j * C_GROUP:(j + 1) * C_GROUP, r * LANES:(r + 1) * LANES] = (
                jnp.concatenate([o_lo, o_hi], axis=0).T.astype(o_ref.dtype))


def _attn_c(qk, vt, *, bsz, s_len, tq, tk):
    n = qk.shape[0]
    nq = s_len // tq
    nk = s_len // tk
    assert s_len % tq == 0 and tq % C_GROUP == 0 and s_len % tk == 0 and nk % 2 == 0
    n_groups = 8 * tq // C_GROUP
    k_first = qk.shape[1] // LANES - 2
    kern = functools.partial(_attn_c_kernel, tq=tq, tk=tk, s_len=s_len)
    return pl.pallas_call(
        kern,
        grid=(bsz, 2, nq),
        in_specs=[pl.BlockSpec((tq, 4 * LANES), lambda b, p, i: (b * nq + i, p)),
                  pl.BlockSpec((s_len, LANES), lambda b, p, i: (b, k_first + p)),
                  pl.BlockSpec((None, None, nk, 2 * VT_ROWS, tk), lambda b, p, i: (b, p, 0, 0, 0))],
        out_specs=pl.BlockSpec((tq, 4 * LANES), lambda b, p, i: (b * nq + i, p)),
        out_shape=jax.ShapeDtypeStruct((n, 8 * LANES), BF16),
        scratch_shapes=[pltpu.VMEM((n_groups, LANES, C_GROUP), BF16),
                        pltpu.VMEM((n_groups, VT_ROWS, C_GROUP), F32),
                        pltpu.VMEM((n_groups, tk, C_GROUP), F32)],
        compiler_params=_params("parallel", "parallel", "arbitrary"),
        name="attn_c",
    )(qk, qk, vt)


def _mix_out_kernel(x_ref, oa_ref, ob_ref, oc_ref, ga_ref, gb_ref, gc_ref, w_ref, o_ref):
    parts = []
    for o_r, g_r in ((oa_ref, ga_ref), (ob_ref, gb_ref), (oc_ref, gc_ref)):
        o = o_r[...].astype(F32)
        parts.append((o * _rms_scale(o) * g_r[...]).astype(BF16))
    mix = jnp.concatenate(parts, axis=-1)
    o_ref[...] = x_ref[...] + jnp.dot(mix, w_ref[...], preferred_element_type=F32)


def _mix_out(x, oa, ob, oc, ga, gb, gc, w, layer, *, tm):
    n, d = x.shape
    assert n % tm == 0
    row = lambda width: pl.BlockSpec((tm, width), lambda i: (i, 0))
    const = lambda shape: pl.BlockSpec(shape, lambda i: (0, 0))
    return pl.pallas_call(
        _mix_out_kernel,
        grid=(n // tm,),
        in_specs=[row(d), row(oa.shape[1]), row(ob.shape[1]), row(oc.shape[1]),
                  const((1, oa.shape[1])), const((1, ob.shape[1])), const((1, oc.shape[1])),
                  pl.BlockSpec((None,) + w.shape[1:], lambda i: (layer, 0, 0))],
        out_specs=row(d),
        out_shape=jax.ShapeDtypeStruct((n, d), F32),
        compiler_params=_params("parallel"),
        name="mix_out",
    )(x, oa, ob, oc, ga.reshape(1, -1), gb.reshape(1, -1), gc.reshape(1, -1), w)


def _mlp_kernel(x_ref, g_ref, wu_ref, wd_ref, gf_ref, o_ref, h_ref, *, final_norm):
    f = pl.program_id(1)

    @pl.when(f == 0)
    def _():
        x = x_ref[...]
        h_ref[...] = (x * _rms_scale(x) * g_ref[...]).astype(BF16)
        o_ref[...] = x

    u = jnp.maximum(jnp.dot(h_ref[...], wu_ref[...], preferred_element_type=F32), 0.0)
    o_ref[...] += jnp.dot((u * u).astype(BF16), wd_ref[...], preferred_element_type=F32)

    if final_norm:
        @pl.when(f == pl.num_programs(1) - 1)
        def _():
            y = o_ref[...]
            o_ref[...] = y * _rms_scale(y) * gf_ref[...]


def _mlp(x, gain, w_up, w_down, layer, final_gain, *, tm, tf, final_norm):
    n, d = x.shape
    ff = w_up.shape[2]
    assert n % tm == 0 and ff % tf == 0
    kern = functools.partial(_mlp_kernel, final_norm=final_norm)
    return pl.pallas_call(
        kern,
        grid=(n // tm, ff // tf),
        in_specs=[pl.BlockSpec((tm, d), lambda i, f: (i, 0)),
                  pl.BlockSpec((1, d), lambda i, f: (0, 0)),
                  pl.BlockSpec((None, d, tf), lambda i, f: (layer, 0, f)),
                  pl.BlockSpec((None, tf, d), lambda i, f: (layer, f, 0)),
                  pl.BlockSpec((1, d), lambda i, f: (0, 0))],
        out_specs=pl.BlockSpec((tm, d), lambda i, f: (i, 0)),
        out_shape=jax.ShapeDtypeStruct((n, d), F32),
        scratch_shapes=[pltpu.VMEM((tm, d), BF16)],
        compiler_params=_params("parallel", "arbitrary"),
        name="mlp",
    )(x, gain.reshape(1, d), w_up, w_down, final_gain.reshape(1, d))


def _t5_bucket_np(rel):
    nb = T5_BUCKETS // 2
    max_exact = nb // 2
    base = np.where(rel > 0, nb, 0)
    n = np.abs(rel)
    nf = np.maximum(n, 1).astype(np.float32)
    large = max_exact + (np.log(nf / np.float32(max_exact)) / np.float32(math.log(T5_MAX_DIST / max_exact))
                         * np.float32(nb - max_exact)).astype(np.int32)
    large = np.minimum(large, nb - 1)
    return base + np.where(n < max_exact, n, large)


def _pair_heads(a, axis):
    shape = a.shape
    halves = shape[axis] // (8 * HEAD_DIM)
    a = a.reshape(shape[:axis] + (halves, 2, 4, HEAD_DIM) + shape[axis + 1:])
    return jnp.swapaxes(a, axis + 1, axis + 2).reshape(shape)


def _split_rotary(a, axis):
    shape = a.shape
    a = a.reshape(shape[:axis] + (shape[axis] // LANES, 2, 2, 2, HEAD_DIM // 4) + shape[axis + 1:])
    perm = (tuple(range(axis + 1)) + (axis + 3, axis + 1, axis + 2, axis + 4)
            + tuple(range(axis + 5, a.ndim)))
    return jnp.transpose(a, perm).reshape(shape)


def _first_head_lanes():
    lane = lax.broadcasted_iota(jnp.int32, (1, LANES), 1)
    return (lane // (HEAD_DIM // 2)) % 2 == 0


def _rope_tables(s_len):
    axis_dim = HEAD_DIM // 2
    t = jnp.arange(s_len)
    row = (t // GRID_W).astype(F32)
    col = (t % GRID_W).astype(F32)
    freqs = ROPE_THETA ** (-jnp.arange(0, axis_dim, 2, dtype=F32) / axis_dim)
    ang = jnp.concatenate([row[:, None] * freqs[None, :], col[:, None] * freqs[None, :]], axis=-1)
    sign = np.where(np.arange(LANES) < LANES // 2, -1.0, 1.0).astype(np.float32)
    return jnp.tile(jnp.cos(ang), (1, 4)), jnp.tile(jnp.sin(ang), (1, 4)) * sign


def kernel(x, norm_mix, w_in, a_sink, t5_table, b_rpb, c_q_gain, c_k_gain, out_gain_a, out_gain_b,
           out_gain_c, w_o, norm_mlp, w_up, w_down, norm_final):
    bsz, s_len, d_model = x.shape
    depth = w_in.shape[0]
    n = bsz * s_len
    scale = HEAD_DIM ** -0.5

    a_w = out_gain_a.shape[1]
    b_w = out_gain_b.shape[1]
    c_w = out_gain_c.shape[1]
    a_kv_w = a_w // 4
    c_kv_w = c_w // 4
    off_qb = a_w + 2 * a_kv_w
    off_qc = off_qb + 3 * b_w
    in_width = off_qc + c_w + 2 * c_kv_w
    assert in_width == w_in.shape[2] and a_w == 8 * HEAD_DIM and c_w == 16 * HEAD_DIM

    off_vc = off_qc + c_w + c_kv_w
    w_in_p = jnp.concatenate(
        [_pair_heads(w_in[..., :a_w], 2) * (scale * LOG2E), w_in[..., a_w:off_qb],
         w_in[..., off_qb:off_qb + b_w] * (scale * LOG2E), w_in[..., off_qb + b_w:off_qc],
         _split_rotary(_pair_heads(w_in[..., off_qc:off_qc + c_w], 2), 2),
         _split_rotary(w_in[..., off_qc + c_w:off_vc], 2), w_in[..., off_vc:]], axis=2).astype(BF16)
    w_o_p = jnp.concatenate(
        [_pair_heads(w_o[:, :a_w], 1), w_o[:, a_w:a_w + b_w], _pair_heads(w_o[:, a_w + b_w:], 1)],
        axis=1).astype(BF16)
    w_up_b = w_up.astype(BF16)
    w_down_b = w_down.astype(BF16)

    qi = np.arange(A_BLOCK)[:, None]
    kj = np.arange(3 * A_BLOCK)[None, :]
    bucket = _t5_bucket_np(kj - A_BLOCK - qi)
    bias_a = t5_table[bucket].astype(F32) * LOG2E
    bias_t = bias_a.reshape(A_BLOCK, 3 * A_BLOCK, 2, 4).transpose(2, 1, 3, 0)
    bias_t = bias_t.reshape(2, 3 * A_BLOCK, 4 * A_BLOCK)
    sink_t = jnp.repeat(a_sink.astype(F32) * LOG2E, A_BLOCK, axis=1).reshape(depth, 2, 1, 4 * A_BLOCK)

    cq = np.arange(GRID_W)[:, None]
    ck = np.arange(GRID_W)[None, :]
    dc = np.clip(ck - cq + NA_COLS - 1, 0, 2 * NA_COLS - 2)
    t_blocks = b_rpb[:, :, :, dc].astype(F32) * LOG2E
    t_pair = jnp.concatenate([t_blocks[:, :, :-1], t_blocks[:, :, 1:]], axis=-1)

    cos_t, sin_t = _rope_tables(s_len)
    two = lambda g: _split_rotary(jnp.concatenate([g, g], axis=-1), 0)
    n_qblk, n_kblk = c_w // LANES, c_kv_w // LANES
    head_of_lane = (np.arange(LANES) // (HEAD_DIM // 2)) % 2
    same_head = jnp.asarray(np.tile(head_of_lane[:, None] == head_of_lane[None, :], (2, 1)), BF16)

    xf = x.reshape(n, d_model)
    for l in range(depth):
        proj = _inproj(xf, norm_mix[l], w_in_p, l, tm=min(1024, n), tn=in_width // 3)
        oa = _attn_a(proj, bias_t, sink_t[l], bsz=bsz, s_len=s_len, tq=min(1024, s_len),
                     k_block=a_w // LANES)
        ob = _attn_b(proj, t_pair[l], bsz=bsz, s_len=s_len, rblk=min(32, s_len // GRID_W),
                     q_block=off_qb // LANES)
        gains = jnp.concatenate([jnp.tile(two(c_q_gain[l] * (scale * LOG2E))[None], (n_qblk, 1)),
                                 jnp.tile(two(c_k_gain[l])[None], (n_kblk, 1))], axis=0)
        qk = _c_prep(proj, gains.reshape(n_qblk + n_kblk, 1, LANES).astype(F32), same_head, cos_t, sin_t,
                     s_len=s_len, tr=min(4096, s_len), first=off_qc // LANES)
        tk_c = min(512, s_len // 2)
        vt = _c_vprep(proj, bsz=bsz, s_len=s_len, tk=tk_c)
        oc = _attn_c(qk, vt, bsz=bsz, s_len=s_len, tq=min(1024, s_len), tk=tk_c)
        xf = _mix_out(xf, oa, ob, oc, _pair_heads(out_gain_a[l], 0), out_gain_b[l],
                      _pair_heads(out_gain_c[l], 0), w_o_p, l, tm=min(512, n))
        xf = _mlp(xf, norm_mlp[l], w_up_b, w_down_b, l, norm_final,
                  tm=min(512, n), tf=1024, final_norm=(l == depth - 1))
    return xf.reshape(bsz, s_len, d_model)
```
